```python
import math, functools
import jax, jax.numpy as jnp
from jax import lax
import numpy as np

D_MODEL = 2048
BATCH = 8
SEQ = 2048
DEPTH = 1
DEC_BATCH = 32
DEC_SEQ = 4
PAST_LEN = 16384
PAGE_SIZE = 128

DA_HEADS = 8
DA_DK = 64
DA_DV = 2 * DA_DK
DA_WIDTH = DA_HEADS * DA_DV
Q_BLOCK = 128
RW_HEAD = 64
RW_WIDTH = D_MODEL - DA_WIDTH
RW_HEADS = RW_WIDTH // RW_HEAD
R_W = 64
R_A = 64
R_G = 160
RW_PROJ = 3 * RW_WIDTH + R_W + R_A + R_G
RW_SPLITS = (RW_WIDTH, 2 * RW_WIDTH, 3 * RW_WIDTH, 3 * RW_WIDTH + R_W, 3 * RW_WIDTH + R_W + R_A)
N_IN = 3 * DA_WIDTH + RW_PROJ
LNX_EPS = 64e-5
PEER_HEADS = 8
N_KEYS = 128
N_EXPERTS = N_KEYS * N_KEYS
PEER_QDIM = 256
PEER_TOPK = 16
PEER_BLOCK = 128
EPS = 1e-6

kernel_name = 'hymba_diffattn_rwkv7_peer_step'


def _rms(x, g, eps=EPS):
    xf = x.astype(jnp.float32)
    y = xf * lax.rsqrt(jnp.mean(xf * xf, axis=-1, keepdims=True) + eps)
    return (y * g.astype(jnp.float32)).astype(x.dtype)


def _attend_prompt(q, k, v):
    b, s = q.shape[:2]
    nb = s // Q_BLOCK
    qb = jnp.moveaxis(q.reshape(b, nb, Q_BLOCK, DA_HEADS, 2, DA_DK), 1, 0)
    kpos = jnp.arange(s)
    vf = v.astype(jnp.float32)

    def block(args):
        qi, i = args
        sc = jnp.einsum('bqhcd,bkhcd->bhcqk', qi, k).astype(jnp.float32)
        qpos = i * Q_BLOCK + jnp.arange(Q_BLOCK)
        mask = kpos[None, :] <= qpos[:, None]
        p = jax.nn.softmax(jnp.where(mask, sc, -jnp.inf), axis=-1)
        return jnp.einsum('bhcqk,bkhd->bqhcd', p, vf)

    out = lax.map(block, (qb, jnp.arange(nb)))
    return jnp.moveaxis(out, 0, 1).reshape(b, s, DA_HEADS, 2, DA_DV)


def _online(carry, sc, v):
    m, l, acc = carry
    m_new = jnp.maximum(m, sc.max(-1))
    corr = jnp.exp(m - m_new)
    p = jnp.exp(sc - m_new[..., None])
    l = l * corr + p.sum(-1)
    acc = acc * corr[..., None] + jnp.einsum('bhctp,bphd->bhctd', p, v.astype(jnp.float32))
    return (m_new, l, acc)


def _attend_sample(q, k, v, cache_k, cache_v, page_table, layer):
    bd, t = q.shape[:2]
    m0 = jnp.full((bd, DA_HEADS, 2, t), -jnp.inf, jnp.float32)
    l0 = jnp.zeros((bd, DA_HEADS, 2, t), jnp.float32)
    a0 = jnp.zeros((bd, DA_HEADS, 2, t, DA_DV), jnp.float32)

    def page_step(carry, pages):
        kp = cache_k[layer, pages]
        vp = cache_v[layer, pages]
        sc = jnp.einsum('bthcd,bphcd->bhctp', q, kp).astype(jnp.float32)
        return _online(carry, sc, vp), None

    carry, _ = lax.scan(page_step, (m0, l0, a0), page_table.T)
    sc = jnp.einsum('bthcd,bshcd->bhcts', q, k).astype(jnp.float32)
    causal = jnp.tril(jnp.ones((t, t), bool))
    m, l, acc = _online(carry, jnp.where(causal, sc, -jnp.inf), v)
    out = acc / l[..., None]
    return out.transpose(0, 3, 1, 2, 4)


def _peer(h, wq, sub_keys, u_tab, v_tab):
    b, t, d = h.shape
    n = b * t
    nb = -(-n // PEER_BLOCK)
    hf = jnp.pad(h.reshape(n, d), ((0, nb * PEER_BLOCK - n), (0, 0))).reshape(nb, PEER_BLOCK, d)

    def block(hb):
        q = (hb @ wq).reshape(PEER_BLOCK, PEER_HEADS, 2, PEER_QDIM // 2)
        sc = jnp.einsum('nhcd,hckd->nhck', q, sub_keys).astype(jnp.float32)
        s_top, i_top = lax.top_k(sc, PEER_TOPK)
        cand = s_top[:, :, 0, :, None] + s_top[:, :, 1, None, :]
        best, flat = lax.top_k(cand.reshape(PEER_BLOCK, PEER_HEADS, PEER_TOPK * PEER_TOPK), PEER_TOPK)
        i1 = jnp.take_along_axis(i_top[:, :, 0, :], flat // PEER_TOPK, axis=-1)
        i2 = jnp.take_along_axis(i_top[:, :, 1, :], flat % PEER_TOPK, axis=-1)
        expert = i1 * N_KEYS + i2
        gate = jax.nn.softmax(best, axis=-1)
        act = jax.nn.gelu(jnp.einsum('nd,nhed->nhe', hb, u_tab[expert]).astype(jnp.float32), approximate=False)
        return jnp.einsum('nhe,nhed->nd', (gate * act).astype(hb.dtype), v_tab[expert])

    out = lax.map(block, hf)
    return out.reshape(nb * PEER_BLOCK, d)[:n].reshape(b, t, d)


def _layer(x, attend, shift0, wkv0, lam_init, norm1_g, w_in, da_qk_g, da_lambda, da_subln_g,
           rw_mu, rw_w0, rw_w2, rw_a0, rw_a2, rw_g2, rw_kk, rw_ka, rw_rk, rw_lnx,
           w_out, norm2_g, peer_wq, peer_keys, peer_u, peer_v):
    f32 = jnp.float32
    b, t, _ = x.shape
    h = _rms(x, norm1_g)
    proj = h @ w_in
    q, k, v, rw = jnp.split(proj, [DA_WIDTH, 2 * DA_WIDTH, 3 * DA_WIDTH], axis=-1)

    q = _rms(q.reshape(b, t, DA_HEADS, 2, DA_DK), da_qk_g[0]) * (DA_DK ** -0.5)
    k = _rms(k.reshape(b, t, DA_HEADS, 2, DA_DK), da_qk_g[1])
    v = v.reshape(b, t, DA_HEADS, DA_DV)
    o = attend(q, k, v)
    lam = (jnp.exp(jnp.sum(da_lambda[0].astype(f32) * da_lambda[1]))
           - jnp.exp(jnp.sum(da_lambda[2].astype(f32) * da_lambda[3])) + lam_init)
    o = o[..., 0, :] - lam * o[..., 1, :]
    o_da = (_rms(o, da_subln_g) * (1.0 - lam_init)).reshape(b, t, DA_WIDTH).astype(x.dtype)

    prev = jnp.concatenate([shift0[:, None, :].astype(rw.dtype), rw[:, :-1]], axis=1)
    xm = rw + (prev - rw) * rw_mu
    r, kr, vr, wd, ad, gd = jnp.split(xm, list(RW_SPLITS), axis=-1)
    w = -jax.nn.softplus(-(rw_w0 + jnp.tanh(wd) @ rw_w2).astype(f32)) - 0.5
    decay = jnp.exp(-jnp.exp(w))
    a = jax.nn.sigmoid((rw_a0 + ad @ rw_a2).astype(f32))
    g = jax.nn.sigmoid(gd) @ rw_g2

    def heads(z):
        return z.astype(f32).reshape(b, t, RW_HEADS, RW_HEAD)

    kk = heads(kr * rw_kk)
    kk = kk / jnp.maximum(jnp.linalg.norm(kk, axis=-1, keepdims=True), 1e-12)
    kh = heads(kr.astype(f32) * (1.0 + (a - 1.0) * rw_ka))
    rh, vh, ah, wh = heads(r), heads(vr), heads(a), heads(decay)

    def step(S, inp):
        r_t, w_t, k_t, v_t, kk_t, a_t = inp
        sa = jnp.einsum('bhvk,bhk->bhv', S, -kk_t)
        S = (S * w_t[:, :, None, :] + sa[..., None] * (kk_t * a_t)[:, :, None, :]
             + v_t[..., None] * k_t[:, :, None, :])
        return S, jnp.einsum('bhvk,bhk->bhv', S, r_t)

    tm = lambda z: jnp.moveaxis(z, 1, 0)
    wkv, ys = lax.scan(step, wkv0.astype(f32), (tm(rh), tm(wh), tm(kh), tm(vh), tm(kk), tm(ah)))
    y = jnp.moveaxis(ys, 0, 1)
    mu = jnp.mean(y, axis=-1, keepdims=True)
    var = jnp.mean(jnp.square(y - mu), axis=-1, keepdims=True)
    y = ((y - mu) * lax.rsqrt(var + LNX_EPS)).reshape(b, t, RW_WIDTH) * rw_lnx[0] + rw_lnx[1]
    bonus = (jnp.sum(rh * kh * rw_rk, axis=-1, keepdims=True) * vh).reshape(b, t, RW_WIDTH)
    o_rw = ((y + bonus) * g).astype(x.dtype)

    x = x + jnp.concatenate([o_da, o_rw], axis=-1) @ w_out
    x = x + _peer(_rms(x, norm2_g), peer_wq, peer_keys, peer_u, peer_v)
    return x, k, v, wkv.astype(wkv0.dtype), rw[:, -1]


def setup_inputs(seed: int = 0) -> dict:
    key = jax.random.key(seed)
    ks = jax.random.split(key, 32)
    f32 = jnp.float32
    n_pages = PAST_LEN // PAGE_SIZE
    n_used = DEC_BATCH * n_pages
    n_pool = n_used + max(1, n_used // 4)
    nrm = lambda k, shape, s: s * jax.random.normal(k, shape, f32)
    gain = lambda k, shape: 1.0 + 0.02 * jax.random.normal(k, shape, f32)
    page_table = jax.random.permutation(ks[0], n_pool)[:n_used].reshape(DEC_BATCH, n_pages).astype(jnp.int32)
    return {
        'x_prompt': nrm(ks[1], (BATCH, SEQ, D_MODEL), 1.0),
        'x_sample': nrm(ks[2], (DEC_BATCH, DEC_SEQ, D_MODEL), 1.0),
        'cache_k': nrm(ks[3], (DEPTH, n_pool, PAGE_SIZE, DA_HEADS, 2, DA_DK), 1.0),
        'cache_v': nrm(ks[4], (DEPTH, n_pool, PAGE_SIZE, DA_HEADS, DA_DV), 1.0),
        'state_wkv': nrm(ks[5], (DEPTH, DEC_BATCH, RW_HEADS, RW_HEAD, RW_HEAD), 0.5),
        'state_shift': nrm(ks[6], (DEPTH, DEC_BATCH, RW_PROJ), 1.0),
        'page_table': page_table,
        'norm1_g': gain(ks[7], (DEPTH, D_MODEL)),
        'w_in': nrm(ks[8], (DEPTH, D_MODEL, N_IN), D_MODEL ** -0.5),
        'da_qk_g': gain(ks[9], (DEPTH, 2, DA_DK)),
        'da_lambda': nrm(ks[10], (DEPTH, 4, DA_DK), 0.1),
        'da_subln_g': gain(ks[11], (DEPTH, DA_DV)),
        'rw_mu': jax.random.uniform(ks[12], (DEPTH, RW_PROJ), f32),
        'rw_w0': jax.random.uniform(ks[13], (DEPTH, RW_WIDTH), f32, -6.0, -1.0),
        'rw_w2': nrm(ks[14], (DEPTH, R_W, RW_WIDTH), 0.1 * R_W ** -0.5),
        'rw_a0': nrm(ks[15], (DEPTH, RW_WIDTH), 0.1),
        'rw_a2': nrm(ks[16], (DEPTH, R_A, RW_WIDTH), 0.1 * R_A ** -0.5),
        'rw_g2': nrm(ks[17], (DEPTH, R_G, RW_WIDTH), R_G ** -0.5),
        'rw_kk': 0.85 + nrm(ks[18], (DEPTH, RW_WIDTH), 0.05),
        'rw_ka': 1.0 + nrm(ks[19], (DEPTH, RW_WIDTH), 0.05),
        'rw_rk': nrm(ks[20], (DEPTH, RW_HEADS, RW_HEAD), 0.1),
        'rw_lnx': jnp.stack([gain(ks[21], (DEPTH, RW_WIDTH)), nrm(ks[22], (DEPTH, RW_WIDTH), 0.02)], axis=1),
        'w_out': nrm(ks[23], (DEPTH, D_MODEL, D_MODEL), D_MODEL ** -0.5),
        'norm2_g': gain(ks[24], (DEPTH, D_MODEL)),
        'peer_wq': nrm(ks[25], (DEPTH, D_MODEL, PEER_HEADS * PEER_QDIM), D_MODEL ** -0.5),
        'peer_keys': nrm(ks[26], (DEPTH, PEER_HEADS, 2, N_KEYS, PEER_QDIM // 2), (PEER_QDIM // 2) ** -0.5),
        'peer_u': nrm(ks[27], (DEPTH, N_EXPERTS, D_MODEL), D_MODEL ** -0.5),
        'peer_v': nrm(ks[28], (DEPTH, N_EXPERTS, D_MODEL), PEER_HEADS ** -0.5),
    }


def reference(x_prompt, x_sample, cache_k, cache_v, state_wkv, state_shift, page_table,
              norm1_g, w_in, da_qk_g, da_lambda, da_subln_g, rw_mu, rw_w0, rw_w2, rw_a0, rw_a2,
              rw_g2, rw_kk, rw_ka, rw_rk, rw_lnx, w_out, norm2_g, peer_wq, peer_keys, peer_u, peer_v):
    xp, xs = x_prompt, x_sample
    kp_l, vp_l, wp_l, sp_l, ks_l, vs_l, ws_l, ss_l = [], [], [], [], [], [], [], []
    for l in range(DEPTH):
        lam_init = 0.8 - 0.6 * math.exp(-0.3 * l)
        weights = (norm1_g[l], w_in[l], da_qk_g[l], da_lambda[l], da_subln_g[l],
                   rw_mu[l], rw_w0[l], rw_w2[l], rw_a0[l], rw_a2[l], rw_g2[l], rw_kk[l], rw_ka[l],
                   rw_rk[l], rw_lnx[l], w_out[l], norm2_g[l], peer_wq[l], peer_keys[l], peer_u[l], peer_v[l])
        shift_zero = jnp.zeros((xp.shape[0], RW_PROJ), xp.dtype)
        wkv_zero = jnp.zeros((xp.shape[0], RW_HEADS, RW_HEAD, RW_HEAD), state_wkv.dtype)
        xp, kp, vp, wp, sp = _layer(xp, _attend_prompt, shift_zero, wkv_zero, lam_init, *weights)
        attend_s = functools.partial(_attend_sample, cache_k=cache_k, cache_v=cache_v,
                                     page_table=page_table, layer=l)
        xs, ksm, vsm, wsm, ssm = _layer(xs, attend_s, state_shift[l], state_wkv[l], lam_init, *weights)
        kp_l.append(kp); vp_l.append(vp); wp_l.append(wp); sp_l.append(sp)
        ks_l.append(ksm); vs_l.append(vsm); ws_l.append(wsm); ss_l.append(ssm)
    k_prompt = jnp.stack(kp_l)
    v_prompt = jnp.stack(vp_l)
    wkv_prompt = jnp.stack(wp_l)
    shift_prompt = jnp.stack(sp_l)
    k_sample = jnp.stack(ks_l)
    v_sample = jnp.stack(vs_l)
    wkv_sample = jnp.stack(ws_l)
    shift_sample = jnp.stack(ss_l)
    return (xp, xs, k_prompt, v_prompt, wkv_prompt, shift_prompt, k_sample, v_sample, wkv_sample, shift_sample)
```

```python
import functools
import math

import jax
import jax.numpy as jnp
from jax import lax
from jax.experimental import pallas as pl
from jax.experimental.pallas import tpu as pltpu

F32 = jnp.float32
BF16 = jnp.bfloat16

LANES = 128
RMS_EPS = 1e-6
LNX_EPS = 64e-5
PEER_TOPK = 16
PAGE_SIZE = 128
RW_CHUNK = 64
VMEM_LIMIT = 48 * 1024 * 1024

_NT = (((1,), (1,)), ((), ()))
_TN = (((0,), (0,)), ((), ()))


def _params(sem):
    return pltpu.CompilerParams(dimension_semantics=sem, vmem_limit_bytes=VMEM_LIMIT)


def _mm(a, b):
    return jnp.dot(a.astype(BF16), b.astype(BF16), preferred_element_type=F32)


def _mm_nt(a, b):
    return lax.dot_general(a.astype(BF16), b.astype(BF16), _NT, preferred_element_type=F32)


def _mm_tn(a, b):
    return lax.dot_general(a.astype(BF16), b.astype(BF16), _TN, preferred_element_type=F32)


def _mm_split(a_exact_bf16, b):
    hi = b.astype(BF16)
    lo = (b - hi.astype(F32)).astype(BF16)
    return (jnp.dot(a_exact_bf16, hi, preferred_element_type=F32)
            + jnp.dot(a_exact_bf16, lo, preferred_element_type=F32))


def _mm_split_lhs(a, b_exact_bf16):
    hi = a.astype(BF16)
    lo = (a - hi.astype(F32)).astype(BF16)
    return (jnp.dot(hi, b_exact_bf16, preferred_element_type=F32)
            + jnp.dot(lo, b_exact_bf16, preferred_element_type=F32))


def _block_diag_ones(n, group):
    i = jnp.arange(n)
    return (i[:, None] // group == i[None, :] // group).astype(BF16)


def _rms_kernel(x_ref, g_ref, o_ref):
    x = x_ref[...]
    ms = jnp.mean(x * x, axis=-1, keepdims=True)
    o_ref[...] = (x * lax.rsqrt(ms + RMS_EPS) * g_ref[...]).astype(o_ref.dtype)


def _rms(x, g, tm):
    n, d = x.shape
    return pl.pallas_call(
        _rms_kernel,
        grid=(n // tm,),
        in_specs=[pl.BlockSpec((tm, d), lambda i: (i, 0)), pl.BlockSpec((1, d), lambda i: (0, 0))],
        out_specs=pl.BlockSpec((tm, d), lambda i: (i, 0)),
        out_shape=jax.ShapeDtypeStruct((n, d), BF16),
        compiler_params=_params(("parallel",)),
        name="rms",
    )(x, g.reshape(1, d))


def _proj_plain_kernel(h_ref, w_ref, o_ref):
    o_ref[...] = jnp.dot(h_ref[...], w_ref[...], preferred_element_type=F32)


def _proj_dual_kernel(h_ref, w_ref, o_ref, ob_ref):
    y = jnp.dot(h_ref[...], w_ref[...], preferred_element_type=F32)
    o_ref[...] = y
    ob_ref[...] = y.astype(BF16)


def _qk_normed(h_ref, w_ref, bd_ref, g_ref, group):
    y = jnp.dot(h_ref[...], w_ref[...], preferred_element_type=F32)
    ms = _mm_split_lhs(y * y, bd_ref[...]) * (1.0 / group)
    return y * lax.rsqrt(ms + RMS_EPS) * g_ref[...]


def _proj_q_kernel(h_ref, w_ref, bd_ref, g_ref, ob_ref, *, group, scale):
    ob_ref[...] = (_qk_normed(h_ref, w_ref, bd_ref, g_ref, group) * scale).astype(BF16)


def _proj_k_kernel(h_ref, w_ref, bd_ref, g_ref, o_ref, ob_ref, *, group):
    y = _qk_normed(h_ref, w_ref, bd_ref, g_ref, group)
    o_ref[...] = y
    ob_ref[...] = y.astype(BF16)


def _proj(kind, h, w, tm, tn, gain=None, group=None, scale=None):
    n, d = h.shape
    ncol = w.shape[1]
    grid = (n // tm, ncol // tn)
    h_spec = pl.BlockSpec((tm, d), lambda i, j: (i, 0))
    w_spec = pl.BlockSpec((d, tn), lambda i, j: (0, j))
    o_spec = pl.BlockSpec((tm, tn), lambda i, j: (i, j))
    f32_out = jax.ShapeDtypeStruct((n, ncol), F32)
    bf_out = jax.ShapeDtypeStruct((n, ncol), BF16)
    if kind in ("q", "k"):
        bd = _block_diag_ones(tn, group)
        extra = [bd, gain.reshape(1, ncol)]
        extra_specs = [pl.BlockSpec((tn, tn), lambda i, j: (0, 0)), pl.BlockSpec((1, tn), lambda i, j: (0, j))]
    else:
        extra, extra_specs = [], []
    if kind == "q":
        body = functools.partial(_proj_q_kernel, group=group, scale=scale)
        out_shape, out_specs = bf_out, o_spec
    elif kind == "k":
        body = functools.partial(_proj_k_kernel, group=group)
        out_shape, out_specs = (f32_out, bf_out), (o_spec, o_spec)
    elif kind == "dual":
        body = _proj_dual_kernel
        out_shape, out_specs = (f32_out, bf_out), (o_spec, o_spec)
    else:
        body = _proj_plain_kernel
        out_shape, out_specs = f32_out, o_spec
    return pl.pallas_call(
        body,
        grid=grid,
        in_specs=[h_spec, w_spec] + extra_specs,
        out_specs=out_specs,
        out_shape=out_shape,
        compiler_params=_params(("parallel", "arbitrary")),
        name="proj_" + kind,
    )(h, w, *extra)


def _lambda(lam_ref, lam_init):
    ll = lam_ref[...]
    s01 = jnp.sum(ll[0:1, :] * ll[1:2, :], axis=-1, keepdims=True)
    s23 = jnp.sum(ll[2:3, :] * ll[3:4, :], axis=-1, keepdims=True)
    return jnp.exp(s01) - jnp.exp(s23) + lam_init


def _subln(o, g, lam_init):
    ms = jnp.mean(o * o, axis=-1, keepdims=True)
    return (o * lax.rsqrt(ms + RMS_EPS) * g) * (1.0 - lam_init)


def _attn_prompt_kernel(q_ref, k_ref, v_ref, lam_ref, g_ref, o_ref, acc0_ref, acc1_ref, *, tq, dk, lam_init):
    qi = pl.program_id(2)
    q = q_ref[0]
    lane = lax.broadcasted_iota(jnp.int32, q.shape, 1)
    qf = q.astype(F32)
    qc = (jnp.where(lane < dk, qf, 0.0).astype(BF16), jnp.where(lane >= dk, qf, 0.0).astype(BF16))
    accs = (acc0_ref, acc1_ref)
    acc0_ref[...] = jnp.zeros_like(acc0_ref)
    acc1_ref[...] = jnp.zeros_like(acc1_ref)
    row = qi * tq + lax.broadcasted_iota(jnp.int32, (tq, tq), 0)
    col0 = lax.broadcasted_iota(jnp.int32, (tq, tq), 1)

    def body(j, carry):
        kb = k_ref[0, pl.ds(pl.multiple_of(j * tq, tq), tq), :]
        vb = v_ref[0, pl.ds(pl.multiple_of(j * tq, tq), tq), :]
        mask = (j * tq + col0) <= row
        new = []
        for c in range(2):
            m, l = carry[2 * c], carry[2 * c + 1]
            s = lax.dot_general(qc[c], kb, _NT, preferred_element_type=F32)
            s = jnp.where(mask, s, -jnp.inf)
            m_new = jnp.maximum(m, jnp.max(s, axis=-1, keepdims=True))
            corr = jnp.exp(m - m_new)
            p = jnp.exp(s - m_new)
            l = l * corr + jnp.sum(p, axis=-1, keepdims=True)
            accs[c][...] = accs[c][...] * corr + jnp.dot(p.astype(BF16), vb, preferred_element_type=F32)
            new += [m_new, l]
        return tuple(new)

    minit = jnp.full((tq, 1), -jnp.inf, F32)
    linit = jnp.zeros((tq, 1), F32)
    _, l0, _, l1 = lax.fori_loop(0, qi + 1, body, (minit, linit, minit, linit))
    lam = _lambda(lam_ref, lam_init)
    o = acc0_ref[...] / l0 - lam * (acc1_ref[...] / l1)
    o_ref[0] = _subln(o, g_ref[...], lam_init).astype(o_ref.dtype)


def _attn_prompt(qb, kb, vb, da_lambda, subln_g, heads, lam_init, tq):
    b, s, w = qb.shape
    dv = w // heads
    body = functools.partial(_attn_prompt_kernel, tq=tq, dk=dv // 2, lam_init=lam_init)
    return pl.pallas_call(
        body,
        grid=(b, heads, s // tq),
        in_specs=[
            pl.BlockSpec((1, tq, dv), lambda bi, h, qi: (bi, qi, h)),
            pl.BlockSpec((1, s, dv), lambda bi, h, qi: (bi, 0, h)),
            pl.BlockSpec((1, s, dv), lambda bi, h, qi: (bi, 0, h)),
            pl.BlockSpec(da_lambda.shape, lambda bi, h, qi: (0, 0)),
            pl.BlockSpec((1, dv), lambda bi, h, qi: (0, 0)),
        ],
        out_specs=pl.BlockSpec((1, tq, dv), lambda bi, h, qi: (bi, qi, h)),
        out_shape=jax.ShapeDtypeStruct((b, s, w), BF16),
        scratch_shapes=[pltpu.VMEM((tq, dv), F32), pltpu.VMEM((tq, dv), F32)],
        compiler_params=_params(("parallel", "parallel", "arbitrary")),
        name="attn_prompt",
    )(qb, kb, vb, da_lambda, subln_g.reshape(1, dv))


def _attn_sample_kernel(pt_ref, qrep_ref, ks_ref, vs_ref, lam_ref, g_ref, *rest,
                        pp, heads, t, dk, lam_init):
    kc_refs = rest[:pp]
    vc_refs = rest[pp:2 * pp]
    o_ref = rest[2 * pp]
    qbd_ref, m_ref, l_ref, acc_ref = rest[2 * pp + 1:]
    del pt_ref
    p_idx = pl.program_id(1)
    rows = 2 * t * heads
    w = heads * 2 * dk

    @pl.when(p_idx == 0)
    def _():
        r = lax.broadcasted_iota(jnp.int32, (rows, w), 0)
        col = lax.broadcasted_iota(jnp.int32, (rows, w), 1)
        keep = (col // dk) == ((r % heads) * 2 + r // (t * heads))
        qbd_ref[...] = jnp.where(keep, qrep_ref[0].astype(F32), 0.0).astype(BF16)
        m_ref[...] = jnp.full(m_ref.shape, -jnp.inf, F32)
        l_ref[...] = jnp.zeros(l_ref.shape, F32)
        acc_ref[...] = jnp.zeros(acc_ref.shape, F32)

    qbd = qbd_ref[...]

    def online(s, vbs):
        m = m_ref[...]
        m_new = jnp.maximum(m, jnp.max(s, axis=-1, keepdims=True))
        corr = jnp.exp(m - m_new)
        p = jnp.exp(s - m_new)
        l_ref[...] = l_ref[...] * corr + jnp.sum(p, axis=-1, keepdims=True)
        pv = None
        for i, vb in enumerate(vbs):
            n = vb.shape[0]
            d = jnp.dot(p[:, i * n:(i + 1) * n].astype(BF16), vb, preferred_element_type=F32)
            pv = d if pv is None else pv + d
        acc_ref[...] = acc_ref[...] * corr + pv
        m_ref[...] = m_new

    s_pages = [lax.dot_general(qbd, kc[0].astype(BF16), _NT, preferred_element_type=F32) for kc in kc_refs]
    online(jnp.concatenate(s_pages, axis=1), [vc[0].astype(BF16) for vc in vc_refs])

    @pl.when(p_idx == pl.num_programs(1) - 1)
    def _():
        ks = ks_ref[0]
        s = lax.dot_general(qbd, ks, _NT, preferred_element_type=F32)
        r = lax.broadcasted_iota(jnp.int32, s.shape, 0)
        col = lax.broadcasted_iota(jnp.int32, s.shape, 1)
        tok = (r // heads) % t
        s = jnp.where(col <= tok, s, -jnp.inf)
        online(s, [vs_ref[0]])
        out = acc_ref[...] / l_ref[...]
        dv = 2 * dk
        hrow = lax.broadcasted_iota(jnp.int32, (heads, w), 0)
        hcol = lax.broadcasted_iota(jnp.int32, (heads, w), 1) // dv
        pick = hrow == hcol
        lam = _lambda(lam_ref, lam_init)
        toks = []
        for tok_i in range(t):
            maps = []
            for c in range(2):
                base = (c * t + tok_i) * heads
                tile = out[base:base + heads, :]
                maps.append(jnp.sum(jnp.where(pick, tile, 0.0), axis=0, keepdims=True))
            toks.append(maps[0] - lam * maps[1])
        o = jnp.concatenate(toks, axis=0)
        g = g_ref[...]
        outs = [_subln(o[:, h * dv:(h + 1) * dv], g, lam_init) for h in range(heads)]
        o_ref[0] = jnp.concatenate(outs, axis=1).astype(o_ref.dtype)


def _attn_sample(qb, kb, vb, cache_k, cache_v, page_table, da_lambda, subln_g, heads, lam_init, pp):
    bd, t, w = qb.shape
    dk = w // heads // 2
    n_pool = cache_k.shape[0]
    n_pages = page_table.shape[1]
    rows = 2 * t * heads
    qrep = jnp.tile(jnp.repeat(qb, heads, axis=1), (1, 2, 1))
    pad = ((0, 0), (0, PAGE_SIZE - t), (0, 0))
    ks_pad = jnp.pad(kb, pad)
    vs_pad = jnp.pad(vb, pad)
    ck = cache_k.reshape(n_pool, PAGE_SIZE, w)
    cv = cache_v.reshape(n_pool, PAGE_SIZE, w)

    def page_spec(i):
        return pl.BlockSpec((1, PAGE_SIZE, w), lambda b, p, pt: (pt[b, p * pp + i], 0, 0))

    body = functools.partial(_attn_sample_kernel, pp=pp, heads=heads, t=t, dk=dk, lam_init=lam_init)
    grid_spec = pltpu.PrefetchScalarGridSpec(
        num_scalar_prefetch=1,
        grid=(bd, n_pages // pp),
        in_specs=[
            pl.BlockSpec((1, rows, w), lambda b, p, pt: (b, 0, 0)),
            pl.BlockSpec((1, PAGE_SIZE, w), lambda b, p, pt: (b, 0, 0)),
            pl.BlockSpec((1, PAGE_SIZE, w), lambda b, p, pt: (b, 0, 0)),
            pl.BlockSpec(da_lambda.shape, lambda b, p, pt: (0, 0)),
            pl.BlockSpec((1, 2 * dk), lambda b, p, pt: (0, 0)),
        ] + [page_spec(i) for i in range(pp)] + [page_spec(i) for i in range(pp)],
        out_specs=pl.BlockSpec((1, t, w), lambda b, p, pt: (b, 0, 0)),
        scratch_shapes=[pltpu.VMEM((rows, w), BF16), pltpu.VMEM((rows, 1), F32),
                        pltpu.VMEM((rows, 1), F32), pltpu.VMEM((rows, w), F32)],
    )
    return pl.pallas_call(
        body,
        grid_spec=grid_spec,
        out_shape=jax.ShapeDtypeStruct((bd, t, w), BF16),
        compiler_params=_params(("parallel", "arbitrary")),
        name="attn_sample",
    )(page_table, qrep, ks_pad, vs_pad, da_lambda, subln_g.reshape(1, 2 * dk),
      *([ck] * pp), *([cv] * pp))


def _softplus(z):
    return jnp.maximum(z, 0.0) + jnp.log1p(jnp.exp(-jnp.abs(z)))


def _sigmoid(z):
    return 1.0 / (1.0 + jnp.exp(-z))


def _split_heads(x, first):
    zero = jnp.zeros_like(x)
    return jnp.concatenate([jnp.where(first, x, zero), jnp.where(first, zero, x)], axis=0)


def _rwkv_kernel(rw_ref, shift_ref, s0_ref, mu_ref, w0_ref, a0_ref, wwa_ref, g2_ref, kk_ref, ka_ref,
                 rk_ref, lng_ref, lnb_ref, tril_ref, bd_ref, o_ref, sout_ref, state_ref, prev_ref,
                 *, wd, r_w, t_real, head):
    c_idx = pl.program_id(1)
    cl = RW_CHUNK
    n_pairs = wd // LANES

    @pl.when(c_idx == 0)
    def _():
        state_ref[...] = s0_ref[0]
        prev_ref[...] = shift_ref[0]

    x = rw_ref[0]
    rowp = lax.broadcasted_iota(jnp.int32, x.shape, 0)
    prevx = jnp.where(rowp == 0, prev_ref[...], pltpu.roll(x, 1, axis=0))
    prev_ref[...] = x[cl - 1:cl, :]
    xm = x + (prevx - x) * mu_ref[...]
    r = xm[:, 0:wd]
    kr = xm[:, wd:2 * wd]
    vr = xm[:, 2 * wd:3 * wd]
    slab = xm[:, 3 * wd:3 * wd + LANES]
    lane_s = lax.broadcasted_iota(jnp.int32, slab.shape, 1)
    wa = _mm(jnp.where(lane_s < r_w, jnp.tanh(slab), slab), wwa_ref[...])
    gd = xm[:, 3 * wd + LANES:]
    g = _mm(_sigmoid(gd), g2_ref[...])
    w = -_softplus(-(w0_ref[...] + wa[:, :wd])) - 0.5
    lw = -jnp.exp(w)
    a = _sigmoid(a0_ref[...] + wa[:, wd:])
    kk = kr * kk_ref[...]
    kh = kr * (1.0 + (a - 1.0) * ka_ref[...])
    beta_scale = a
    if t_real < cl:
        valid = lax.broadcasted_iota(jnp.int32, (cl, wd), 0) < t_real
        lw = jnp.where(valid, lw, 0.0)
        beta_scale = jnp.where(valid, a, 0.0)
        kh_state = jnp.where(valid, kh, 0.0)
    else:
        kh_state = kh
    cum = _mm_split(tril_ref[...], lw)
    cum_last = cum[cl - 1:cl, :]
    e_neg = jnp.exp(-cum)
    e_end = jnp.exp(cum_last - cum)
    w_end = jnp.exp(cum_last)
    rt_all = r * jnp.exp(cum)
    ex_all = jnp.exp(cum - lw)

    lane = lax.broadcasted_iota(jnp.int32, (cl, LANES), 1)
    first = lane < head
    tt = lax.broadcasted_iota(jnp.int32, (cl, 2 * cl), 0)
    ss = lax.broadcasted_iota(jnp.int32, (cl, 2 * cl), 1) % cl
    strict = ss < tt
    incl = ss <= tt
    firstw = lax.broadcasted_iota(jnp.int32, (cl, 2 * cl), 1) < cl
    r2 = lax.broadcasted_iota(jnp.int32, (2 * cl, 2 * cl), 0)
    c2 = lax.broadcasted_iota(jnp.int32, (2 * cl, 2 * cl), 1)
    eye = (r2 == c2).astype(F32)
    rs = lax.broadcasted_iota(jnp.int32, (LANES, LANES), 0) // head
    cs = lax.broadcasted_iota(jnp.int32, (LANES, LANES), 1) // head
    same_head = rs == cs
    bd = bd_ref[...]
    n_levels = int(math.log2(cl))

    for p in range(n_pairs):
        sl = slice(p * LANES, (p + 1) * LANES)
        kk_p = kk[:, sl]
        nrm = jnp.sqrt(_mm(kk_p * kk_p, bd))
        kkn = kk_p / jnp.maximum(nrm, 1e-12)
        beta = kkn * beta_scale[:, sl]
        at = -ex_all[:, sl] * kkn
        bt = beta * e_neg[:, sl]
        kt = kh_state[:, sl] * e_neg[:, sl]
        rt = rt_all[:, sl]
        v_p = vr[:, sl]
        lhs = jnp.concatenate([at, rt], axis=0)
        rhs = jnp.concatenate([_split_heads(bt, first), _split_heads(kt, first)], axis=0)
        m = _mm_nt(lhs, rhs)
        aab = jnp.where(strict, m[:cl, :2 * cl], 0.0)
        aak = jnp.where(strict, m[:cl, 2 * cl:], 0.0)
        arb = jnp.where(incl, m[cl:, :2 * cl], 0.0)
        ark = jnp.where(incl, m[cl:, 2 * cl:], 0.0)
        abd = _split_heads(aab, firstw)
        tinv = eye + abd
        xp = _mm(abd, abd)
        for lvl in range(1, n_levels):
            if lvl < n_levels - 1:
                both = _mm(xp, jnp.concatenate([tinv, xp], axis=1))
                tinv = tinv + both[:, :2 * cl]
                xp = both[:, 2 * cl:]
            else:
                tinv = tinv + _mm(xp, tinv)
        tw = tinv[:cl, :] + tinv[cl:, :]
        st = state_ref[p]
        sa = _mm_nt(lhs, st)
        vbd = _split_heads(v_p, first)
        rhs_u = sa[:cl, :] + _mm(aak, vbd)
        u = _mm(tw, _split_heads(rhs_u, first))
        ubd = _split_heads(u, first)
        y = sa[cl:, :] + _mm(jnp.concatenate([arb, ark], axis=1), jnp.concatenate([ubd, vbd], axis=0))
        ds = _mm_tn(jnp.concatenate([u, v_p], axis=0),
                    jnp.concatenate([beta * e_end[:, sl], kh_state[:, sl] * e_end[:, sl]], axis=0))
        state_ref[p] = st * w_end[:, sl] + jnp.where(same_head, ds, 0.0)
        mean = _mm(y, bd) * (1.0 / head)
        dlt = y - mean
        var = _mm(dlt * dlt, bd) * (1.0 / head)
        yn = dlt * lax.rsqrt(var + LNX_EPS) * lng_ref[:, sl] + lnb_ref[:, sl]
        bonus = _mm(r[:, sl] * kh[:, sl] * rk_ref[:, sl], bd) * v_p
        o_ref[0, :, sl] = ((yn + bonus) * g[:, sl]).astype(o_ref.dtype)

    @pl.when(c_idx == pl.num_programs(1) - 1)
    def _():
        sout_ref[0] = state_ref[...]


def _rwkv(rw, shift0, s0, prm, t_real):
    b, s, p = rw.shape
    wd = prm["w0"].shape[1]
    n_pairs = wd // LANES
    cl = RW_CHUNK
    body = functools.partial(_rwkv_kernel, wd=wd, r_w=prm["r_w"], t_real=t_real, head=prm["head"])

    def const(shape):
        return pl.BlockSpec(shape, lambda bi, ci: (0,) * len(shape))

    names = ["mu", "w0", "a0", "wwa", "g2", "kk", "ka", "rk", "lng", "lnb", "tril", "bd"]
    return pl.pallas_call(
        body,
        grid=(b, s // cl),
        in_specs=[
            pl.BlockSpec((1, cl, p), lambda bi, ci: (bi, ci, 0)),
            pl.BlockSpec((1, 1, p), lambda bi, ci: (bi, 0, 0)),
            pl.BlockSpec((1, n_pairs, LANES, LANES), lambda bi, ci: (bi, 0, 0, 0)),
        ] + [const(prm[nm].shape) for nm in names],
        out_specs=(pl.BlockSpec((1, cl, wd), lambda bi, ci: (bi, ci, 0)),
                   pl.BlockSpec((1, n_pairs, LANES, LANES), lambda bi, ci: (bi, 0, 0, 0))),
        out_shape=(jax.ShapeDtypeStruct((b, s, wd), BF16),
                   jax.ShapeDtypeStruct((b, n_pairs, LANES, LANES), F32)),
        scratch_shapes=[pltpu.VMEM((n_pairs, LANES, LANES), F32), pltpu.VMEM((1, p), F32)],
        compiler_params=_params(("parallel", "arbitrary")),
        name="rwkv",
    )(rw, shift0, s0, *[prm[nm] for nm in names])


def _pack_state(wkv, head):
    b, h = wkv.shape[:2]
    pairs = wkv.reshape(b, h // 2, 2, head, head)
    z = jnp.zeros((b, h // 2, head, head), wkv.dtype)
    top = jnp.concatenate([pairs[:, :, 0], z], axis=-1)
    bot = jnp.concatenate([z, pairs[:, :, 1]], axis=-1)
    return jnp.concatenate([top, bot], axis=-2)


def _unpack_state(sbd, head):
    b, n_pairs = sbd.shape[:2]
    h0 = sbd[:, :, :head, :head]
    h1 = sbd[:, :, head:, head:]
    return jnp.stack([h0, h1], axis=2).reshape(b, 2 * n_pairs, head, head)


def _outproj_kernel(x_ref, oda_ref, orw_ref, w1_ref, w2_ref, g_ref, x1_ref, ht_ref):
    y = (jnp.dot(oda_ref[...], w1_ref[...], preferred_element_type=F32)
         + jnp.dot(orw_ref[...], w2_ref[...], preferred_element_type=F32))
    x1 = x_ref[...] + y
    x1_ref[...] = x1
    ms = jnp.mean(x1 * x1, axis=-1, keepdims=True)
    h2 = x1 * lax.rsqrt(ms + RMS_EPS) * g_ref[...]
    ht_ref[...] = h2.T.astype(BF16)


def _outproj(x, oda, orw, w1, w2, g, tm):
    n, d = x.shape
    wa, wr = oda.shape[1], orw.shape[1]
    return pl.pallas_call(
        _outproj_kernel,
        grid=(n // tm,),
        in_specs=[
            pl.BlockSpec((tm, d), lambda i: (i, 0)),
            pl.BlockSpec((tm, wa), lambda i: (i, 0)),
            pl.BlockSpec((tm, wr), lambda i: (i, 0)),
            pl.BlockSpec((wa, d), lambda i: (0, 0)),
            pl.BlockSpec((wr, d), lambda i: (0, 0)),
            pl.BlockSpec((1, d), lambda i: (0, 0)),
        ],
        out_specs=(pl.BlockSpec((tm, d), lambda i: (i, 0)), pl.BlockSpec((d, tm), lambda i: (0, i))),
        out_shape=(jax.ShapeDtypeStruct((n, d), F32), jax.ShapeDtypeStruct((d, n), BF16)),
        compiler_params=_params(("parallel",)),
        name="outproj",
    )(x, oda, orw, w1, w2, g.reshape(1, d))


def _peer_scores_kernel(ht_ref, wqt_ref, keys_ref, s_ref):
    q = jnp.dot(wqt_ref[...], ht_ref[...], preferred_element_type=F32)
    s_ref[0] = jnp.dot(keys_ref[0], q.astype(BF16), preferred_element_type=F32)


def _peer_scores(ht, wqt, keys, tn):
    d, n = ht.shape
    nhc, n_keys, qd = keys.shape
    return pl.pallas_call(
        _peer_scores_kernel,
        grid=(n // tn, nhc),
        in_specs=[
            pl.BlockSpec((d, tn), lambda t, j: (0, t)),
            pl.BlockSpec((qd, d), lambda t, j: (j, 0)),
            pl.BlockSpec((1, n_keys, qd), lambda t, j: (j, 0, 0)),
        ],
        out_specs=pl.BlockSpec((1, n_keys, tn), lambda t, j: (j, 0, t)),
        out_shape=jax.ShapeDtypeStruct((nhc, n_keys, n), F32),
        compiler_params=_params(("parallel", "arbitrary")),
        name="peer_scores",
    )(ht, wqt, keys)


def _top_desc(s, k):
    cur = jnp.max(s, axis=0, keepdims=True)
    out = [cur]
    for _ in range(k - 1):
        cur = jnp.max(jnp.where(s < cur, s, -jnp.inf), axis=0, keepdims=True)
        out.append(cur)
    return out


def _peer_route_kernel(s_ref, e1_ref, e2_ref, tau_ref):
    s1 = s_ref[0]
    s2 = s_ref[1]
    a = _top_desc(s1, PEER_TOPK)
    b = jnp.concatenate(_top_desc(s2, PEER_TOPK), axis=0)
    cand = jnp.concatenate([ai + b for ai in a], axis=0)
    best = _top_desc(cand, PEER_TOPK)
    tau = best[-1]
    z = sum(jnp.exp(c - best[0]) for c in best)
    e1_ref[0] = jnp.exp(s1 - a[0]) / z
    e2_ref[0] = jnp.exp(s2 - b[0:1, :])
    tau_ref[0] = tau


def _peer_route(scores, tn):
    nhc, n_keys, n = scores.shape
    nh = nhc // 2
    blk = pl.BlockSpec((1, n_keys, tn), lambda t, h: (h, 0, t))
    return pl.pallas_call(
        _peer_route_kernel,
        grid=(n // tn, nh),
        in_specs=[pl.BlockSpec((2, n_keys, tn), lambda t, h: (h, 0, t))],
        out_specs=(blk, blk, pl.BlockSpec((1, 1, tn), lambda t, h: (h, 0, t))),
        out_shape=(jax.ShapeDtypeStruct((nh, n_keys, n), F32), jax.ShapeDtypeStruct((nh, n_keys, n), F32),
                   jax.ShapeDtypeStruct((nh, 1, n), F32)),
        compiler_params=_params(("parallel", "arbitrary")),
        name="peer_route",
    )(scores)


def _gelu(x):
    return 0.5 * x * (1.0 + lax.erf(x * (2.0 ** -0.5)))


def _peer_dense_kernel(ht_ref, x1_ref, s_ref, e1_ref, e2_ref, tau_ref, u_ref, v_ref, o_ref, acc_ref,
                       *, ib, nh, n_keys):
    e_idx = pl.program_id(1)

    @pl.when(e_idx == 0)
    def _():
        acc_ref[...] = jnp.zeros_like(acc_ref)

    act = _gelu(jnp.dot(u_ref[...], ht_ref[...], preferred_element_type=F32))
    gates = []
    for i in range(ib):
        i1 = e_idx * ib + i
        gsum = None
        for h in range(nh):
            s1row = s_ref[2 * h, pl.ds(i1, 1), :]
            e1row = e1_ref[h, pl.ds(i1, 1), :]
            keep = (s1row + s_ref[2 * h + 1]) >= tau_ref[h]
            term = jnp.where(keep, e1row * e2_ref[h], 0.0)
            gsum = term if gsum is None else gsum + term
        gates.append(gsum)
    wt = jnp.concatenate(gates, axis=0) * act
    acc_ref[...] += jnp.dot(wt.T.astype(BF16), v_ref[...], preferred_element_type=F32)

    @pl.when(e_idx == pl.num_programs(1) - 1)
    def _():
        o_ref[...] = x1_ref[...] + acc_ref[...]


def _peer_dense(ht, x1, scores, e1, e2, tau, u_bf, v_bf, tn, ib):
    d, n = ht.shape
    nh, n_keys, _ = e1.shape
    n_exp = u_bf.shape[0]
    ec = ib * n_keys
    once = pl.Buffered(1)
    body = functools.partial(_peer_dense_kernel, ib=ib, nh=nh, n_keys=n_keys)
    return pl.pallas_call(
        body,
        grid=(n // tn, n_exp // ec),
        in_specs=[
            pl.BlockSpec((d, tn), lambda t, e: (0, t)),
            pl.BlockSpec((tn, d), lambda t, e: (t, 0)),
            pl.BlockSpec((2 * nh, n_keys, tn), lambda t, e: (0, 0, t), pipeline_mode=once),
            pl.BlockSpec((nh, n_keys, tn), lambda t, e: (0, 0, t), pipeline_mode=once),
            pl.BlockSpec((nh, n_keys, tn), lambda t, e: (0, 0, t), pipeline_mode=once),
            pl.BlockSpec((nh, 1, tn), lambda t, e: (0, 0, t)),
            pl.BlockSpec((ec, d), lambda t, e: (e, 0)),
            pl.BlockSpec((ec, d), lambda t, e: (e, 0)),
        ],
        out_specs=pl.BlockSpec((tn, d), lambda t, e: (t, 0)),
        out_shape=jax.ShapeDtypeStruct((n, d), F32),
        scratch_shapes=[pltpu.VMEM((tn, d), F32)],
        compiler_params=_params(("parallel", "arbitrary")),
        name="peer_dense",
    )(ht, x1, scores, e1, e2, tau, u_bf, v_bf)


def _tile(n, pref):
    return pref if n % pref == 0 else n


def _layer(x, attend, shift0, s0, t_real, lam_init, lw):
    b, t, d = x.shape
    n = b * t
    xf = x.reshape(n, d)
    w_da = lw["w_q"].shape[1]
    tm = _tile(n, 512)
    h1 = _rms(xf, lw["norm1_g"], tm)
    tmp = _tile(n, 1024)
    qb = _proj("q", h1, lw["w_q"], tmp, _tile(w_da, 512), gain=lw["q_gain"], group=lw["dk"], scale=lw["dk"] ** -0.5)
    k, kb = _proj("k", h1, lw["w_k"], tmp, _tile(w_da, 512), gain=lw["k_gain"], group=lw["dk"])
    v, vb = _proj("dual", h1, lw["w_v"], tmp, _tile(w_da, 512))
    p_pad = lw["w_rw"].shape[1]
    rw = _proj("plain", h1, lw["w_rw"], tmp, p_pad // 3)

    o_da = attend(qb.reshape(b, t, w_da), kb.reshape(b, t, w_da), vb.reshape(b, t, w_da))

    rw3 = rw.reshape(b, t, p_pad)
    if t % RW_CHUNK:
        rw3 = jnp.pad(rw3, ((0, 0), (0, RW_CHUNK - t % RW_CHUNK), (0, 0)))
    o_rw, s_out = _rwkv(rw3, shift0, s0, lw["rwkv"], t_real)
    o_rw = o_rw[:, :t].reshape(n, -1)

    x1, h2t = _outproj(xf, o_da.reshape(n, w_da), o_rw, lw["w_out_da"], lw["w_out_rw"], lw["norm2_g"],
                       _tile(n, 256))
    tn = _tile(n, 512)
    scores = _peer_scores(h2t, lw["wq_t"], lw["peer_keys"], tn)
    e1, e2, tau = _peer_route(scores, tn)
    out = _peer_dense(h2t, x1, scores, e1, e2, tau, lw["peer_u"], lw["peer_v"], tn, 4)
    return out.reshape(b, t, d), k, v, s_out, rw3[:, t - 1, :lw["rw_proj"]]


def kernel(x_prompt, x_sample, cache_k, cache_v, state_wkv, state_shift, page_table, norm1_g, w_in, da_qk_g,
           da_lambda, da_subln_g, rw_mu, rw_w0, rw_w2, rw_a0, rw_a2, rw_g2, rw_kk, rw_ka, rw_rk, rw_lnx,
           w_out, norm2_g, peer_wq, peer_keys, peer_u, peer_v):
    depth = w_in.shape[0]
    d = x_prompt.shape[-1]
    heads, dk = cache_k.shape[3], cache_k.shape[5]
    w_da = heads * 2 * dk
    rw_heads, head = state_wkv.shape[2], state_wkv.shape[3]
    wd = rw_heads * head
    r_w, r_a, r_g = rw_w2.shape[1], rw_a2.shape[1], rw_g2.shape[1]
    rw_proj = 3 * wd + r_w + r_a + r_g
    g_pad = 2 * LANES
    p_pad = 3 * wd + LANES + g_pad
    assert d == w_da + wd and head * 2 == LANES and wd % LANES == 0
    assert r_w + r_a == LANES and r_g <= g_pad and p_pad % 3 == 0 and (p_pad // 3) % LANES == 0
    assert cache_k.shape[2] == PAGE_SIZE
    nh, _, n_keys, qd = peer_keys.shape[1:]
    bsz = x_prompt.shape[0]
    dec_b, dec_t = x_sample.shape[:2]

    xp, xs = x_prompt, x_sample
    outs = [[] for _ in range(8)]
    for l in range(depth):
        lam_init = 0.8 - 0.6 * math.exp(-0.3 * l)
        wl = w_in[l].astype(BF16)
        w_rw = jnp.pad(wl[:, 3 * w_da:], ((0, 0), (0, p_pad - rw_proj)))
        zero_wa = jnp.zeros((r_w, wd), F32)
        wwa = jnp.concatenate([jnp.concatenate([rw_w2[l], zero_wa], axis=1),
                               jnp.concatenate([zero_wa, rw_a2[l]], axis=1)], axis=0).astype(BF16)
        row = lambda z: z.reshape(1, -1)
        rwkv_prm = dict(
            mu=jnp.pad(row(rw_mu[l]), ((0, 0), (0, p_pad - rw_proj))),
            w0=row(rw_w0[l]), a0=row(rw_a0[l]), wwa=wwa,
            g2=jnp.pad(rw_g2[l], ((0, g_pad - r_g), (0, 0))).astype(BF16),
            kk=row(rw_kk[l]), ka=row(rw_ka[l]), rk=row(rw_rk[l]),
            lng=row(rw_lnx[l, 0]), lnb=row(rw_lnx[l, 1]),
            tril=jnp.tril(jnp.ones((RW_CHUNK, RW_CHUNK), BF16)),
            bd=_block_diag_ones(LANES, head), r_w=r_w, head=head)
        lw = dict(
            norm1_g=norm1_g[l], w_q=wl[:, :w_da], w_k=wl[:, w_da:2 * w_da], w_v=wl[:, 2 * w_da:3 * w_da],
            w_rw=w_rw, q_gain=jnp.tile(da_qk_g[l, 0], 2 * heads), k_gain=jnp.tile(da_qk_g[l, 1], 2 * heads),
            dk=dk, rwkv=rwkv_prm, rw_proj=rw_proj,
            w_out_da=w_out[l, :w_da].astype(BF16), w_out_rw=w_out[l, w_da:].astype(BF16),
            norm2_g=norm2_g[l], wq_t=peer_wq[l].T.astype(BF16),
            peer_keys=peer_keys[l].reshape(nh * 2, n_keys, qd).astype(BF16),
            peer_u=peer_u[l].astype(BF16), peer_v=peer_v[l].astype(BF16))

        attend_p = functools.partial(_attn_prompt, da_lambda=da_lambda[l], subln_g=da_subln_g[l], heads=heads,
                                     lam_init=lam_init, tq=_tile(x_prompt.shape[1], 256))
        n_pairs = wd // LANES
        xp, kp, vp, sp, shp = _layer(
            xp, attend_p, jnp.zeros((bsz, 1, p_pad), F32), jnp.zeros((bsz, n_pairs, LANES, LANES), F32),
            RW_CHUNK, lam_init, lw)

        attend_s = functools.partial(_attn_sample, cache_k=cache_k[l], cache_v=cache_v[l], page_table=page_table,
                                     da_lambda=da_lambda[l], subln_g=da_subln_g[l], heads=heads,
                                     lam_init=lam_init, pp=4)
        shift_s = jnp.pad(state_shift[l], ((0, 0), (0, p_pad - rw_proj))).reshape(dec_b, 1, p_pad)
        xs, ksm, vsm, ssm, shs = _layer(xs, attend_s, shift_s, _pack_state(state_wkv[l], head), dec_t, lam_init, lw)

        outs[0].append(kp.reshape(bsz, -1, heads, 2, dk))
        outs[1].append(vp.reshape(bsz, -1, heads, 2 * dk))
        outs[2].append(_unpack_state(sp, head))
        outs[3].append(shp)
        outs[4].append(ksm.reshape(dec_b, dec_t, heads, 2, dk))
        outs[5].append(vsm.reshape(dec_b, dec_t, heads, 2 * dk))
        outs[6].append(_unpack_state(ssm, head))
        outs[7].append(shs)
    return (xp, xs) + tuple(jnp.stack(o) for o in outs)
```

```python
import functools
import math

import jax
import jax.numpy as jnp
from jax import lax
from jax.experimental import pallas as pl
from jax.experimental.pallas import tpu as pltpu

F32 = jnp.float32
BF16 = jnp.bfloat16

LANES = 128
RMS_EPS = 1e-6
LNX_EPS = 64e-5
PEER_TOPK = 16
PAGE_SIZE = 128
RW_CHUNK = 64
VMEM_LIMIT = 48 * 1024 * 1024

_NT = (((1,), (1,)), ((), ()))
_TN = (((0,), (0,)), ((), ()))


def _params(sem):
    return pltpu.CompilerParams(dimension_semantics=sem, vmem_limit_bytes=VMEM_LIMIT)


def _mm(a, b):
    return jnp.dot(a.astype(BF16), b.astype(BF16), preferred_element_type=F32)


def _mm_nt(a, b):
    return lax.dot_general(a.astype(BF16), b.astype(BF16), _NT, preferred_element_type=F32)


def _mm_tn(a, b):
    return lax.dot_general(a.astype(BF16), b.astype(BF16), _TN, preferred_element_type=F32)


def _mm_split(a_exact_bf16, b):
    hi = b.astype(BF16)
    lo = (b - hi.astype(F32)).astype(BF16)
    return (jnp.dot(a_exact_bf16, hi, preferred_element_type=F32)
            + jnp.dot(a_exact_bf16, lo, preferred_element_type=F32))


def _mm_split_lhs(a, b_exact_bf16):
    hi = a.astype(BF16)
    lo = (a - hi.astype(F32)).astype(BF16)
    return (jnp.dot(hi, b_exact_bf16, preferred_element_type=F32)
            + jnp.dot(lo, b_exact_bf16, preferred_element_type=F32))


def _block_diag_ones(n, group):
    i = jnp.arange(n)
    return (i[:, None] // group == i[None, :] // group).astype(BF16)


def _rms_kernel(x_ref, g_ref, o_ref):
    x = x_ref[...]
    ms = jnp.mean(x * x, axis=-1, keepdims=True)
    o_ref[...] = (x * lax.rsqrt(ms + RMS_EPS) * g_ref[...]).astype(o_ref.dtype)


def _rms(x, g, tm):
    n, d = x.shape
    return pl.pallas_call(
        _rms_kernel,
        grid=(n // tm,),
        in_specs=[pl.BlockSpec((tm, d), lambda i: (i, 0)), pl.BlockSpec((1, d), lambda i: (0, 0))],
        out_specs=pl.BlockSpec((tm, d), lambda i: (i, 0)),
        out_shape=jax.ShapeDtypeStruct((n, d), BF16),
        compiler_params=_params(("parallel",)),
        name="rms",
    )(x, g.reshape(1, d))


def _proj_plain_kernel(h_ref, w_ref, o_ref):
    o_ref[...] = jnp.dot(h_ref[...], w_ref[...], preferred_element_type=F32)


def _proj_dual_kernel(h_ref, w_ref, o_ref, ob_ref):
    y = jnp.dot(h_ref[...], w_ref[...], preferred_element_type=F32)
    o_ref[...] = y
    ob_ref[...] = y.astype(BF16)


def _qk_normed(h_ref, w_ref, bd_ref, g_ref, group):
    y = jnp.dot(h_ref[...], w_ref[...], preferred_element_type=F32)
    ms = _mm_split_lhs(y * y, bd_ref[...]) * (1.0 / group)
    return y * lax.rsqrt(ms + RMS_EPS) * g_ref[...]


def _proj_q_kernel(h_ref, w_ref, bd_ref, g_ref, ob_ref, *, group, scale):
    ob_ref[...] = (_qk_normed(h_ref, w_ref, bd_ref, g_ref, group) * scale).astype(BF16)


def _proj_k_kernel(h_ref, w_ref, bd_ref, g_ref, o_ref, ob_ref, *, group):
    y = _qk_normed(h_ref, w_ref, bd_ref, g_ref, group)
    o_ref[...] = y
    ob_ref[...] = y.astype(BF16)


def _proj(kind, h, w, tm, tn, gain=None, group=None, scale=None):
    n, d = h.shape
    ncol = w.shape[1]
    grid = (n // tm, ncol // tn)
    h_spec = pl.BlockSpec((tm, d), lambda i, j: (i, 0))
    w_spec = pl.BlockSpec((d, tn), lambda i, j: (0, j))
    o_spec = pl.BlockSpec((tm, tn), lambda i, j: (i, j))
    f32_out = jax.ShapeDtypeStruct((n, ncol), F32)
    bf_out = jax.ShapeDtypeStruct((n, ncol), BF16)
    if kind in ("q", "k"):
        bd = _block_diag_ones(tn, group)
        extra = [bd, gain.reshape(1, ncol)]
        extra_specs = [pl.BlockSpec((tn, tn), lambda i, j: (0, 0)), pl.BlockSpec((1, tn), lambda i, j: (0, j))]
    else:
        extra, extra_specs = [], []
    if kind == "q":
        body = functools.partial(_proj_q_kernel, group=group, scale=scale)
        out_shape, out_specs = bf_out, o_spec
    elif kind == "k":
        body = functools.partial(_proj_k_kernel, group=group)
        out_shape, out_specs = (f32_out, bf_out), (o_spec, o_spec)
    elif kind == "dual":
        body = _proj_dual_kernel
        out_shape, out_specs = (f32_out, bf_out), (o_spec, o_spec)
    else:
        body = _proj_plain_kernel
        out_shape, out_specs = f32_out, o_spec
    return pl.pallas_call(
        body,
        grid=grid,
        in_specs=[h_spec, w_spec] + extra_specs,
        out_specs=out_specs,
        out_shape=out_shape,
        compiler_params=_params(("parallel", "arbitrary")),
        name="proj_" + kind,
    )(h, w, *extra)


def _lambda(lam_ref, lam_init):
    ll = lam_ref[...]
    s01 = jnp.sum(ll[0:1, :] * ll[1:2, :], axis=-1, keepdims=True)
    s23 = jnp.sum(ll[2:3, :] * ll[3:4, :], axis=-1, keepdims=True)
    return jnp.exp(s01) - jnp.exp(s23) + lam_init


def _subln(o, g, lam_init):
    ms = jnp.mean(o * o, axis=-1, keepdims=True)
    return (o * lax.rsqrt(ms + RMS_EPS) * g) * (1.0 - lam_init)


def _attn_prompt_kernel(q_ref, k_ref, v_ref, lam_ref, g_ref, o_ref, acc0_ref, acc1_ref, *, tq, dk, lam_init):
    qi = pl.program_id(2)
    q = q_ref[0]
    lane = lax.broadcasted_iota(jnp.int32, q.shape, 1)
    qf = q.astype(F32)
    qc = (jnp.where(lane < dk, qf, 0.0).astype(BF16), jnp.where(lane >= dk, qf, 0.0).astype(BF16))
    accs = (acc0_ref, acc1_ref)
    acc0_ref[...] = jnp.zeros_like(acc0_ref)
    acc1_ref[...] = jnp.zeros_like(acc1_ref)
    causal = (lax.broadcasted_iota(jnp.int32, (tq, tq), 1) <= lax.broadcasted_iota(jnp.int32, (tq, tq), 0))

    def step(j, carry, diagonal):
        kb = k_ref[0, pl.ds(pl.multiple_of(j * tq, tq), tq), :]
        vb = v_ref[0, pl.ds(pl.multiple_of(j * tq, tq), tq), :]
        new = []
        for c in range(2):
            m, l = carry[2 * c], carry[2 * c + 1]
            s = lax.dot_general(qc[c], kb, _NT, preferred_element_type=F32)
            if diagonal:
                s = jnp.where(causal, s, -jnp.inf)
            m_new = jnp.maximum(m, jnp.max(s, axis=-1, keepdims=True))
            corr = jnp.exp(m - m_new)
            p = jnp.exp(s - m_new)
            l = l * corr + jnp.sum(p, axis=-1, keepdims=True)
            accs[c][...] = accs[c][...] * corr + jnp.dot(p.astype(BF16), vb, preferred_element_type=F32)
            new += [m_new, l]
        return tuple(new)

    minit = jnp.full((tq, 1), -jnp.inf, F32)
    linit = jnp.zeros((tq, 1), F32)
    carry = lax.fori_loop(0, qi, functools.partial(step, diagonal=False), (minit, linit, minit, linit))
    _, l0, _, l1 = step(qi, carry, True)
    lam = _lambda(lam_ref, lam_init)
    o = acc0_ref[...] / l0 - lam * (acc1_ref[...] / l1)
    o_ref[0] = _subln(o, g_ref[...], lam_init).astype(o_ref.dtype)


def _attn_prompt(qb, kb, vb, da_lambda, subln_g, heads, lam_init, tq):
    b, s, w = qb.shape
    dv = w // heads
    body = functools.partial(_attn_prompt_kernel, tq=tq, dk=dv // 2, lam_init=lam_init)
    return pl.pallas_call(
        body,
        grid=(b, heads, s // tq),
        in_specs=[
            pl.BlockSpec((1, tq, dv), lambda bi, h, qi: (bi, qi, h)),
            pl.BlockSpec((1, s, dv), lambda bi, h, qi: (bi, 0, h)),
            pl.BlockSpec((1, s, dv), lambda bi, h, qi: (bi, 0, h)),
            pl.BlockSpec(da_lambda.shape, lambda bi, h, qi: (0, 0)),
            pl.BlockSpec((1, dv), lambda bi, h, qi: (0, 0)),
        ],
        out_specs=pl.BlockSpec((1, tq, dv), lambda bi, h, qi: (bi, qi, h)),
        out_shape=jax.ShapeDtypeStruct((b, s, w), BF16),
        scratch_shapes=[pltpu.VMEM((tq, dv), F32), pltpu.VMEM((tq, dv), F32)],
        compiler_params=_params(("parallel", "parallel", "arbitrary")),
        name="attn_prompt",
    )(qb, kb, vb, da_lambda, subln_g.reshape(1, dv))


def _attn_sample_kernel(pt_ref, qrep_ref, kst_ref, vs_ref, lam_ref, g_ref, *rest, pp, heads, t, dk, lam_init):
    kc_refs = rest[:pp]
    vc_refs = rest[pp:2 * pp]
    o_ref = rest[2 * pp]
    qbd_ref, m_ref, l_ref, acc_ref = rest[2 * pp + 1:]
    del pt_ref
    p_idx = pl.program_id(1)
    rows = 2 * t * heads
    dv = 2 * dk
    w = heads * dv

    @pl.when(p_idx == 0)
    def _():
        r = lax.broadcasted_iota(jnp.int32, (rows, w), 0)
        col = lax.broadcasted_iota(jnp.int32, (rows, w), 1)
        keep = (col // dk) == ((r % heads) * 2 + r // (t * heads))
        qbd_ref[...] = jnp.where(keep, qrep_ref[0].astype(F32), 0.0).astype(BF16)
        m_ref[...] = jnp.full(m_ref.shape, -jnp.inf, F32)
        l_ref[...] = jnp.zeros(l_ref.shape, F32)
        acc_ref[...] = jnp.zeros(acc_ref.shape, F32)

    qbd = qbd_ref[...]

    def online(s, vbs):
        m = m_ref[...]
        m_new = jnp.maximum(m, jnp.max(s, axis=-1, keepdims=True))
        corr = jnp.exp(m - m_new)
        p = jnp.exp(s - m_new)
        l_ref[...] = l_ref[...] * corr + jnp.sum(p, axis=-1, keepdims=True)
        pv = None
        for i, vb in enumerate(vbs):
            n = vb.shape[0]
            d = jnp.dot(p[:, i * n:(i + 1) * n].astype(BF16), vb, preferred_element_type=F32)
            pv = d if pv is None else pv + d
        acc_ref[...] = acc_ref[...] * corr + pv
        m_ref[...] = m_new

    kt = jnp.concatenate([kc[0] for kc in kc_refs], axis=1).astype(BF16)
    vwide = [jnp.concatenate([vc[0, pl.ds(h, PAGE_SIZE, stride=heads), :] for h in range(heads)], axis=1)
             .astype(BF16) for vc in vc_refs]
    online(jnp.dot(qbd, kt, preferred_element_type=F32), vwide)

    @pl.when(p_idx == pl.num_programs(1) - 1)
    def _():
        s = jnp.dot(qbd, kst_ref[0], preferred_element_type=F32)
        r = lax.broadcasted_iota(jnp.int32, s.shape, 0)
        col = lax.broadcasted_iota(jnp.int32, s.shape, 1)
        online(jnp.where(col <= (r // heads) % t, s, -jnp.inf), [vs_ref[0]])
        out = acc_ref[...] / l_ref[...]
        hrow = lax.broadcasted_iota(jnp.int32, (heads, w), 0)
        hcol = lax.broadcasted_iota(jnp.int32, (heads, w), 1) // dv
        pick = hrow == hcol
        lam = _lambda(lam_ref, lam_init)
        toks = []
        for tok_i in range(t):
            maps = []
            for c in range(2):
                base = (c * t + tok_i) * heads
                maps.append(jnp.sum(jnp.where(pick, out[base:base + heads, :], 0.0), axis=0, keepdims=True))
            toks.append(maps[0] - lam * maps[1])
        o = jnp.concatenate(toks, axis=0)
        g = g_ref[...]
        outs = [_subln(o[:, h * dv:(h + 1) * dv], g, lam_init) for h in range(heads)]
        o_ref[0] = jnp.concatenate(outs, axis=1).astype(o_ref.dtype)


def _attn_sample(qb, kb, vb, cache_k, cache_v, layer, page_table, da_lambda, subln_g, heads, lam_init, pp):
    bd, t, w = qb.shape
    dv = w // heads
    n_pool = cache_k.shape[1]
    n_pages = page_table.shape[1]
    rows = 2 * t * heads
    qrep = jnp.tile(jnp.repeat(qb, heads, axis=1), (1, 2, 1))
    kst_pad = jnp.pad(jnp.swapaxes(kb, 1, 2), ((0, 0), (0, 0), (0, PAGE_SIZE - t)))
    vs_pad = jnp.pad(vb, ((0, 0), (0, PAGE_SIZE - t), (0, 0)))
    ckt = jnp.transpose(cache_k, (0, 1, 3, 4, 5, 2)).reshape(-1, w, PAGE_SIZE)
    cv = cache_v.reshape(-1, PAGE_SIZE * heads, dv)
    base = layer * n_pool

    def k_spec(i):
        return pl.BlockSpec((1, w, PAGE_SIZE), lambda b, p, pt: (base + pt[b, p * pp + i], 0, 0))

    def v_spec(i):
        return pl.BlockSpec((1, PAGE_SIZE * heads, dv), lambda b, p, pt: (base + pt[b, p * pp + i], 0, 0))

    body = functools.partial(_attn_sample_kernel, pp=pp, heads=heads, t=t, dk=dv // 2, lam_init=lam_init)
    grid_spec = pltpu.PrefetchScalarGridSpec(
        num_scalar_prefetch=1,
        grid=(bd, n_pages // pp),
        in_specs=[
            pl.BlockSpec((1, rows, w), lambda b, p, pt: (b, 0, 0)),
            pl.BlockSpec((1, w, PAGE_SIZE), lambda b, p, pt: (b, 0, 0)),
            pl.BlockSpec((1, PAGE_SIZE, w), lambda b, p, pt: (b, 0, 0)),
            pl.BlockSpec(da_lambda.shape, lambda b, p, pt: (0, 0)),
            pl.BlockSpec((1, dv), lambda b, p, pt: (0, 0)),
        ] + [k_spec(i) for i in range(pp)] + [v_spec(i) for i in range(pp)],
        out_specs=pl.BlockSpec((1, t, w), lambda b, p, pt: (b, 0, 0)),
        scratch_shapes=[pltpu.VMEM((rows, w), BF16), pltpu.VMEM((rows, 1), F32),
                        pltpu.VMEM((rows, 1), F32), pltpu.VMEM((rows, w), F32)],
    )
    return pl.pallas_call(
        body,
        grid_spec=grid_spec,
        out_shape=jax.ShapeDtypeStruct((bd, t, w), BF16),
        compiler_params=_params(("parallel", "arbitrary")),
        name="attn_sample",
    )(page_table, qrep, kst_pad, vs_pad, da_lambda, subln_g.reshape(1, dv), *([ckt] * pp), *([cv] * pp))


def _softplus(z):
    return jnp.maximum(z, 0.0) + jnp.log1p(jnp.exp(-jnp.abs(z)))


def _sigmoid(z):
    return 1.0 / (1.0 + jnp.exp(-z))


def _split_heads(x, first):
    zero = jnp.zeros_like(x)
    return jnp.concatenate([jnp.where(first, x, zero), jnp.where(first, zero, x)], axis=0)


def _rwkv_kernel(rw_ref, shift_ref, s0_ref, mu_ref, w0_ref, a0_ref, wwa_ref, g2_ref, kk_ref, ka_ref,
                 rk_ref, lng_ref, lnb_ref, tril_ref, bd_ref, o_ref, sout_ref, state_ref, prev_ref,
                 *, wd, r_w, t_real, head):
    c_idx = pl.program_id(1)
    cl = RW_CHUNK
    pairs = range(wd // LANES)

    @pl.when(c_idx == 0)
    def _():
        state_ref[...] = s0_ref[0]
        prev_ref[...] = shift_ref[0]

    x = rw_ref[0]
    rowp = lax.broadcasted_iota(jnp.int32, x.shape, 0)
    prevx = jnp.where(rowp == 0, prev_ref[...], pltpu.roll(x, 1, axis=0))
    prev_ref[...] = x[cl - 1:cl, :]
    xm = x + (prevx - x) * mu_ref[...]
    r = xm[:, 0:wd]
    kr = xm[:, wd:2 * wd]
    vr = xm[:, 2 * wd:3 * wd]
    slab = xm[:, 3 * wd:3 * wd + LANES]
    lane_s = lax.broadcasted_iota(jnp.int32, slab.shape, 1)
    wa = _mm(jnp.where(lane_s < r_w, jnp.tanh(slab), slab), wwa_ref[...])
    gd = xm[:, 3 * wd + LANES:]
    g = _mm(_sigmoid(gd), g2_ref[...])
    w = -_softplus(-(w0_ref[...] + wa[:, :wd])) - 0.5
    lw = -jnp.exp(w)
    a = _sigmoid(a0_ref[...] + wa[:, wd:])
    kk = kr * kk_ref[...]
    kh = kr * (1.0 + (a - 1.0) * ka_ref[...])
    beta_scale = a
    if t_real < cl:
        valid = lax.broadcasted_iota(jnp.int32, (cl, wd), 0) < t_real
        lw = jnp.where(valid, lw, 0.0)
        beta_scale = jnp.where(valid, a, 0.0)
        kh_state = jnp.where(valid, kh, 0.0)
    else:
        kh_state = kh
    cum = _mm_split(tril_ref[...], lw)
    cum_last = cum[cl - 1:cl, :]
    e_neg = jnp.exp(-cum)
    e_end = jnp.exp(cum_last - cum)
    w_end = jnp.exp(cum_last)
    rt_all = r * jnp.exp(cum)
    ex_all = jnp.exp(cum - lw)
    bonus_in = r * kh * rk_ref[...]

    lane = lax.broadcasted_iota(jnp.int32, (cl, LANES), 1)
    first = lane < head
    tt = lax.broadcasted_iota(jnp.int32, (cl, 2 * cl), 0)
    ss = lax.broadcasted_iota(jnp.int32, (cl, 2 * cl), 1) % cl
    strict = ss < tt
    incl = ss <= tt
    firstw = lax.broadcasted_iota(jnp.int32, (cl, 2 * cl), 1) < cl
    r2 = lax.broadcasted_iota(jnp.int32, (2 * cl, 2 * cl), 0)
    c2 = lax.broadcasted_iota(jnp.int32, (2 * cl, 2 * cl), 1)
    eye = (r2 == c2).astype(F32)
    rs = lax.broadcasted_iota(jnp.int32, (LANES, LANES), 0) // head
    cs = lax.broadcasted_iota(jnp.int32, (LANES, LANES), 1) // head
    same_head = rs == cs
    bd = bd_ref[...]
    n_levels = int(math.log2(cl))
    sls = [slice(p * LANES, (p + 1) * LANES) for p in pairs]

    kk_p = [kk[:, sl] for sl in sls]
    nrm = [jnp.sqrt(_mm(k * k, bd)) for k in kk_p]
    kkn = [k / jnp.maximum(n, 1e-12) for k, n in zip(kk_p, nrm)]
    beta = [kkn[p] * beta_scale[:, sls[p]] for p in pairs]
    v_p = [vr[:, sl] for sl in sls]
    st = [state_ref[p] for p in pairs]
    lhs = [jnp.concatenate([-ex_all[:, sls[p]] * kkn[p], rt_all[:, sls[p]]], axis=0) for p in pairs]
    rhs = [jnp.concatenate([_split_heads(beta[p] * e_neg[:, sls[p]], first),
                            _split_heads(kh_state[:, sls[p]] * e_neg[:, sls[p]], first)], axis=0) for p in pairs]
    m = [_mm_nt(lhs[p], rhs[p]) for p in pairs]
    sa = [_mm_nt(lhs[p], st[p]) for p in pairs]
    aab = [jnp.where(strict, x_[:cl, :2 * cl], 0.0) for x_ in m]
    aak = [jnp.where(strict, x_[:cl, 2 * cl:], 0.0) for x_ in m]
    arbk = [jnp.concatenate([jnp.where(incl, x_[cl:, :2 * cl], 0.0), jnp.where(incl, x_[cl:, 2 * cl:], 0.0)], axis=1)
            for x_ in m]
    abd = [_split_heads(x_, firstw) for x_ in aab]
    vbd = [_split_heads(x_, first) for x_ in v_p]
    rhs_u = [sa[p][:cl, :] + _mm(aak[p], vbd[p]) for p in pairs]
    tinv = [eye + x_ for x_ in abd]
    xp = [_mm(x_, x_) for x_ in abd]
    for lvl in range(1, n_levels):
        if lvl < n_levels - 1:
            both = [_mm(xp[p], jnp.concatenate([tinv[p], xp[p]], axis=1)) for p in pairs]
            tinv = [tinv[p] + both[p][:, :2 * cl] for p in pairs]
            xp = [both[p][:, 2 * cl:] for p in pairs]
        else:
            tinv = [tinv[p] + _mm(xp[p], tinv[p]) for p in pairs]
    tw = [x_[:cl, :] + x_[cl:, :] for x_ in tinv]
    u = [_mm(tw[p], _split_heads(rhs_u[p], first)) for p in pairs]
    y = [sa[p][cl:, :] + _mm(arbk[p], jnp.concatenate([_split_heads(u[p], first), vbd[p]], axis=0)) for p in pairs]
    ds = [_mm_tn(jnp.concatenate([u[p], v_p[p]], axis=0),
                 jnp.concatenate([beta[p] * e_end[:, sls[p]], kh_state[:, sls[p]] * e_end[:, sls[p]]], axis=0))
          for p in pairs]
    mean = [_mm(x_, bd) * (1.0 / head) for x_ in y]
    dlt = [y[p] - mean[p] for p in pairs]
    var = [_mm(x_ * x_, bd) * (1.0 / head) for x_ in dlt]
    bonus = [_mm(bonus_in[:, sls[p]], bd) * v_p[p] for p in pairs]
    for p in pairs:
        sl = sls[p]
        state_ref[p] = st[p] * w_end[:, sl] + jnp.where(same_head, ds[p], 0.0)
        yn = dlt[p] * lax.rsqrt(var[p] + LNX_EPS) * lng_ref[:, sl] + lnb_ref[:, sl]
        o_ref[0, :, sl] = ((yn + bonus[p]) * g[:, sl]).astype(o_ref.dtype)

    @pl.when(c_idx == pl.num_programs(1) - 1)
    def _():
        sout_ref[0] = state_ref[...]


def _rwkv(rw, shift0, s0, prm, t_real):
    b, s, p = rw.shape
    wd = prm["w0"].shape[1]
    n_pairs = wd // LANES
    cl = RW_CHUNK
    body = functools.partial(_rwkv_kernel, wd=wd, r_w=prm["r_w"], t_real=t_real, head=prm["head"])

    def const(shape):
        return pl.BlockSpec(shape, lambda bi, ci: (0,) * len(shape))

    names = ["mu", "w0", "a0", "wwa", "g2", "kk", "ka", "rk", "lng", "lnb", "tril", "bd"]
    return pl.pallas_call(
        body,
        grid=(b, s // cl),
        in_specs=[
            pl.BlockSpec((1, cl, p), lambda bi, ci: (bi, ci, 0)),
            pl.BlockSpec((1, 1, p), lambda bi, ci: (bi, 0, 0)),
            pl.BlockSpec((1, n_pairs, LANES, LANES), lambda bi, ci: (bi, 0, 0, 0)),
        ] + [const(prm[nm].shape) for nm in names],
        out_specs=(pl.BlockSpec((1, cl, wd), lambda bi, ci: (bi, ci, 0)),
                   pl.BlockSpec((1, n_pairs, LANES, LANES), lambda bi, ci: (bi, 0, 0, 0))),
        out_shape=(jax.ShapeDtypeStruct((b, s, wd), BF16),
                   jax.ShapeDtypeStruct((b, n_pairs, LANES, LANES), F32)),
        scratch_shapes=[pltpu.VMEM((n_pairs, LANES, LANES), F32), pltpu.VMEM((1, p), F32)],
        compiler_params=_params(("parallel", "arbitrary")),
        name="rwkv",
    )(rw, shift0, s0, *[prm[nm] for nm in names])


def _pack_state(wkv, head):
    b, h = wkv.shape[:2]
    pairs = wkv.reshape(b, h // 2, 2, head, head)
    z = jnp.zeros((b, h // 2, head, head), wkv.dtype)
    top = jnp.concatenate([pairs[:, :, 0], z], axis=-1)
    bot = jnp.concatenate([z, pairs[:, :, 1]], axis=-1)
    return jnp.concatenate([top, bot], axis=-2)


def _unpack_state(sbd, head):
    b, n_pairs = sbd.shape[:2]
    h0 = sbd[:, :, :head, :head]
    h1 = sbd[:, :, head:, head:]
    return jnp.stack([h0, h1], axis=2).reshape(b, 2 * n_pairs, head, head)


def _outproj_kernel(x_ref, oda_ref, orw_ref, w1_ref, w2_ref, g_ref, x1_ref, ht_ref):
    y = (jnp.dot(oda_ref[...], w1_ref[...], preferred_element_type=F32)
         + jnp.dot(orw_ref[...], w2_ref[...], preferred_element_type=F32))
    x1 = x_ref[...] + y
    x1_ref[...] = x1
    ms = jnp.mean(x1 * x1, axis=-1, keepdims=True)
    h2 = x1 * lax.rsqrt(ms + RMS_EPS) * g_ref[...]
    ht_ref[...] = h2.T.astype(BF16)


def _outproj(x, oda, orw, w1, w2, g, tm):
    n, d = x.shape
    wa, wr = oda.shape[1], orw.shape[1]
    return pl.pallas_call(
        _outproj_kernel,
        grid=(n // tm,),
        in_specs=[
            pl.BlockSpec((tm, d), lambda i: (i, 0)),
            pl.BlockSpec((tm, wa), lambda i: (i, 0)),
            pl.BlockSpec((tm, wr), lambda i: (i, 0)),
            pl.BlockSpec((wa, d), lambda i: (0, 0)),
            pl.BlockSpec((wr, d), lambda i: (0, 0)),
            pl.BlockSpec((1, d), lambda i: (0, 0)),
        ],
        out_specs=(pl.BlockSpec((tm, d), lambda i: (i, 0)), pl.BlockSpec((d, tm), lambda i: (0, i))),
        out_shape=(jax.ShapeDtypeStruct((n, d), F32), jax.ShapeDtypeStruct((d, n), BF16)),
        compiler_params=_params(("parallel",)),
        name="outproj",
    )(x, oda, orw, w1, w2, g.reshape(1, d))


def _peer_scores_kernel(ht_ref, wqt_ref, keys_ref, s_ref):
    q = jnp.dot(wqt_ref[...], ht_ref[...], preferred_element_type=F32)
    s_ref[0] = jnp.dot(keys_ref[0], q.astype(BF16), preferred_element_type=F32)


def _peer_scores(ht, wqt, keys, tn):
    d, n = ht.shape
    nhc, n_keys, qd = keys.shape
    return pl.pallas_call(
        _peer_scores_kernel,
        grid=(n // tn, nhc),
        in_specs=[
            pl.BlockSpec((d, tn), lambda t, j: (0, t)),
            pl.BlockSpec((qd, d), lambda t, j: (j, 0)),
            pl.BlockSpec((1, n_keys, qd), lambda t, j: (j, 0, 0)),
        ],
        out_specs=pl.BlockSpec((1, n_keys, tn), lambda t, j: (j, 0, t)),
        out_shape=jax.ShapeDtypeStruct((nhc, n_keys, n), F32),
        compiler_params=_params(("parallel", "arbitrary")),
        name="peer_scores",
    )(ht, wqt, keys)


def _top_desc(s, k):
    cur = jnp.max(s, axis=0, keepdims=True)
    out = [cur]
    for _ in range(k - 1):
        cur = jnp.max(jnp.where(s < cur, s, -jnp.inf), axis=0, keepdims=True)
        out.append(cur)
    return out


def _peer_route_kernel(s_ref, e1_ref, e2_ref, tau_ref):
    s1 = s_ref[0]
    s2 = s_ref[1]
    a = _top_desc(s1, PEER_TOPK)
    b = jnp.concatenate(_top_desc(s2, PEER_TOPK), axis=0)
    cand = jnp.concatenate([ai + b for ai in a], axis=0)
    best = _top_desc(cand, PEER_TOPK)
    tau = best[-1]
    z = sum(jnp.exp(c - best[0]) for c in best)
    e1_ref[0] = jnp.exp(s1 - a[0]) / z
    e2_ref[0] = jnp.exp(s2 - b[0:1, :])
    tau_ref[0] = tau


def _peer_route(scores, tn):
    nhc, n_keys, n = scores.shape
    nh = nhc // 2
    blk = pl.BlockSpec((1, n_keys, tn), lambda t, h: (h, 0, t))
    return pl.pallas_call(
        _peer_route_kernel,
        grid=(n // tn, nh),
        in_specs=[pl.BlockSpec((2, n_keys, tn), lambda t, h: (h, 0, t))],
        out_specs=(blk, blk, pl.BlockSpec((1, 1, tn), lambda t, h: (h, 0, t))),
        out_shape=(jax.ShapeDtypeStruct((nh, n_keys, n), F32), jax.ShapeDtypeStruct((nh, n_keys, n), F32),
                   jax.ShapeDtypeStruct((nh, 1, n), F32)),
        compiler_params=_params(("parallel", "arbitrary")),
        name="peer_route",
    )(scores)


def _gelu(x):
    return 0.5 * x * (1.0 + lax.erf(x * (2.0 ** -0.5)))


def _peer_dense_kernel(ht_ref, x1_ref, s_ref, e1_ref, e2_ref, tau_ref, u_ref, v_ref, o_ref,
                       acc_ref, act_ref, wt_ref, wtt_ref, *, ib, nh, n_keys, lane_chunk):
    e_idx = pl.program_id(1)
    n_blocks = pl.num_programs(1) - 1
    tn = ht_ref.shape[1]

    @pl.when(e_idx == 0)
    def _():
        acc_ref[...] = jnp.zeros_like(acc_ref)
        wtt_ref[1] = jnp.zeros(wtt_ref.shape[1:], wtt_ref.dtype)

    blk = jnp.minimum(e_idx, n_blocks - 1)
    d = v_ref.shape[1]
    dc = d // ib
    prev_slot = (e_idx + 1) % 2
    slot = e_idx % 2

    def first_matmul(i):
        rs = slice(i * n_keys, (i + 1) * n_keys)
        act_ref[rs, :] = _gelu(jnp.dot(u_ref[rs, :], ht_ref[...], preferred_element_type=F32))

    def gate_tile(i, ch):
        i1 = blk * ib + i
        rs = slice(i * n_keys, (i + 1) * n_keys)
        cs = slice(ch * lane_chunk, (ch + 1) * lane_chunk)
        gsum = None
        for h in range(nh):
            s1row = s_ref[2 * h, pl.ds(i1, 1), cs]
            e1row = e1_ref[h, pl.ds(i1, 1), cs]
            keep = (s1row + s_ref[2 * h + 1, :, cs]) >= tau_ref[h, :, cs]
            term = jnp.where(keep, e1row * e2_ref[h, :, cs], 0.0)
            gsum = term if gsum is None else gsum + term
        wt_ref[rs, cs] = gsum * act_ref[rs, cs]

    def second_matmul(i):
        cols = slice(i * dc, (i + 1) * dc)
        acc_ref[:, cols] += jnp.dot(wtt_ref[prev_slot], v_ref[:, cols], preferred_element_type=F32)

    n_ch = tn // lane_chunk
    first_matmul(0)
    for i in range(ib):
        if i + 1 < ib:
            first_matmul(i + 1)
        for ch in range(n_ch):
            gate_tile(i, ch)
            if ch == 0:
                second_matmul(i)
        rs = slice(i * n_keys, (i + 1) * n_keys)
        wtt_ref[slot, :, rs] = wt_ref[rs, :].T.astype(BF16)

    @pl.when(e_idx == n_blocks)
    def _():
        o_ref[...] = x1_ref[...] + acc_ref[...]


def _peer_dense(ht, x1, scores, e1, e2, tau, u_bf, v_bf, tn, ib):
    d, n = ht.shape
    nh, n_keys, _ = e1.shape
    n_exp = u_bf.shape[0]
    ec = ib * n_keys
    n_blocks = n_exp // ec
    once = pl.Buffered(1)
    body = functools.partial(_peer_dense_kernel, ib=ib, nh=nh, n_keys=n_keys, lane_chunk=min(tn, 2 * LANES))
    return pl.pallas_call(
        body,
        grid=(n // tn, n_blocks + 1),
        in_specs=[
            pl.BlockSpec((d, tn), lambda t, e: (0, t)),
            pl.BlockSpec((tn, d), lambda t, e: (t, 0), pipeline_mode=once),
            pl.BlockSpec((2 * nh, n_keys, tn), lambda t, e: (0, 0, t), pipeline_mode=once),
            pl.BlockSpec((nh, n_keys, tn), lambda t, e: (0, 0, t), pipeline_mode=once),
            pl.BlockSpec((nh, n_keys, tn), lambda t, e: (0, 0, t), pipeline_mode=once),
            pl.BlockSpec((nh, 1, tn), lambda t, e: (0, 0, t)),
            pl.BlockSpec((ec, d), lambda t, e: (jnp.minimum(e, n_blocks - 1), 0)),
            pl.BlockSpec((ec, d), lambda t, e: (jnp.maximum(e - 1, 0), 0)),
        ],
        out_specs=pl.BlockSpec((tn, d), lambda t, e: (t, 0)),
        out_shape=jax.ShapeDtypeStruct((n, d), F32),
        scratch_shapes=[pltpu.VMEM((tn, d), F32), pltpu.VMEM((ec, tn), F32), pltpu.VMEM((ec, tn), F32),
                        pltpu.VMEM((2, tn, ec), BF16)],
        compiler_params=_params(("parallel", "arbitrary")),
        name="peer_dense",
    )(ht, x1, scores, e1, e2, tau, u_bf, v_bf)


def _tile(n, pref):
    return pref if n % pref == 0 else n


def _layer(x, attend, shift0, s0, t_real, lam_init, lw):
    b, t, d = x.shape
    n = b * t
    xf = x.reshape(n, d)
    w_da = lw["w_q"].shape[1]
    tm = _tile(n, 512)
    h1 = _rms(xf, lw["norm1_g"], tm)
    tmp = _tile(n, 1024)
    qb = _proj("q", h1, lw["w_q"], tmp, _tile(w_da, 512), gain=lw["q_gain"], group=lw["dk"], scale=lw["dk"] ** -0.5)
    k, kb = _proj("k", h1, lw["w_k"], tmp, _tile(w_da, 512), gain=lw["k_gain"], group=lw["dk"])
    v, vb = _proj("dual", h1, lw["w_v"], tmp, _tile(w_da, 512))
    p_pad = lw["w_rw"].shape[1]
    rw = _proj("plain", h1, lw["w_rw"], tmp, p_pad // 3)

    o_da = attend(qb.reshape(b, t, w_da), kb.reshape(b, t, w_da), vb.reshape(b, t, w_da))

    rw3 = rw.reshape(b, t, p_pad)
    if t % RW_CHUNK:
        rw3 = jnp.pad(rw3, ((0, 0), (0, RW_CHUNK - t % RW_CHUNK), (0, 0)))
    o_rw, s_out = _rwkv(rw3, shift0, s0, lw["rwkv"], t_real)
    o_rw = o_rw[:, :t].reshape(n, -1)

    x1, h2t = _outproj(xf, o_da.reshape(n, w_da), o_rw, lw["w_out_da"], lw["w_out_rw"], lw["norm2_g"],
                       _tile(n, 256))
    tn = _tile(n, 512)
    scores = _peer_scores(h2t, lw["wq_t"], lw["peer_keys"], tn)
    e1, e2, tau = _peer_route(scores, tn)
    out = _peer_dense(h2t, x1, scores, e1, e2, tau, lw["peer_u"], lw["peer_v"], tn, 4)
    return out.reshape(b, t, d), k, v, s_out, rw3[:, t - 1, :lw["rw_proj"]]


def kernel(x_prompt, x_sample, cache_k, cache_v, state_wkv, state_shift, page_table, norm1_g, w_in, da_qk_g,
           da_lambda, da_subln_g, rw_mu, rw_w0, rw_w2, rw_a0, rw_a2, rw_g2, rw_kk, rw_ka, rw_rk, rw_lnx,
           w_out, norm2_g, peer_wq, peer_keys, peer_u, peer_v):
    depth = w_in.shape[0]
    d = x_prompt.shape[-1]
    heads, dk = cache_k.shape[3], cache_k.shape[5]
    w_da = heads * 2 * dk
    rw_heads, head = state_wkv.shape[2], state_wkv.shape[3]
    wd = rw_heads * head
    r_w, r_a, r_g = rw_w2.shape[1], rw_a2.shape[1], rw_g2.shape[1]
    rw_proj = 3 * wd + r_w + r_a + r_g
    g_pad = 2 * LANES
    p_pad = 3 * wd + LANES + g_pad
    assert d == w_da + wd and head * 2 == LANES and wd % LANES == 0 and 2 * dk == LANES
    assert r_w + r_a == LANES and r_g <= g_pad and p_pad % 3 == 0 and (p_pad // 3) % LANES == 0
    assert cache_k.shape[2] == PAGE_SIZE
    nh, _, n_keys, qd = peer_keys.shape[1:]
    bsz = x_prompt.shape[0]
    dec_b, dec_t = x_sample.shape[:2]
    assert dec_t * heads <= PAGE_SIZE

    xp, xs = x_prompt, x_sample
    outs = [[] for _ in range(8)]
    for l in range(depth):
        lam_init = 0.8 - 0.6 * math.exp(-0.3 * l)
        wl = w_in[l].astype(BF16)
        w_rw = jnp.pad(wl[:, 3 * w_da:], ((0, 0), (0, p_pad - rw_proj)))
        zero_wa = jnp.zeros((r_w, wd), F32)
        wwa = jnp.concatenate([jnp.concatenate([rw_w2[l], zero_wa], axis=1),
                               jnp.concatenate([zero_wa, rw_a2[l]], axis=1)], axis=0).astype(BF16)
        row = lambda z: z.reshape(1, -1)
        rwkv_prm = dict(
            mu=jnp.pad(row(rw_mu[l]), ((0, 0), (0, p_pad - rw_proj))),
            w0=row(rw_w0[l]), a0=row(rw_a0[l]), wwa=wwa,
            g2=jnp.pad(rw_g2[l], ((0, g_pad - r_g), (0, 0))).astype(BF16),
            kk=row(rw_kk[l]), ka=row(rw_ka[l]), rk=row(rw_rk[l]),
            lng=row(rw_lnx[l, 0]), lnb=row(rw_lnx[l, 1]),
            tril=jnp.tril(jnp.ones((RW_CHUNK, RW_CHUNK), BF16)),
            bd=_block_diag_ones(LANES, head), r_w=r_w, head=head)
        lw = dict(
            norm1_g=norm1_g[l], w_q=wl[:, :w_da], w_k=wl[:, w_da:2 * w_da], w_v=wl[:, 2 * w_da:3 * w_da],
            w_rw=w_rw, q_gain=jnp.tile(da_qk_g[l, 0], 2 * heads), k_gain=jnp.tile(da_qk_g[l, 1], 2 * heads),
            dk=dk, rwkv=rwkv_prm, rw_proj=rw_proj,
            w_out_da=w_out[l, :w_da].astype(BF16), w_out_rw=w_out[l, w_da:].astype(BF16),
            norm2_g=norm2_g[l], wq_t=peer_wq[l].T.astype(BF16),
            peer_keys=peer_keys[l].reshape(nh * 2, n_keys, qd).astype(BF16),
            peer_u=peer_u[l].astype(BF16), peer_v=peer_v[l].astype(BF16))

        attend_p = functools.partial(_attn_prompt, da_lambda=da_lambda[l], subln_g=da_subln_g[l], heads=heads,
                                     lam_init=lam_init, tq=_tile(x_prompt.shape[1], 512))
        n_pairs = wd // LANES
        xp, kp, vp, sp, shp = _layer(
            xp, attend_p, jnp.zeros((bsz, 1, p_pad), F32), jnp.zeros((bsz, n_pairs, LANES, LANES), F32),
            RW_CHUNK, lam_init, lw)

        attend_s = functools.partial(_attn_sample, cache_k=cache_k, cache_v=cache_v, layer=l,
                                     page_table=page_table, da_lambda=da_lambda[l], subln_g=da_subln_g[l],
                                     heads=heads, lam_init=lam_init, pp=4)
        shift_s = jnp.pad(state_shift[l], ((0, 0), (0, p_pad - rw_proj))).reshape(dec_b, 1, p_pad)
        xs, ksm, vsm, ssm, shs = _layer(xs, attend_s, shift_s, _pack_state(state_wkv[l], head), dec_t, lam_init, lw)

        outs[0].append(kp.reshape(bsz, -1, heads, 2, dk))
        outs[1].append(vp.reshape(bsz, -1, heads, 2 * dk))
        outs[2].append(_unpack_state(sp, head))
        outs[3].append(shp)
        outs[4].append(ksm.reshape(dec_b, dec_t, heads, 2, dk))
        outs[5].append(vsm.reshape(dec_b, dec_t, heads, 2 * dk))
        outs[6].append(_unpack_state(ssm, head))
        outs[7].append(shs)
    return (xp, xs) + tuple(jnp.stack(o) for o in outs)
```

```python
import functools
import math

import jax
import jax.numpy as jnp
from jax import lax
from jax.experimental import pallas as pl
from jax.experimental.pallas import tpu as pltpu

F32 = jnp.float32
BF16 = jnp.bfloat16

LANES = 128
RMS_EPS = 1e-6
LNX_EPS = 64e-5
PEER_TOPK = 16
PAGE_SIZE = 128
RW_CHUNK = 64
VMEM_LIMIT = 48 * 1024 * 1024

_NT = (((1,), (1,)), ((), ()))
_TN = (((0,), (0,)), ((), ()))


def _params(sem):
    return pltpu.CompilerParams(dimension_semantics=sem, vmem_limit_bytes=VMEM_LIMIT)


def _mm(a, b):
    return jnp.dot(a.astype(BF16), b.astype(BF16), preferred_element_type=F32)


def _mm_nt(a, b):
    return lax.dot_general(a.astype(BF16), b.astype(BF16), _NT, preferred_element_type=F32)


def _mm_tn(a, b):
    return lax.dot_general(a.astype(BF16), b.astype(BF16), _TN, preferred_element_type=F32)


def _mm_split(a_exact_bf16, b):
    hi = b.astype(BF16)
    lo = (b - hi.astype(F32)).astype(BF16)
    return (jnp.dot(a_exact_bf16, hi, preferred_element_type=F32)
            + jnp.dot(a_exact_bf16, lo, preferred_element_type=F32))


def _mm_split_lhs(a, b_exact_bf16):
    hi = a.astype(BF16)
    lo = (a - hi.astype(F32)).astype(BF16)
    return (jnp.dot(hi, b_exact_bf16, preferred_element_type=F32)
            + jnp.dot(lo, b_exact_bf16, preferred_element_type=F32))


def _block_diag_ones(n, group):
    i = jnp.arange(n)
    return (i[:, None] // group == i[None, :] // group).astype(BF16)


def _rms_kernel(x_ref, g_ref, o_ref):
    x = x_ref[...]
    ms = jnp.mean(x * x, axis=-1, keepdims=True)
    o_ref[...] = (x * lax.rsqrt(ms + RMS_EPS) * g_ref[...]).astype(o_ref.dtype)


def _rms(x, g, tm):
    n, d = x.shape
    return pl.pallas_call(
        _rms_kernel,
        grid=(n // tm,),
        in_specs=[pl.BlockSpec((tm, d), lambda i: (i, 0)), pl.BlockSpec((1, d), lambda i: (0, 0))],
        out_specs=pl.BlockSpec((tm, d), lambda i: (i, 0)),
        out_shape=jax.ShapeDtypeStruct((n, d), BF16),
        compiler_params=_params(("parallel",)),
        name="rms",
    )(x, g.reshape(1, d))


def _proj_plain_kernel(h_ref, w_ref, o_ref):
    o_ref[...] = jnp.dot(h_ref[...], w_ref[...], preferred_element_type=F32)


def _proj_dual_kernel(h_ref, w_ref, o_ref, ob_ref):
    y = jnp.dot(h_ref[...], w_ref[...], preferred_element_type=F32)
    o_ref[...] = y
    ob_ref[...] = y.astype(BF16)


def _qk_normed(h_ref, w_ref, bd_ref, g_ref, group):
    y = jnp.dot(h_ref[...], w_ref[...], preferred_element_type=F32)
    ms = _mm_split_lhs(y * y, bd_ref[...]) * (1.0 / group)
    return y * lax.rsqrt(ms + RMS_EPS) * g_ref[...]


def _proj_q_kernel(h_ref, w_ref, bd_ref, g_ref, ob_ref, *, group, scale):
    ob_ref[...] = (_qk_normed(h_ref, w_ref, bd_ref, g_ref, group) * scale).astype(BF16)


def _proj_k_kernel(h_ref, w_ref, bd_ref, g_ref, o_ref, ob_ref, *, group):
    y = _qk_normed(h_ref, w_ref, bd_ref, g_ref, group)
    o_ref[...] = y
    ob_ref[...] = y.astype(BF16)


def _proj(kind, h, w, tm, tn, gain=None, group=None, scale=None):
    n, d = h.shape
    ncol = w.shape[1]
    grid = (n // tm, ncol // tn)
    h_spec = pl.BlockSpec((tm, d), lambda i, j: (i, 0))
    w_spec = pl.BlockSpec((d, tn), lambda i, j: (0, j))
    o_spec = pl.BlockSpec((tm, tn), lambda i, j: (i, j))
    f32_out = jax.ShapeDtypeStruct((n, ncol), F32)
    bf_out = jax.ShapeDtypeStruct((n, ncol), BF16)
    if kind in ("q", "k"):
        bd = _block_diag_ones(tn, group)
        extra = [bd, gain.reshape(1, ncol)]
        extra_specs = [pl.BlockSpec((tn, tn), lambda i, j: (0, 0)), pl.BlockSpec((1, tn), lambda i, j: (0, j))]
    else:
        extra, extra_specs = [], []
    if kind == "q":
        body = functools.partial(_proj_q_kernel, group=group, scale=scale)
        out_shape, out_specs = bf_out, o_spec
    elif kind == "k":
        body = functools.partial(_proj_k_kernel, group=group)
        out_shape, out_specs = (f32_out, bf_out), (o_spec, o_spec)
    elif kind == "dual":
        body = _proj_dual_kernel
        out_shape, out_specs = (f32_out, bf_out), (o_spec, o_spec)
    else:
        body = _proj_plain_kernel
        out_shape, out_specs = f32_out, o_spec
    return pl.pallas_call(
        body,
        grid=grid,
        in_specs=[h_spec, w_spec] + extra_specs,
        out_specs=out_specs,
        out_shape=out_shape,
        compiler_params=_params(("parallel", "arbitrary")),
        name="proj_" + kind,
    )(h, w, *extra)


def _lambda(lam_ref, lam_init):
    ll = lam_ref[...]
    s01 = jnp.sum(ll[0:1, :] * ll[1:2, :], axis=-1, keepdims=True)
    s23 = jnp.sum(ll[2:3, :] * ll[3:4, :], axis=-1, keepdims=True)
    return jnp.exp(s01) - jnp.exp(s23) + lam_init


def _subln(o, g, lam_init):
    ms = jnp.mean(o * o, axis=-1, keepdims=True)
    return (o * lax.rsqrt(ms + RMS_EPS) * g) * (1.0 - lam_init)


def _attn_prompt_kernel(q_ref, k_ref, v_ref, lam_ref, g_ref, o_ref, acc0_ref, acc1_ref, *, tq, dk, lam_init):
    qi = pl.program_id(2)
    q = q_ref[0]
    lane = lax.broadcasted_iota(jnp.int32, q.shape, 1)
    qf = q.astype(F32)
    qc = (jnp.where(lane < dk, qf, 0.0).astype(BF16), jnp.where(lane >= dk, qf, 0.0).astype(BF16))
    accs = (acc0_ref, acc1_ref)
    acc0_ref[...] = jnp.zeros_like(acc0_ref)
    acc1_ref[...] = jnp.zeros_like(acc1_ref)
    causal = (lax.broadcasted_iota(jnp.int32, (tq, tq), 1) <= lax.broadcasted_iota(jnp.int32, (tq, tq), 0))

    def step(j, carry, diagonal):
        kb = k_ref[0, pl.ds(pl.multiple_of(j * tq, tq), tq), :]
        vb = v_ref[0, pl.ds(pl.multiple_of(j * tq, tq), tq), :]
        new = []
        for c in range(2):
            m, l = carry[2 * c], carry[2 * c + 1]
            s = lax.dot_general(qc[c], kb, _NT, preferred_element_type=F32)
            if diagonal:
                s = jnp.where(causal, s, -jnp.inf)
            m_new = jnp.maximum(m, jnp.max(s, axis=-1, keepdims=True))
            corr = jnp.exp(m - m_new)
            p = jnp.exp(s - m_new)
            l = l * corr + jnp.sum(p, axis=-1, keepdims=True)
            accs[c][...] = accs[c][...] * corr + jnp.dot(p.astype(BF16), vb, preferred_element_type=F32)
            new += [m_new, l]
        return tuple(new)

    minit = jnp.full((tq, 1), -jnp.inf, F32)
    linit = jnp.zeros((tq, 1), F32)
    carry = lax.fori_loop(0, qi, functools.partial(step, diagonal=False), (minit, linit, minit, linit))
    _, l0, _, l1 = step(qi, carry, True)
    lam = _lambda(lam_ref, lam_init)
    o = acc0_ref[...] / l0 - lam * (acc1_ref[...] / l1)
    o_ref[0] = _subln(o, g_ref[...], lam_init).astype(o_ref.dtype)


def _attn_prompt(qb, kb, vb, da_lambda, subln_g, heads, lam_init, tq):
    b, s, w = qb.shape
    dv = w // heads
    body = functools.partial(_attn_prompt_kernel, tq=tq, dk=dv // 2, lam_init=lam_init)
    return pl.pallas_call(
        body,
        grid=(b, heads, s // tq),
        in_specs=[
            pl.BlockSpec((1, tq, dv), lambda bi, h, qi: (bi, qi, h)),
            pl.BlockSpec((1, s, dv), lambda bi, h, qi: (bi, 0, h)),
            pl.BlockSpec((1, s, dv), lambda bi, h, qi: (bi, 0, h)),
            pl.BlockSpec(da_lambda.shape, lambda bi, h, qi: (0, 0)),
            pl.BlockSpec((1, dv), lambda bi, h, qi: (0, 0)),
        ],
        out_specs=pl.BlockSpec((1, tq, dv), lambda bi, h, qi: (bi, qi, h)),
        out_shape=jax.ShapeDtypeStruct((b, s, w), BF16),
        scratch_shapes=[pltpu.VMEM((tq, dv), F32), pltpu.VMEM((tq, dv), F32)],
        compiler_params=_params(("parallel", "parallel", "arbitrary")),
        name="attn_prompt",
    )(qb, kb, vb, da_lambda, subln_g.reshape(1, dv))


def _attn_sample_kernel(pt_ref, qrep_ref, kst_ref, vs_ref, lam_ref, g_ref, *rest, pp, heads, t, dk, lam_init):
    kc_refs = rest[:pp]
    vc_refs = rest[pp:2 * pp]
    o_ref = rest[2 * pp]
    qbd_ref, m_ref, l_ref, acc_ref = rest[2 * pp + 1:]
    del pt_ref
    p_idx = pl.program_id(1)
    rows = 2 * t * heads
    dv = 2 * dk
    w = heads * dv

    @pl.when(p_idx == 0)
    def _():
        r = lax.broadcasted_iota(jnp.int32, (rows, w), 0)
        col = lax.broadcasted_iota(jnp.int32, (rows, w), 1)
        keep = (col // dk) == ((r % heads) * 2 + r // (t * heads))
        qbd_ref[...] = jnp.where(keep, qrep_ref[0].astype(F32), 0.0).astype(BF16)
        m_ref[...] = jnp.full(m_ref.shape, -jnp.inf, F32)
        l_ref[...] = jnp.zeros(l_ref.shape, F32)
        acc_ref[...] = jnp.zeros(acc_ref.shape, F32)

    qbd = qbd_ref[...]

    def online(s, vbs):
        m = m_ref[...]
        m_new = jnp.maximum(m, jnp.max(s, axis=-1, keepdims=True))
        corr = jnp.exp(m - m_new)
        p = jnp.exp(s - m_new)
        l_ref[...] = l_ref[...] * corr + jnp.sum(p, axis=-1, keepdims=True)
        pv = None
        for i, vb in enumerate(vbs):
            n = vb.shape[0]
            d = jnp.dot(p[:, i * n:(i + 1) * n].astype(BF16), vb, preferred_element_type=F32)
            pv = d if pv is None else pv + d
        acc_ref[...] = acc_ref[...] * corr + pv
        m_ref[...] = m_new

    kt = jnp.concatenate([kc[0] for kc in kc_refs], axis=1).astype(BF16)
    vwide = [jnp.concatenate([vc[0, pl.ds(h, PAGE_SIZE, stride=heads), :] for h in range(heads)], axis=1)
             .astype(BF16) for vc in vc_refs]
    online(jnp.dot(qbd, kt, preferred_element_type=F32), vwide)

    @pl.when(p_idx == pl.num_programs(1) - 1)
    def _():
        s = jnp.dot(qbd, kst_ref[0], preferred_element_type=F32)
        r = lax.broadcasted_iota(jnp.int32, s.shape, 0)
        col = lax.broadcasted_iota(jnp.int32, s.shape, 1)
        online(jnp.where(col <= (r // heads) % t, s, -jnp.inf), [vs_ref[0]])
        out = acc_ref[...] / l_ref[...]
        hrow = lax.broadcasted_iota(jnp.int32, (heads, w), 0)
        hcol = lax.broadcasted_iota(jnp.int32, (heads, w), 1) // dv
        pick = hrow == hcol
        lam = _lambda(lam_ref, lam_init)
        toks = []
        for tok_i in range(t):
            maps = []
            for c in range(2):
                base = (c * t + tok_i) * heads
                maps.append(jnp.sum(jnp.where(pick, out[base:base + heads, :], 0.0), axis=0, keepdims=True))
            toks.append(maps[0] - lam * maps[1])
        o = jnp.concatenate(toks, axis=0)
        g = g_ref[...]
        outs = [_subln(o[:, h * dv:(h + 1) * dv], g, lam_init) for h in range(heads)]
        o_ref[0] = jnp.concatenate(outs, axis=1).astype(o_ref.dtype)


def _attn_sample(qb, kb, vb, cache_k, cache_v, layer, page_table, da_lambda, subln_g, heads, lam_init, pp):
    bd, t, w = qb.shape
    dv = w // heads
    n_pool = cache_k.shape[1]
    n_pages = page_table.shape[1]
    rows = 2 * t * heads
    qrep = jnp.tile(jnp.repeat(qb, heads, axis=1), (1, 2, 1))
    kst_pad = jnp.pad(jnp.swapaxes(kb, 1, 2), ((0, 0), (0, 0), (0, PAGE_SIZE - t)))
    vs_pad = jnp.pad(vb, ((0, 0), (0, PAGE_SIZE - t), (0, 0)))
    ckt = jnp.transpose(cache_k, (0, 1, 3, 4, 5, 2)).reshape(-1, w, PAGE_SIZE)
    cv = cache_v.reshape(-1, PAGE_SIZE * heads, dv)
    base = layer * n_pool

    def k_spec(i):
        return pl.BlockSpec((1, w, PAGE_SIZE), lambda b, p, pt: (base + pt[b, p * pp + i], 0, 0))

    def v_spec(i):
        return pl.BlockSpec((1, PAGE_SIZE * heads, dv), lambda b, p, pt: (base + pt[b, p * pp + i], 0, 0))

    body = functools.partial(_attn_sample_kernel, pp=pp, heads=heads, t=t, dk=dv // 2, lam_init=lam_init)
    grid_spec = pltpu.PrefetchScalarGridSpec(
        num_scalar_prefetch=1,
        grid=(bd, n_pages // pp),
        in_specs=[
            pl.BlockSpec((1, rows, w), lambda b, p, pt: (b, 0, 0)),
            pl.BlockSpec((1, w, PAGE_SIZE), lambda b, p, pt: (b, 0, 0)),
            pl.BlockSpec((1, PAGE_SIZE, w), lambda b, p, pt: (b, 0, 0)),
            pl.BlockSpec(da_lambda.shape, lambda b, p, pt: (0, 0)),
            pl.BlockSpec((1, dv), lambda b, p, pt: (0, 0)),
        ] + [k_spec(i) for i in range(pp)] + [v_spec(i) for i in range(pp)],
        out_specs=pl.BlockSpec((1, t, w), lambda b, p, pt: (b, 0, 0)),
        scratch_shapes=[pltpu.VMEM((rows, w), BF16), pltpu.VMEM((rows, 1), F32),
                        pltpu.VMEM((rows, 1), F32), pltpu.VMEM((rows, w), F32)],
    )
    return pl.pallas_call(
        body,
        grid_spec=grid_spec,
        out_shape=jax.ShapeDtypeStruct((bd, t, w), BF16),
        compiler_params=_params(("parallel", "arbitrary")),
        name="attn_sample",
    )(page_table, qrep, kst_pad, vs_pad, da_lambda, subln_g.reshape(1, dv), *([ckt] * pp), *([cv] * pp))


def _softplus(z):
    return jnp.maximum(z, 0.0) + jnp.log1p(jnp.exp(-jnp.abs(z)))


def _sigmoid(z):
    return 1.0 / (1.0 + jnp.exp(-z))


def _split_heads(x, first):
    zero = jnp.zeros_like(x)
    return jnp.concatenate([jnp.where(first, x, zero), jnp.where(first, zero, x)], axis=0)


def _rwkv_kernel(rw_ref, shift_ref, s0_ref, mu_ref, w0_ref, a0_ref, wwa_ref, g2_ref, kk_ref, ka_ref,
                 rk_ref, lng_ref, lnb_ref, tril_ref, bd_ref, o_ref, sout_ref, state_ref, prev_ref,
                 *, wd, r_w, t_real, head):
    c_idx = pl.program_id(1)
    cl = RW_CHUNK
    pairs = range(wd // LANES)

    @pl.when(c_idx == 0)
    def _():
        state_ref[...] = s0_ref[0]
        prev_ref[...] = shift_ref[0]

    x = rw_ref[0]
    rowp = lax.broadcasted_iota(jnp.int32, x.shape, 0)
    prevx = jnp.where(rowp == 0, prev_ref[...], pltpu.roll(x, 1, axis=0))
    prev_ref[...] = x[cl - 1:cl, :]
    xm = x + (prevx - x) * mu_ref[...]
    r = xm[:, 0:wd]
    kr = xm[:, wd:2 * wd]
    vr = xm[:, 2 * wd:3 * wd]
    slab = xm[:, 3 * wd:3 * wd + LANES]
    lane_s = lax.broadcasted_iota(jnp.int32, slab.shape, 1)
    wa = _mm(jnp.where(lane_s < r_w, jnp.tanh(slab), slab), wwa_ref[...])
    gd = xm[:, 3 * wd + LANES:]
    g = _mm(_sigmoid(gd), g2_ref[...])
    w = -_softplus(-(w0_ref[...] + wa[:, :wd])) - 0.5
    lw = -jnp.exp(w)
    a = _sigmoid(a0_ref[...] + wa[:, wd:])
    kk = kr * kk_ref[...]
    kh = kr * (1.0 + (a - 1.0) * ka_ref[...])
    beta_scale = a
    if t_real < cl:
        valid = lax.broadcasted_iota(jnp.int32, (cl, wd), 0) < t_real
        lw = jnp.where(valid, lw, 0.0)
        beta_scale = jnp.where(valid, a, 0.0)
        kh_state = jnp.where(valid, kh, 0.0)
    else:
        kh_state = kh
    cum = _mm_split(tril_ref[...], lw)
    cum_last = cum[cl - 1:cl, :]
    e_neg = jnp.exp(-cum)
    e_end = jnp.exp(cum_last - cum)
    w_end = jnp.exp(cum_last)
    rt_all = r * jnp.exp(cum)
    ex_all = jnp.exp(cum - lw)
    bonus_in = r * kh * rk_ref[...]

    lane = lax.broadcasted_iota(jnp.int32, (cl, LANES), 1)
    first = lane < head
    tt = lax.broadcasted_iota(jnp.int32, (cl, 2 * cl), 0)
    ss = lax.broadcasted_iota(jnp.int32, (cl, 2 * cl), 1) % cl
    strict = ss < tt
    incl = ss <= tt
    firstw = lax.broadcasted_iota(jnp.int32, (cl, 2 * cl), 1) < cl
    r2 = lax.broadcasted_iota(jnp.int32, (2 * cl, 2 * cl), 0)
    c2 = lax.broadcasted_iota(jnp.int32, (2 * cl, 2 * cl), 1)
    eye = (r2 == c2).astype(F32)
    rs = lax.broadcasted_iota(jnp.int32, (LANES, LANES), 0) // head
    cs = lax.broadcasted_iota(jnp.int32, (LANES, LANES), 1) // head
    same_head = rs == cs
    bd = bd_ref[...]
    n_levels = int(math.log2(cl))
    sls = [slice(p * LANES, (p + 1) * LANES) for p in pairs]

    kk_p = [kk[:, sl] for sl in sls]
    nrm = [jnp.sqrt(_mm(k * k, bd)) for k in kk_p]
    kkn = [k / jnp.maximum(n, 1e-12) for k, n in zip(kk_p, nrm)]
    beta = [kkn[p] * beta_scale[:, sls[p]] for p in pairs]
    v_p = [vr[:, sl] for sl in sls]
    st = [state_ref[p] for p in pairs]
    lhs = [jnp.concatenate([-ex_all[:, sls[p]] * kkn[p], rt_all[:, sls[p]]], axis=0) for p in pairs]
    rhs = [jnp.concatenate([_split_heads(beta[p] * e_neg[:, sls[p]], first),
                            _split_heads(kh_state[:, sls[p]] * e_neg[:, sls[p]], first)], axis=0) for p in pairs]
    m = [_mm_nt(lhs[p], rhs[p]) for p in pairs]
    sa = [_mm_nt(lhs[p], st[p]) for p in pairs]
    aab = [jnp.where(strict, x_[:cl, :2 * cl], 0.0) for x_ in m]
    aak = [jnp.where(strict, x_[:cl, 2 * cl:], 0.0) for x_ in m]
    arbk = [jnp.concatenate([jnp.where(incl, x_[cl:, :2 * cl], 0.0), jnp.where(incl, x_[cl:, 2 * cl:], 0.0)], axis=1)
            for x_ in m]
    abd = [_split_heads(x_, firstw) for x_ in aab]
    vbd = [_split_heads(x_, first) for x_ in v_p]
    rhs_u = [sa[p][:cl, :] + _mm(aak[p], vbd[p]) for p in pairs]
    tinv = [eye + x_ for x_ in abd]
    xp = [_mm(x_, x_) for x_ in abd]
    for lvl in range(1, n_levels):
        if lvl < n_levels - 1:
            both = [_mm(xp[p], jnp.concatenate([tinv[p], xp[p]], axis=1)) for p in pairs]
            tinv = [tinv[p] + both[p][:, :2 * cl] for p in pairs]
            xp = [both[p][:, 2 * cl:] for p in pairs]
        else:
            tinv = [tinv[p] + _mm(xp[p], tinv[p]) for p in pairs]
    tw = [x_[:cl, :] + x_[cl:, :] for x_ in tinv]
    u = [_mm(tw[p], _split_heads(rhs_u[p], first)) for p in pairs]
    y = [sa[p][cl:, :] + _mm(arbk[p], jnp.concatenate([_split_heads(u[p], first), vbd[p]], axis=0)) for p in pairs]
    ds = [_mm_tn(jnp.concatenate([u[p], v_p[p]], axis=0),
                 jnp.concatenate([beta[p] * e_end[:, sls[p]], kh_state[:, sls[p]] * e_end[:, sls[p]]], axis=0))
          for p in pairs]
    mean = [_mm(x_, bd) * (1.0 / head) for x_ in y]
    dlt = [y[p] - mean[p] for p in pairs]
    var = [_mm(x_ * x_, bd) * (1.0 / head) for x_ in dlt]
    bonus = [_mm(bonus_in[:, sls[p]], bd) * v_p[p] for p in pairs]
    for p in pairs:
        sl = sls[p]
        state_ref[p] = st[p] * w_end[:, sl] + jnp.where(same_head, ds[p], 0.0)
        yn = dlt[p] * lax.rsqrt(var[p] + LNX_EPS) * lng_ref[:, sl] + lnb_ref[:, sl]
        o_ref[0, :, sl] = ((yn + bonus[p]) * g[:, sl]).astype(o_ref.dtype)

    @pl.when(c_idx == pl.num_programs(1) - 1)
    def _():
        sout_ref[0] = state_ref[...]


def _rwkv(rw, shift0, s0, prm, t_real):
    b, s, p = rw.shape
    wd = prm["w0"].shape[1]
    n_pairs = wd // LANES
    cl = RW_CHUNK
    body = functools.partial(_rwkv_kernel, wd=wd, r_w=prm["r_w"], t_real=t_real, head=prm["head"])

    def const(shape):
        return pl.BlockSpec(shape, lambda bi, ci: (0,) * len(shape))

    names = ["mu", "w0", "a0", "wwa", "g2", "kk", "ka", "rk", "lng", "lnb", "tril", "bd"]
    return pl.pallas_call(
        body,
        grid=(b, s // cl),
        in_specs=[
            pl.BlockSpec((1, cl, p), lambda bi, ci: (bi, ci, 0)),
            pl.BlockSpec((1, 1, p), lambda bi, ci: (bi, 0, 0)),
            pl.BlockSpec((1, n_pairs, LANES, LANES), lambda bi, ci: (bi, 0, 0, 0)),
        ] + [const(prm[nm].shape) for nm in names],
        out_specs=(pl.BlockSpec((1, cl, wd), lambda bi, ci: (bi, ci, 0)),
                   pl.BlockSpec((1, n_pairs, LANES, LANES), lambda bi, ci: (bi, 0, 0, 0))),
        out_shape=(jax.ShapeDtypeStruct((b, s, wd), BF16),
                   jax.ShapeDtypeStruct((b, n_pairs, LANES, LANES), F32)),
        scratch_shapes=[pltpu.VMEM((n_pairs, LANES, LANES), F32), pltpu.VMEM((1, p), F32)],
        compiler_params=_params(("parallel", "arbitrary")),
        name="rwkv",
    )(rw, shift0, s0, *[prm[nm] for nm in names])


def _pack_state(wkv, head):
    b, h = wkv.shape[:2]
    pairs = wkv.reshape(b, h // 2, 2, head, head)
    z = jnp.zeros((b, h // 2, head, head), wkv.dtype)
    top = jnp.concatenate([pairs[:, :, 0], z], axis=-1)
    bot = jnp.concatenate([z, pairs[:, :, 1]], axis=-1)
    return jnp.concatenate([top, bot], axis=-2)


def _unpack_state(sbd, head):
    b, n_pairs = sbd.shape[:2]
    h0 = sbd[:, :, :head, :head]
    h1 = sbd[:, :, head:, head:]
    return jnp.stack([h0, h1], axis=2).reshape(b, 2 * n_pairs, head, head)


def _outproj_kernel(x_ref, oda_ref, orw_ref, w1_ref, w2_ref, g_ref, x1_ref, ht_ref):
    y = (jnp.dot(oda_ref[...], w1_ref[...], preferred_element_type=F32)
         + jnp.dot(orw_ref[...], w2_ref[...], preferred_element_type=F32))
    x1 = x_ref[...] + y
    x1_ref[...] = x1
    ms = jnp.mean(x1 * x1, axis=-1, keepdims=True)
    h2 = x1 * lax.rsqrt(ms + RMS_EPS) * g_ref[...]
    ht_ref[...] = h2.T.astype(BF16)


def _outproj(x, oda, orw, w1, w2, g, tm):
    n, d = x.shape
    wa, wr = oda.shape[1], orw.shape[1]
    return pl.pallas_call(
        _outproj_kernel,
        grid=(n // tm,),
        in_specs=[
            pl.BlockSpec((tm, d), lambda i: (i, 0)),
            pl.BlockSpec((tm, wa), lambda i: (i, 0)),
            pl.BlockSpec((tm, wr), lambda i: (i, 0)),
            pl.BlockSpec((wa, d), lambda i: (0, 0)),
            pl.BlockSpec((wr, d), lambda i: (0, 0)),
            pl.BlockSpec((1, d), lambda i: (0, 0)),
        ],
        out_specs=(pl.BlockSpec((tm, d), lambda i: (i, 0)), pl.BlockSpec((d, tm), lambda i: (0, i))),
        out_shape=(jax.ShapeDtypeStruct((n, d), F32), jax.ShapeDtypeStruct((d, n), BF16)),
        compiler_params=_params(("parallel",)),
        name="outproj",
    )(x, oda, orw, w1, w2, g.reshape(1, d))


def _peer_scores_kernel(ht_ref, wqt_ref, keys_ref, s_ref, q_ref):
    q_ref[...] = jnp.dot(wqt_ref[...], ht_ref[...], preferred_element_type=F32).astype(BF16)
    nhc, _, qd = keys_ref.shape
    for j in range(nhc):
        s_ref[j] = jnp.dot(keys_ref[j], q_ref[j * qd:(j + 1) * qd, :], preferred_element_type=F32)


def _peer_scores(ht, wqt, keys, tn):
    d, n = ht.shape
    nhc, n_keys, qd = keys.shape
    once = pl.Buffered(1)
    return pl.pallas_call(
        _peer_scores_kernel,
        grid=(n // tn,),
        in_specs=[
            pl.BlockSpec((d, tn), lambda t: (0, t)),
            pl.BlockSpec((nhc * qd, d), lambda t: (0, 0), pipeline_mode=once),
            pl.BlockSpec((nhc, n_keys, qd), lambda t: (0, 0, 0), pipeline_mode=once),
        ],
        out_specs=pl.BlockSpec((nhc, n_keys, tn), lambda t: (0, 0, t)),
        out_shape=jax.ShapeDtypeStruct((nhc, n_keys, n), F32),
        scratch_shapes=[pltpu.VMEM((nhc * qd, tn), BF16)],
        compiler_params=_params(("parallel",)),
        name="peer_scores",
    )(ht, wqt, keys)


def _top_desc(s, k):
    cur = jnp.max(s, axis=0, keepdims=True)
    out = [cur]
    for _ in range(k - 1):
        cur = jnp.max(jnp.where(s < cur, s, -jnp.inf), axis=0, keepdims=True)
        out.append(cur)
    return out


def _peer_route_kernel(s_ref, e1_ref, e2_ref, tau_ref):
    s1 = s_ref[0]
    s2 = s_ref[1]
    a = _top_desc(s1, PEER_TOPK)
    b = jnp.concatenate(_top_desc(s2, PEER_TOPK), axis=0)
    cand = jnp.concatenate([ai + b[:PEER_TOPK // (i + 1), :] for i, ai in enumerate(a)], axis=0)
    best = _top_desc(cand, PEER_TOPK)
    tau = best[-1]
    z = sum(jnp.exp(c - best[0]) for c in best)
    e1_ref[0] = jnp.exp(s1 - a[0]) / z
    e2_ref[0] = jnp.exp(s2 - b[0:1, :])
    tau_ref[0] = tau


def _peer_route(scores, tn):
    nhc, n_keys, n = scores.shape
    nh = nhc // 2
    blk = pl.BlockSpec((1, n_keys, tn), lambda t, h: (h, 0, t))
    return pl.pallas_call(
        _peer_route_kernel,
        grid=(n // tn, nh),
        in_specs=[pl.BlockSpec((2, n_keys, tn), lambda t, h: (h, 0, t))],
        out_specs=(blk, blk, pl.BlockSpec((1, 1, tn), lambda t, h: (h, 0, t))),
        out_shape=(jax.ShapeDtypeStruct((nh, n_keys, n), F32), jax.ShapeDtypeStruct((nh, n_keys, n), F32),
                   jax.ShapeDtypeStruct((nh, 1, n), F32)),
        compiler_params=_params(("parallel", "arbitrary")),
        name="peer_route",
    )(scores)


def _gelu(x):
    return 0.5 * x * (1.0 + lax.erf(x * (2.0 ** -0.5)))


def _peer_dense_kernel(ht_ref, x1_ref, s_ref, e1_ref, e2_ref, tau_ref, u_ref, v_ref, o_ref,
                       acc_ref, act_ref, wt_ref, wtt_ref, *, ib, nh, n_keys, lane_chunk):
    e_idx = pl.program_id(1)
    n_blocks = pl.num_programs(1) - 1
    tn = ht_ref.shape[1]

    @pl.when(e_idx == 0)
    def _():
        acc_ref[...] = jnp.zeros_like(acc_ref)
        wtt_ref[1] = jnp.zeros(wtt_ref.shape[1:], wtt_ref.dtype)

    blk = jnp.minimum(e_idx, n_blocks - 1)
    d = v_ref.shape[1]
    dc = d // ib
    prev_slot = (e_idx + 1) % 2
    slot = e_idx % 2

    def first_matmul(i):
        rs = slice(i * n_keys, (i + 1) * n_keys)
        act_ref[rs, :] = _gelu(jnp.dot(u_ref[rs, :], ht_ref[...], preferred_element_type=F32))

    def gate_tile(i, ch):
        i1 = blk * ib + i
        rs = slice(i * n_keys, (i + 1) * n_keys)
        cs = slice(ch * lane_chunk, (ch + 1) * lane_chunk)
        gsum = None
        for h in range(nh):
            s1row = s_ref[2 * h, pl.ds(i1, 1), cs]
            e1row = e1_ref[h, pl.ds(i1, 1), cs]
            keep = (s1row + s_ref[2 * h + 1, :, cs]) >= tau_ref[h, :, cs]
            term = jnp.where(keep, e1row * e2_ref[h, :, cs], 0.0)
            gsum = term if gsum is None else gsum + term
        wt_ref[rs, cs] = gsum * act_ref[rs, cs]

    def second_matmul(i):
        cols = slice(i * dc, (i + 1) * dc)
        acc_ref[:, cols] += jnp.dot(wtt_ref[prev_slot], v_ref[:, cols], preferred_element_type=F32)

    n_ch = tn // lane_chunk
    first_matmul(0)
    for i in range(ib):
        if i + 1 < ib:
            first_matmul(i + 1)
        for ch in range(n_ch):
            gate_tile(i, ch)
            if ch == 0:
                second_matmul(i)
        rs = slice(i * n_keys, (i + 1) * n_keys)
        wtt_ref[slot, :, rs] = wt_ref[rs, :].T.astype(BF16)

    @pl.when(e_idx == n_blocks)
    def _():
        o_ref[...] = x1_ref[...] + acc_ref[...]


def _peer_dense(ht, x1, scores, e1, e2, tau, u_bf, v_bf, tn, ib):
    d, n = ht.shape
    nh, n_keys, _ = e1.shape
    n_exp = u_bf.shape[0]
    ec = ib * n_keys
    n_blocks = n_exp // ec
    once = pl.Buffered(1)
    body = functools.partial(_peer_dense_kernel, ib=ib, nh=nh, n_keys=n_keys, lane_chunk=min(tn, 2 * LANES))
    return pl.pallas_call(
        body,
        grid=(n // tn, n_blocks + 1),
        in_specs=[
            pl.BlockSpec((d, tn), lambda t, e: (0, t)),
            pl.BlockSpec((tn, d), lambda t, e: (t, 0), pipeline_mode=once),
            pl.BlockSpec((2 * nh, n_keys, tn), lambda t, e: (0, 0, t), pipeline_mode=once),
            pl.BlockSpec((nh, n_keys, tn), lambda t, e: (0, 0, t), pipeline_mode=once),
            pl.BlockSpec((nh, n_keys, tn), lambda t, e: (0, 0, t), pipeline_mode=once),
            pl.BlockSpec((nh, 1, tn), lambda t, e: (0, 0, t)),
            pl.BlockSpec((ec, d), lambda t, e: (jnp.minimum(e, n_blocks - 1), 0)),
            pl.BlockSpec((ec, d), lambda t, e: (jnp.maximum(e - 1, 0), 0)),
        ],
        out_specs=pl.BlockSpec((tn, d), lambda t, e: (t, 0)),
        out_shape=jax.ShapeDtypeStruct((n, d), F32),
        scratch_shapes=[pltpu.VMEM((tn, d), F32), pltpu.VMEM((ec, tn), F32), pltpu.VMEM((ec, tn), F32),
                        pltpu.VMEM((2, tn, ec), BF16)],
        compiler_params=_params(("parallel", "arbitrary")),
        name="peer_dense",
    )(ht, x1, scores, e1, e2, tau, u_bf, v_bf)


def _tile(n, pref):
    return pref if n % pref == 0 else n


def _layer(x, attend, shift0, s0, t_real, lam_init, lw):
    b, t, d = x.shape
    n = b * t
    xf = x.reshape(n, d)
    w_da = lw["w_q"].shape[1]
    tm = _tile(n, 512)
    h1 = _rms(xf, lw["norm1_g"], tm)
    tmp = _tile(n, 1024)
    qb = _proj("q", h1, lw["w_q"], tmp, _tile(w_da, 512), gain=lw["q_gain"], group=lw["dk"], scale=lw["dk"] ** -0.5)
    k, kb = _proj("k", h1, lw["w_k"], tmp, _tile(w_da, 512), gain=lw["k_gain"], group=lw["dk"])
    v, vb = _proj("dual", h1, lw["w_v"], tmp, _tile(w_da, 512))
    p_pad = lw["w_rw"].shape[1]
    rw = _proj("plain", h1, lw["w_rw"], tmp, p_pad // 3)

    o_da = attend(qb.reshape(b, t, w_da), kb.reshape(b, t, w_da), vb.reshape(b, t, w_da))

    rw3 = rw.reshape(b, t, p_pad)
    if t % RW_CHUNK:
        rw3 = jnp.pad(rw3, ((0, 0), (0, RW_CHUNK - t % RW_CHUNK), (0, 0)))
    o_rw, s_out = _rwkv(rw3, shift0, s0, lw["rwkv"], t_real)
    o_rw = o_rw[:, :t].reshape(n, -1)

    x1, h2t = _outproj(xf, o_da.reshape(n, w_da), o_rw, lw["w_out_da"], lw["w_out_rw"], lw["norm2_g"],
                       _tile(n, 256))
    tn = _tile(n, 512)
    scores = _peer_scores(h2t, lw["wq_t"], lw["peer_keys"], tn)
    e1, e2, tau = _peer_route(scores, tn)
    out = _peer_dense(h2t, x1, scores, e1, e2, tau, lw["peer_u"], lw["peer_v"], tn, 4)
    return out.reshape(b, t, d), k, v, s_out, rw3[:, t - 1, :lw["rw_proj"]]


def kernel(x_prompt, x_sample, cache_k, cache_v, state_wkv, state_shift, page_table, norm1_g, w_in, da_qk_g,
           da_lambda, da_subln_g, rw_mu, rw_w0, rw_w2, rw_a0, rw_a2, rw_g2, rw_kk, rw_ka, rw_rk, rw_lnx,
           w_out, norm2_g, peer_wq, peer_keys, peer_u, peer_v):
    depth = w_in.shape[0]
    d = x_prompt.shape[-1]
    heads, dk = cache_k.shape[3], cache_k.shape[5]
    w_da = heads * 2 * dk
    rw_heads, head = state_wkv.shape[2], state_wkv.shape[3]
    wd = rw_heads * head
    r_w, r_a, r_g = rw_w2.shape[1], rw_a2.shape[1], rw_g2.shape[1]
    rw_proj = 3 * wd + r_w + r_a + r_g
    g_pad = 2 * LANES
    p_pad = 3 * wd + LANES + g_pad
    assert d == w_da + wd and head * 2 == LANES and wd % LANES == 0 and 2 * dk == LANES
    assert r_w + r_a == LANES and r_g <= g_pad and p_pad % 3 == 0 and (p_pad // 3) % LANES == 0
    assert cache_k.shape[2] == PAGE_SIZE
    nh, _, n_keys, qd = peer_keys.shape[1:]
    bsz = x_prompt.shape[0]
    dec_b, dec_t = x_sample.shape[:2]
    assert dec_t * heads <= PAGE_SIZE

    xp, xs = x_prompt, x_sample
    outs = [[] for _ in range(8)]
    for l in range(depth):
        lam_init = 0.8 - 0.6 * math.exp(-0.3 * l)
        wl = w_in[l].astype(BF16)
        w_rw = jnp.pad(wl[:, 3 * w_da:], ((0, 0), (0, p_pad - rw_proj)))
        zero_wa = jnp.zeros((r_w, wd), F32)
        wwa = jnp.concatenate([jnp.concatenate([rw_w2[l], zero_wa], axis=1),
                               jnp.concatenate([zero_wa, rw_a2[l]], axis=1)], axis=0).astype(BF16)
        row = lambda z: z.reshape(1, -1)
        rwkv_prm = dict(
            mu=jnp.pad(row(rw_mu[l]), ((0, 0), (0, p_pad - rw_proj))),
            w0=row(rw_w0[l]), a0=row(rw_a0[l]), wwa=wwa,
            g2=jnp.pad(rw_g2[l], ((0, g_pad - r_g), (0, 0))).astype(BF16),
            kk=row(rw_kk[l]), ka=row(rw_ka[l]), rk=row(rw_rk[l]),
            lng=row(rw_lnx[l, 0]), lnb=row(rw_lnx[l, 1]),
            tril=jnp.tril(jnp.ones((RW_CHUNK, RW_CHUNK), BF16)),
            bd=_block_diag_ones(LANES, head), r_w=r_w, head=head)
        lw = dict(
            norm1_g=norm1_g[l], w_q=wl[:, :w_da], w_k=wl[:, w_da:2 * w_da], w_v=wl[:, 2 * w_da:3 * w_da],
            w_rw=w_rw, q_gain=jnp.tile(da_qk_g[l, 0], 2 * heads), k_gain=jnp.tile(da_qk_g[l, 1], 2 * heads),
            dk=dk, rwkv=rwkv_prm, rw_proj=rw_proj,
            w_out_da=w_out[l, :w_da].astype(BF16), w_out_rw=w_out[l, w_da:].astype(BF16),
            norm2_g=norm2_g[l], wq_t=peer_wq[l].T.astype(BF16),
            peer_keys=peer_keys[l].reshape(nh * 2, n_keys, qd).astype(BF16),
            peer_u=peer_u[l].astype(BF16), peer_v=peer_v[l].astype(BF16))

        attend_p = functools.partial(_attn_prompt, da_lambda=da_lambda[l], subln_g=da_subln_g[l], heads=heads,
                                     lam_init=lam_init, tq=_tile(x_prompt.shape[1], 512))
        n_pairs = wd // LANES
        xp, kp, vp, sp, shp = _layer(
            xp, attend_p, jnp.zeros((bsz, 1, p_pad), F32), jnp.zeros((bsz, n_pairs, LANES, LANES), F32),
            RW_CHUNK, lam_init, lw)

        attend_s = functools.partial(_attn_sample, cache_k=cache_k, cache_v=cache_v, layer=l,
                                     page_table=page_table, da_lambda=da_lambda[l], subln_g=da_subln_g[l],
                                     heads=heads, lam_init=lam_init,
                                     pp=8 if page_table.shape[1] % 8 == 0 else 4)
        shift_s = jnp.pad(state_shift[l], ((0, 0), (0, p_pad - rw_proj))).reshape(dec_b, 1, p_pad)
        xs, ksm, vsm, ssm, shs = _layer(xs, attend_s, shift_s, _pack_state(state_wkv[l], head), dec_t, lam_init, lw)

        outs[0].append(kp.reshape(bsz, -1, heads, 2, dk))
        outs[1].append(vp.reshape(bsz, -1, heads, 2 * dk))
        outs[2].append(_unpack_state(sp, head))
        outs[3].append(shp)
        outs[4].append(ksm.reshape(dec_b, dec_t, heads, 2, dk))
        outs[5].append(vsm.reshape(dec_b, dec_t, heads, 2 * dk))
        outs[6].append(_unpack_state(ssm, head))
        outs[7].append(shs)
    return (xp, xs) + tuple(jnp.stack(o) for o in outs)
```

```python
import functools
import math

import jax
import jax.numpy as jnp
from jax import lax
from jax.experimental import pallas as pl
from jax.experimental.pallas import tpu as pltpu

F32 = jnp.float32
BF16 = jnp.bfloat16

LANES = 128
RMS_EPS = 1e-6
LNX_EPS = 64e-5
PEER_TOPK = 16
PAGE_SIZE = 128
RW_CHUNK = 64
VMEM_LIMIT = 48 * 1024 * 1024

_NT = (((1,), (1,)), ((), ()))
_TN = (((0,), (0,)), ((), ()))


def _params(sem):
    return pltpu.CompilerParams(dimension_semantics=sem, vmem_limit_bytes=VMEM_LIMIT)


def _mm(a, b):
    return jnp.dot(a.astype(BF16), b.astype(BF16), preferred_element_type=F32)


def _mm_nt(a, b):
    return lax.dot_general(a.astype(BF16), b.astype(BF16), _NT, preferred_element_type=F32)


def _mm_tn(a, b):
    return lax.dot_general(a.astype(BF16), b.astype(BF16), _TN, preferred_element_type=F32)


def _mm_split(a_exact_bf16, b):
    hi = b.astype(BF16)
    lo = (b - hi.astype(F32)).astype(BF16)
    return (jnp.dot(a_exact_bf16, hi, preferred_element_type=F32)
            + jnp.dot(a_exact_bf16, lo, preferred_element_type=F32))


def _mm_split_lhs(a, b_exact_bf16):
    hi = a.astype(BF16)
    lo = (a - hi.astype(F32)).astype(BF16)
    return (jnp.dot(hi, b_exact_bf16, preferred_element_type=F32)
            + jnp.dot(lo, b_exact_bf16, preferred_element_type=F32))


def _block_diag_ones(n, group):
    i = jnp.arange(n)
    return (i[:, None] // group == i[None, :] // group).astype(BF16)


def _rms_kernel(x_ref, g_ref, o_ref):
    x = x_ref[...]
    ms = jnp.mean(x * x, axis=-1, keepdims=True)
    o_ref[...] = (x * lax.rsqrt(ms + RMS_EPS) * g_ref[...]).astype(o_ref.dtype)


def _rms(x, g, tm):
    n, d = x.shape
    return pl.pallas_call(
        _rms_kernel,
        grid=(n // tm,),
        in_specs=[pl.BlockSpec((tm, d), lambda i: (i, 0)), pl.BlockSpec((1, d), lambda i: (0, 0))],
        out_specs=pl.BlockSpec((tm, d), lambda i: (i, 0)),
        out_shape=jax.ShapeDtypeStruct((n, d), BF16),
        compiler_params=_params(("parallel",)),
        name="rms",
    )(x, g.reshape(1, d))


def _proj_plain_kernel(h_ref, w_ref, o_ref):
    o_ref[...] = jnp.dot(h_ref[...], w_ref[...], preferred_element_type=F32)


def _proj_dual_kernel(h_ref, w_ref, o_ref, ob_ref):
    y = jnp.dot(h_ref[...], w_ref[...], preferred_element_type=F32)
    o_ref[...] = y
    ob_ref[...] = y.astype(BF16)


def _qk_normed(h_ref, w_ref, bd_ref, g_ref, group):
    y = jnp.dot(h_ref[...], w_ref[...], preferred_element_type=F32)
    ms = _mm_split_lhs(y * y, bd_ref[...]) * (1.0 / group)
    return y * lax.rsqrt(ms + RMS_EPS) * g_ref[...]


def _proj_q_kernel(h_ref, w_ref, bd_ref, g_ref, ob_ref, *, group, scale):
    ob_ref[...] = (_qk_normed(h_ref, w_ref, bd_ref, g_ref, group) * scale).astype(BF16)


def _proj_k_kernel(h_ref, w_ref, bd_ref, g_ref, o_ref, ob_ref, *, group):
    y = _qk_normed(h_ref, w_ref, bd_ref, g_ref, group)
    o_ref[...] = y
    ob_ref[...] = y.astype(BF16)


def _proj(kind, h, w, tm, tn, gain=None, group=None, scale=None):
    n, d = h.shape
    ncol = w.shape[1]
    grid = (n // tm, ncol // tn)
    h_spec = pl.BlockSpec((tm, d), lambda i, j: (i, 0))
    w_spec = pl.BlockSpec((d, tn), lambda i, j: (0, j))
    o_spec = pl.BlockSpec((tm, tn), lambda i, j: (i, j))
    f32_out = jax.ShapeDtypeStruct((n, ncol), F32)
    bf_out = jax.ShapeDtypeStruct((n, ncol), BF16)
    if kind in ("q", "k"):
        bd = _block_diag_ones(tn, group)
        extra = [bd, gain.reshape(1, ncol)]
        extra_specs = [pl.BlockSpec((tn, tn), lambda i, j: (0, 0)), pl.BlockSpec((1, tn), lambda i, j: (0, j))]
    else:
        extra, extra_specs = [], []
    if kind == "q":
        body = functools.partial(_proj_q_kernel, group=group, scale=scale)
        out_shape, out_specs = bf_out, o_spec
    elif kind == "k":
        body = functools.partial(_proj_k_kernel, group=group)
        out_shape, out_specs = (f32_out, bf_out), (o_spec, o_spec)
    elif kind == "dual":
        body = _proj_dual_kernel
        out_shape, out_specs = (f32_out, bf_out), (o_spec, o_spec)
    else:
        body = _proj_plain_kernel
        out_shape, out_specs = f32_out, o_spec
    return pl.pallas_call(
        body,
        grid=grid,
        in_specs=[h_spec, w_spec] + extra_specs,
        out_specs=out_specs,
        out_shape=out_shape,
        compiler_params=_params(("parallel", "arbitrary")),
        name="proj_" + kind,
    )(h, w, *extra)


def _lambda(lam_ref, lam_init):
    ll = lam_ref[...]
    s01 = jnp.sum(ll[0:1, :] * ll[1:2, :], axis=-1, keepdims=True)
    s23 = jnp.sum(ll[2:3, :] * ll[3:4, :], axis=-1, keepdims=True)
    return jnp.exp(s01) - jnp.exp(s23) + lam_init


def _subln(o, g, lam_init):
    ms = jnp.mean(o * o, axis=-1, keepdims=True)
    return (o * lax.rsqrt(ms + RMS_EPS) * g) * (1.0 - lam_init)


def _attn_prompt_kernel(q_ref, k_ref, v_ref, lam_ref, g_ref, o_ref, acc0_ref, acc1_ref, *, tq, dk, lam_init):
    qi = pl.program_id(2)
    q = q_ref[0]
    lane = lax.broadcasted_iota(jnp.int32, q.shape, 1)
    qf = q.astype(F32)
    qc = (jnp.where(lane < dk, qf, 0.0).astype(BF16), jnp.where(lane >= dk, qf, 0.0).astype(BF16))
    accs = (acc0_ref, acc1_ref)
    acc0_ref[...] = jnp.zeros_like(acc0_ref)
    acc1_ref[...] = jnp.zeros_like(acc1_ref)
    causal = (lax.broadcasted_iota(jnp.int32, (tq, tq), 1) <= lax.broadcasted_iota(jnp.int32, (tq, tq), 0))

    def step(j, carry, diagonal):
        kb = k_ref[0, pl.ds(pl.multiple_of(j * tq, tq), tq), :]
        vb = v_ref[0, pl.ds(pl.multiple_of(j * tq, tq), tq), :]
        new = []
        for c in range(2):
            m, l = carry[2 * c], carry[2 * c + 1]
            s = lax.dot_general(qc[c], kb, _NT, preferred_element_type=F32)
            if diagonal:
                s = jnp.where(causal, s, -jnp.inf)
            m_new = jnp.maximum(m, jnp.max(s, axis=-1, keepdims=True))
            corr = jnp.exp(m - m_new)
            p = jnp.exp(s - m_new)
            l = l * corr + jnp.sum(p, axis=-1, keepdims=True)
            accs[c][...] = accs[c][...] * corr + jnp.dot(p.astype(BF16), vb, preferred_element_type=F32)
            new += [m_new, l]
        return tuple(new)

    minit = jnp.full((tq, 1), -jnp.inf, F32)
    linit = jnp.zeros((tq, 1), F32)
    carry = lax.fori_loop(0, qi, functools.partial(step, diagonal=False), (minit, linit, minit, linit))
    _, l0, _, l1 = step(qi, carry, True)
    lam = _lambda(lam_ref, lam_init)
    o = acc0_ref[...] / l0 - lam * (acc1_ref[...] / l1)
    o_ref[0] = _subln(o, g_ref[...], lam_init).astype(o_ref.dtype)


def _attn_prompt(qb, kb, vb, da_lambda, subln_g, heads, lam_init, tq):
    b, s, w = qb.shape
    dv = w // heads
    body = functools.partial(_attn_prompt_kernel, tq=tq, dk=dv // 2, lam_init=lam_init)
    return pl.pallas_call(
        body,
        grid=(b, heads, s // tq),
        in_specs=[
            pl.BlockSpec((1, tq, dv), lambda bi, h, qi: (bi, qi, h)),
            pl.BlockSpec((1, s, dv), lambda bi, h, qi: (bi, 0, h)),
            pl.BlockSpec((1, s, dv), lambda bi, h, qi: (bi, 0, h)),
            pl.BlockSpec(da_lambda.shape, lambda bi, h, qi: (0, 0)),
            pl.BlockSpec((1, dv), lambda bi, h, qi: (0, 0)),
        ],
        out_specs=pl.BlockSpec((1, tq, dv), lambda bi, h, qi: (bi, qi, h)),
        out_shape=jax.ShapeDtypeStruct((b, s, w), BF16),
        scratch_shapes=[pltpu.VMEM((tq, dv), F32), pltpu.VMEM((tq, dv), F32)],
        compiler_params=_params(("parallel", "parallel", "arbitrary")),
        name="attn_prompt",
    )(qb, kb, vb, da_lambda, subln_g.reshape(1, dv))


def _attn_sample_kernel(pt_ref, qrep_ref, kst_ref, vs_ref, lam_ref, g_ref, *rest, pp, heads, t, dk, lam_init):
    kc_refs = rest[:pp]
    vc_refs = rest[pp:2 * pp]
    o_ref = rest[2 * pp]
    qbd_ref, m_ref, l_ref, acc_ref = rest[2 * pp + 1:]
    del pt_ref
    p_idx = pl.program_id(1)
    rows = 2 * t * heads
    dv = 2 * dk
    w = heads * dv

    @pl.when(p_idx == 0)
    def _():
        r = lax.broadcasted_iota(jnp.int32, (rows, w), 0)
        col = lax.broadcasted_iota(jnp.int32, (rows, w), 1)
        keep = (col // dk) == ((r % heads) * 2 + r // (t * heads))
        qbd_ref[...] = jnp.where(keep, qrep_ref[0].astype(F32), 0.0).astype(BF16)
        m_ref[...] = jnp.full(m_ref.shape, -jnp.inf, F32)
        l_ref[...] = jnp.zeros(l_ref.shape, F32)
        acc_ref[...] = jnp.zeros(acc_ref.shape, F32)

    qbd = qbd_ref[...]

    def online(s, vbs):
        m = m_ref[...]
        m_new = jnp.maximum(m, jnp.max(s, axis=-1, keepdims=True))
        corr = jnp.exp(m - m_new)
        p = jnp.exp(s - m_new)
        l_ref[...] = l_ref[...] * corr + jnp.sum(p, axis=-1, keepdims=True)
        pv = None
        for i, vb in enumerate(vbs):
            n = vb.shape[0]
            d = jnp.dot(p[:, i * n:(i + 1) * n].astype(BF16), vb, preferred_element_type=F32)
            pv = d if pv is None else pv + d
        acc_ref[...] = acc_ref[...] * corr + pv
        m_ref[...] = m_new

    kt = jnp.concatenate([kc[0] for kc in kc_refs], axis=1).astype(BF16)
    vwide = [jnp.concatenate([vc[0, pl.ds(h, PAGE_SIZE, stride=heads), :] for h in range(heads)], axis=1)
             .astype(BF16) for vc in vc_refs]
    online(jnp.dot(qbd, kt, preferred_element_type=F32), vwide)

    @pl.when(p_idx == pl.num_programs(1) - 1)
    def _():
        s = jnp.dot(qbd, kst_ref[0], preferred_element_type=F32)
        r = lax.broadcasted_iota(jnp.int32, s.shape, 0)
        col = lax.broadcasted_iota(jnp.int32, s.shape, 1)
        online(jnp.where(col <= (r // heads) % t, s, -jnp.inf), [vs_ref[0]])
        out = acc_ref[...] / l_ref[...]
        hrow = lax.broadcasted_iota(jnp.int32, (heads, w), 0)
        hcol = lax.broadcasted_iota(jnp.int32, (heads, w), 1) // dv
        pick = hrow == hcol
        lam = _lambda(lam_ref, lam_init)
        toks = []
        for tok_i in range(t):
            maps = []
            for c in range(2):
                base = (c * t + tok_i) * heads
                maps.append(jnp.sum(jnp.where(pick, out[base:base + heads, :], 0.0), axis=0, keepdims=True))
            toks.append(maps[0] - lam * maps[1])
        o = jnp.concatenate(toks, axis=0)
        g = g_ref[...]
        outs = [_subln(o[:, h * dv:(h + 1) * dv], g, lam_init) for h in range(heads)]
        o_ref[0] = jnp.concatenate(outs, axis=1).astype(o_ref.dtype)


def _attn_sample(qb, kb, vb, cache_k, cache_v, layer, page_table, da_lambda, subln_g, heads, lam_init, pp):
    bd, t, w = qb.shape
    dv = w // heads
    n_pool = cache_k.shape[1]
    n_pages = page_table.shape[1]
    rows = 2 * t * heads
    qrep = jnp.tile(jnp.repeat(qb, heads, axis=1), (1, 2, 1))
    kst_pad = jnp.pad(jnp.swapaxes(kb, 1, 2), ((0, 0), (0, 0), (0, PAGE_SIZE - t)))
    vs_pad = jnp.pad(vb, ((0, 0), (0, PAGE_SIZE - t), (0, 0)))
    ckt = jnp.transpose(cache_k, (0, 1, 3, 4, 5, 2)).reshape(-1, w, PAGE_SIZE)
    cv = cache_v.reshape(-1, PAGE_SIZE * heads, dv)
    base = layer * n_pool

    def k_spec(i):
        return pl.BlockSpec((1, w, PAGE_SIZE), lambda b, p, pt: (base + pt[b, p * pp + i], 0, 0))

    def v_spec(i):
        return pl.BlockSpec((1, PAGE_SIZE * heads, dv), lambda b, p, pt: (base + pt[b, p * pp + i], 0, 0))

    body = functools.partial(_attn_sample_kernel, pp=pp, heads=heads, t=t, dk=dv // 2, lam_init=lam_init)
    grid_spec = pltpu.PrefetchScalarGridSpec(
        num_scalar_prefetch=1,
        grid=(bd, n_pages // pp),
        in_specs=[
            pl.BlockSpec((1, rows, w), lambda b, p, pt: (b, 0, 0)),
            pl.BlockSpec((1, w, PAGE_SIZE), lambda b, p, pt: (b, 0, 0)),
            pl.BlockSpec((1, PAGE_SIZE, w), lambda b, p, pt: (b, 0, 0)),
            pl.BlockSpec(da_lambda.shape, lambda b, p, pt: (0, 0)),
            pl.BlockSpec((1, dv), lambda b, p, pt: (0, 0)),
        ] + [k_spec(i) for i in range(pp)] + [v_spec(i) for i in range(pp)],
        out_specs=pl.BlockSpec((1, t, w), lambda b, p, pt: (b, 0, 0)),
        scratch_shapes=[pltpu.VMEM((rows, w), BF16), pltpu.VMEM((rows, 1), F32),
                        pltpu.VMEM((rows, 1), F32), pltpu.VMEM((rows, w), F32)],
    )
    return pl.pallas_call(
        body,
        grid_spec=grid_spec,
        out_shape=jax.ShapeDtypeStruct((bd, t, w), BF16),
        compiler_params=_params(("parallel", "arbitrary")),
        name="attn_sample",
    )(page_table, qrep, kst_pad, vs_pad, da_lambda, subln_g.reshape(1, dv), *([ckt] * pp), *([cv] * pp))


def _softplus(z):
    return jnp.maximum(z, 0.0) + jnp.log1p(jnp.exp(-jnp.abs(z)))


def _sigmoid(z):
    return 1.0 / (1.0 + jnp.exp(-z))


def _split_heads(x, first):
    zero = jnp.zeros_like(x)
    return jnp.concatenate([jnp.where(first, x, zero), jnp.where(first, zero, x)], axis=0)


def _rwkv_kernel(rw_ref, shift_ref, s0_ref, mu_ref, w0_ref, a0_ref, wwa_ref, g2_ref, kk_ref, ka_ref,
                 rk_ref, lng_ref, lnb_ref, tril_ref, bd_ref, o_ref, sout_ref, state_ref, prev_ref,
                 *, wd, r_w, t_real, head):
    c_idx = pl.program_id(1)
    cl = RW_CHUNK
    pairs = range(wd // LANES)

    @pl.when(c_idx == 0)
    def _():
        state_ref[...] = s0_ref[0]
        prev_ref[...] = shift_ref[0]

    x = rw_ref[0]
    rowp = lax.broadcasted_iota(jnp.int32, x.shape, 0)
    prevx = jnp.where(rowp == 0, prev_ref[...], pltpu.roll(x, 1, axis=0))
    prev_ref[...] = x[cl - 1:cl, :]
    xm = x + (prevx - x) * mu_ref[...]
    r = xm[:, 0:wd]
    kr = xm[:, wd:2 * wd]
    vr = xm[:, 2 * wd:3 * wd]
    slab = xm[:, 3 * wd:3 * wd + LANES]
    lane_s = lax.broadcasted_iota(jnp.int32, slab.shape, 1)
    wa = _mm(jnp.where(lane_s < r_w, jnp.tanh(slab), slab), wwa_ref[...])
    gd = xm[:, 3 * wd + LANES:]
    g = _mm(_sigmoid(gd), g2_ref[...])
    w = -_softplus(-(w0_ref[...] + wa[:, :wd])) - 0.5
    lw = -jnp.exp(w)
    a = _sigmoid(a0_ref[...] + wa[:, wd:])
    kk = kr * kk_ref[...]
    kh = kr * (1.0 + (a - 1.0) * ka_ref[...])
    beta_scale = a
    if t_real < cl:
        valid = lax.broadcasted_iota(jnp.int32, (cl, wd), 0) < t_real
        lw = jnp.where(valid, lw, 0.0)
        beta_scale = jnp.where(valid, a, 0.0)
        kh_state = jnp.where(valid, kh, 0.0)
    else:
        kh_state = kh
    cum = _mm_split(tril_ref[...], lw)
    cum_last = cum[cl - 1:cl, :]
    e_neg = jnp.exp(-cum)
    e_end = jnp.exp(cum_last - cum)
    w_end = jnp.exp(cum_last)
    rt_all = r * jnp.exp(cum)
    ex_all = jnp.exp(cum - lw)
    bonus_in = r * kh * rk_ref[...]

    lane = lax.broadcasted_iota(jnp.int32, (cl, LANES), 1)
    first = lane < head
    tt = lax.broadcasted_iota(jnp.int32, (cl, 2 * cl), 0)
    ss = lax.broadcasted_iota(jnp.int32, (cl, 2 * cl), 1) % cl
    strict = ss < tt
    incl = ss <= tt
    firstw = lax.broadcasted_iota(jnp.int32, (cl, 2 * cl), 1) < cl
    r2 = lax.broadcasted_iota(jnp.int32, (2 * cl, 2 * cl), 0)
    c2 = lax.broadcasted_iota(jnp.int32, (2 * cl, 2 * cl), 1)
    eye = (r2 == c2).astype(F32)
    rs = lax.broadcasted_iota(jnp.int32, (LANES, LANES), 0) // head
    cs = lax.broadcasted_iota(jnp.int32, (LANES, LANES), 1) // head
    same_head = rs == cs
    bd = bd_ref[...]
    n_levels = int(math.log2(cl))
    sls = [slice(p * LANES, (p + 1) * LANES) for p in pairs]

    kk_p = [kk[:, sl] for sl in sls]
    nrm = [jnp.sqrt(_mm(k * k, bd)) for k in kk_p]
    kkn = [k / jnp.maximum(n, 1e-12) for k, n in zip(kk_p, nrm)]
    beta = [kkn[p] * beta_scale[:, sls[p]] for p in pairs]
    v_p = [vr[:, sl] for sl in sls]
    st = [state_ref[p] for p in pairs]
    lhs = [jnp.concatenate([-ex_all[:, sls[p]] * kkn[p], rt_all[:, sls[p]]], axis=0) for p in pairs]
    rhs = [jnp.concatenate([_split_heads(beta[p] * e_neg[:, sls[p]], first),
                            _split_heads(kh_state[:, sls[p]] * e_neg[:, sls[p]], first)], axis=0) for p in pairs]
    m = [_mm_nt(lhs[p], rhs[p]) for p in pairs]
    sa = [_mm_nt(lhs[p], st[p]) for p in pairs]
    aab = [jnp.where(strict, x_[:cl, :2 * cl], 0.0) for x_ in m]
    aak = [jnp.where(strict, x_[:cl, 2 * cl:], 0.0) for x_ in m]
    arbk = [jnp.concatenate([jnp.where(incl, x_[cl:, :2 * cl], 0.0), jnp.where(incl, x_[cl:, 2 * cl:], 0.0)], axis=1)
            for x_ in m]
    abd = [_split_heads(x_, firstw) for x_ in aab]
    vbd = [_split_heads(x_, first) for x_ in v_p]
    rhs_u = [sa[p][:cl, :] + _mm(aak[p], vbd[p]) for p in pairs]
    tinv = [eye + x_ for x_ in abd]
    xp = [_mm(x_, x_) for x_ in abd]
    for lvl in range(1, n_levels):
        if lvl < n_levels - 1:
            both = [_mm(xp[p], jnp.concatenate([tinv[p], xp[p]], axis=1)) for p in pairs]
            tinv = [tinv[p] + both[p][:, :2 * cl] for p in pairs]
            xp = [both[p][:, 2 * cl:] for p in pairs]
        else:
            tinv = [tinv[p] + _mm(xp[p], tinv[p]) for p in pairs]
    tw = [x_[:cl, :] + x_[cl:, :] for x_ in tinv]
    u = [_mm(tw[p], _split_heads(rhs_u[p], first)) for p in pairs]
    y = [sa[p][cl:, :] + _mm(arbk[p], jnp.concatenate([_split_heads(u[p], first), vbd[p]], axis=0)) for p in pairs]
    ds = [_mm_tn(jnp.concatenate([u[p], v_p[p]], axis=0),
                 jnp.concatenate([beta[p] * e_end[:, sls[p]], kh_state[:, sls[p]] * e_end[:, sls[p]]], axis=0))
          for p in pairs]
    mean = [_mm(x_, bd) * (1.0 / head) for x_ in y]
    dlt = [y[p] - mean[p] for p in pairs]
    var = [_mm(x_ * x_, bd) * (1.0 / head) for x_ in dlt]
    bonus = [_mm(bonus_in[:, sls[p]], bd) * v_p[p] for p in pairs]
    for p in pairs:
        sl = sls[p]
        state_ref[p] = st[p] * w_end[:, sl] + jnp.where(same_head, ds[p], 0.0)
        yn = dlt[p] * lax.rsqrt(var[p] + LNX_EPS) * lng_ref[:, sl] + lnb_ref[:, sl]
        o_ref[0, :, sl] = ((yn + bonus[p]) * g[:, sl]).astype(o_ref.dtype)

    @pl.when(c_idx == pl.num_programs(1) - 1)
    def _():
        sout_ref[0] = state_ref[...]


def _rwkv(rw, shift0, s0, prm, t_real):
    b, s, p = rw.shape
    wd = prm["w0"].shape[1]
    n_pairs = wd // LANES
    cl = RW_CHUNK
    body = functools.partial(_rwkv_kernel, wd=wd, r_w=prm["r_w"], t_real=t_real, head=prm["head"])

    def const(shape):
        return pl.BlockSpec(shape, lambda bi, ci: (0,) * len(shape))

    names = ["mu", "w0", "a0", "wwa", "g2", "kk", "ka", "rk", "lng", "lnb", "tril", "bd"]
    return pl.pallas_call(
        body,
        grid=(b, s // cl),
        in_specs=[
            pl.BlockSpec((1, cl, p), lambda bi, ci: (bi, ci, 0)),
            pl.BlockSpec((1, 1, p), lambda bi, ci: (bi, 0, 0)),
            pl.BlockSpec((1, n_pairs, LANES, LANES), lambda bi, ci: (bi, 0, 0, 0)),
        ] + [const(prm[nm].shape) for nm in names],
        out_specs=(pl.BlockSpec((1, cl, wd), lambda bi, ci: (bi, ci, 0)),
                   pl.BlockSpec((1, n_pairs, LANES, LANES), lambda bi, ci: (bi, 0, 0, 0))),
        out_shape=(jax.ShapeDtypeStruct((b, s, wd), BF16),
                   jax.ShapeDtypeStruct((b, n_pairs, LANES, LANES), F32)),
        scratch_shapes=[pltpu.VMEM((n_pairs, LANES, LANES), F32), pltpu.VMEM((1, p), F32)],
        compiler_params=_params(("parallel", "arbitrary")),
        name="rwkv",
    )(rw, shift0, s0, *[prm[nm] for nm in names])


def _pack_state(wkv, head):
    b, h = wkv.shape[:2]
    pairs = wkv.reshape(b, h // 2, 2, head, head)
    z = jnp.zeros((b, h // 2, head, head), wkv.dtype)
    top = jnp.concatenate([pairs[:, :, 0], z], axis=-1)
    bot = jnp.concatenate([z, pairs[:, :, 1]], axis=-1)
    return jnp.concatenate([top, bot], axis=-2)


def _unpack_state(sbd, head):
    b, n_pairs = sbd.shape[:2]
    h0 = sbd[:, :, :head, :head]
    h1 = sbd[:, :, head:, head:]
    return jnp.stack([h0, h1], axis=2).reshape(b, 2 * n_pairs, head, head)


def _outproj_kernel(x_ref, oda_ref, orw_ref, w1_ref, w2_ref, g_ref, x1_ref, h_ref, ht_ref):
    y = (jnp.dot(oda_ref[...], w1_ref[...], preferred_element_type=F32)
         + jnp.dot(orw_ref[...], w2_ref[...], preferred_element_type=F32))
    x1 = x_ref[...] + y
    x1_ref[...] = x1
    ms = jnp.mean(x1 * x1, axis=-1, keepdims=True)
    h2 = x1 * lax.rsqrt(ms + RMS_EPS) * g_ref[...]
    h_ref[...] = h2.astype(BF16)
    ht_ref[...] = h2.T.astype(BF16)


def _outproj(x, oda, orw, w1, w2, g, tm):
    n, d = x.shape
    wa, wr = oda.shape[1], orw.shape[1]
    return pl.pallas_call(
        _outproj_kernel,
        grid=(n // tm,),
        in_specs=[
            pl.BlockSpec((tm, d), lambda i: (i, 0)),
            pl.BlockSpec((tm, wa), lambda i: (i, 0)),
            pl.BlockSpec((tm, wr), lambda i: (i, 0)),
            pl.BlockSpec((wa, d), lambda i: (0, 0)),
            pl.BlockSpec((wr, d), lambda i: (0, 0)),
            pl.BlockSpec((1, d), lambda i: (0, 0)),
        ],
        out_specs=(pl.BlockSpec((tm, d), lambda i: (i, 0)), pl.BlockSpec((tm, d), lambda i: (i, 0)),
                   pl.BlockSpec((d, tm), lambda i: (0, i))),
        out_shape=(jax.ShapeDtypeStruct((n, d), F32), jax.ShapeDtypeStruct((n, d), BF16),
                   jax.ShapeDtypeStruct((d, n), BF16)),
        compiler_params=_params(("parallel",)),
        name="outproj",
    )(x, oda, orw, w1, w2, g.reshape(1, d))


def _peer_scores_kernel(ht_ref, wqt_ref, keys_ref, s_ref, q_ref):
    q_ref[...] = jnp.dot(wqt_ref[...], ht_ref[...], preferred_element_type=F32).astype(BF16)
    nhc, _, qd = keys_ref.shape
    for j in range(nhc):
        s_ref[j] = jnp.dot(keys_ref[j], q_ref[j * qd:(j + 1) * qd, :], preferred_element_type=F32)


def _peer_scores(ht, wqt, keys, tn):
    d, n = ht.shape
    nhc, n_keys, qd = keys.shape
    once = pl.Buffered(1)
    return pl.pallas_call(
        _peer_scores_kernel,
        grid=(n // tn,),
        in_specs=[
            pl.BlockSpec((d, tn), lambda t: (0, t)),
            pl.BlockSpec((nhc * qd, d), lambda t: (0, 0), pipeline_mode=once),
            pl.BlockSpec((nhc, n_keys, qd), lambda t: (0, 0, 0), pipeline_mode=once),
        ],
        out_specs=pl.BlockSpec((nhc, n_keys, tn), lambda t: (0, 0, t)),
        out_shape=jax.ShapeDtypeStruct((nhc, n_keys, n), F32),
        scratch_shapes=[pltpu.VMEM((nhc * qd, tn), BF16)],
        compiler_params=_params(("parallel",)),
        name="peer_scores",
    )(ht, wqt, keys)


def _top_desc(s, k):
    cur = jnp.max(s, axis=0, keepdims=True)
    out = [cur]
    for _ in range(k - 1):
        cur = jnp.max(jnp.where(s < cur, s, -jnp.inf), axis=0, keepdims=True)
        out.append(cur)
    return out


def _peer_route_kernel(s_ref, e1_ref, e2_ref, tau_ref):
    s1 = s_ref[0]
    s2 = s_ref[1]
    a = _top_desc(s1, PEER_TOPK)
    b = jnp.concatenate(_top_desc(s2, PEER_TOPK), axis=0)
    cand = jnp.concatenate([ai + b[:PEER_TOPK // (i + 1), :] for i, ai in enumerate(a)], axis=0)
    best = _top_desc(cand, PEER_TOPK)
    tau = best[-1]
    z = sum(jnp.exp(c - best[0]) for c in best)
    e1_ref[0] = jnp.exp(s1 - a[0]) / z
    e2_ref[0] = jnp.exp(s2 - b[0:1, :])
    tau_ref[0] = tau


def _peer_route(scores, tn):
    nhc, n_keys, n = scores.shape
    nh = nhc // 2
    blk = pl.BlockSpec((1, n_keys, tn), lambda t, h: (h, 0, t))
    return pl.pallas_call(
        _peer_route_kernel,
        grid=(n // tn, nh),
        in_specs=[pl.BlockSpec((2, n_keys, tn), lambda t, h: (h, 0, t))],
        out_specs=(blk, blk, pl.BlockSpec((1, 1, tn), lambda t, h: (h, 0, t))),
        out_shape=(jax.ShapeDtypeStruct((nh, n_keys, n), F32), jax.ShapeDtypeStruct((nh, n_keys, n), F32),
                   jax.ShapeDtypeStruct((nh, 1, n), F32)),
        compiler_params=_params(("parallel", "arbitrary")),
        name="peer_route",
    )(scores)


def _gelu(x):
    return 0.5 * x * (1.0 + lax.erf(x * (2.0 ** -0.5)))


def _peer_dense_kernel(ht_ref, x1_ref, s_ref, e1_ref, e2_ref, tau_ref, u_ref, v_ref, o_ref,
                       acc_ref, act_ref, wt_ref, wtt_ref, *, ib, nh, n_keys, lane_chunk):
    e_idx = pl.program_id(1)
    n_blocks = pl.num_programs(1) - 1
    tn = ht_ref.shape[1]

    @pl.when(e_idx == 0)
    def _():
        acc_ref[...] = jnp.zeros_like(acc_ref)
        wtt_ref[1] = jnp.zeros(wtt_ref.shape[1:], wtt_ref.dtype)

    blk = jnp.minimum(e_idx, n_blocks - 1)
    d = v_ref.shape[1]
    dc = d // ib
    prev_slot = (e_idx + 1) % 2
    slot = e_idx % 2

    def first_matmul(i):
        rs = slice(i * n_keys, (i + 1) * n_keys)
        act_ref[rs, :] = _gelu(jnp.dot(u_ref[rs, :], ht_ref[...], preferred_element_type=F32))

    def gate_tile(i, ch):
        i1 = blk * ib + i
        rs = slice(i * n_keys, (i + 1) * n_keys)
        cs = slice(ch * lane_chunk, (ch + 1) * lane_chunk)
        gsum = None
        for h in range(nh):
            s1row = s_ref[2 * h, pl.ds(i1, 1), cs]
            e1row = e1_ref[h, pl.ds(i1, 1), cs]
            keep = (s1row + s_ref[2 * h + 1, :, cs]) >= tau_ref[h, :, cs]
            term = jnp.where(keep, e1row * e2_ref[h, :, cs], 0.0)
            gsum = term if gsum is None else gsum + term
        wt_ref[rs, cs] = gsum * act_ref[rs, cs]

    def second_matmul(i):
        cols = slice(i * dc, (i + 1) * dc)
        acc_ref[:, cols] += jnp.dot(wtt_ref[prev_slot], v_ref[:, cols], preferred_element_type=F32)

    n_ch = tn // lane_chunk
    first_matmul(0)
    for i in range(ib):
        if i + 1 < ib:
            first_matmul(i + 1)
        for ch in range(n_ch):
            gate_tile(i, ch)
            if ch == 0:
                second_matmul(i)
        rs = slice(i * n_keys, (i + 1) * n_keys)
        wtt_ref[slot, :, rs] = wt_ref[rs, :].T.astype(BF16)

    @pl.when(e_idx == n_blocks)
    def _():
        o_ref[...] = x1_ref[...] + acc_ref[...]


def _peer_dense(ht, x1, scores, e1, e2, tau, u_bf, v_bf, tn, ib):
    d, n = ht.shape
    nh, n_keys, _ = e1.shape
    n_exp = u_bf.shape[0]
    ec = ib * n_keys
    n_blocks = n_exp // ec
    once = pl.Buffered(1)
    body = functools.partial(_peer_dense_kernel, ib=ib, nh=nh, n_keys=n_keys, lane_chunk=min(tn, 2 * LANES))
    return pl.pallas_call(
        body,
        grid=(n // tn, n_blocks + 1),
        in_specs=[
            pl.BlockSpec((d, tn), lambda t, e: (0, t)),
            pl.BlockSpec((tn, d), lambda t, e: (t, 0), pipeline_mode=once),
            pl.BlockSpec((2 * nh, n_keys, tn), lambda t, e: (0, 0, t), pipeline_mode=once),
            pl.BlockSpec((nh, n_keys, tn), lambda t, e: (0, 0, t), pipeline_mode=once),
            pl.BlockSpec((nh, n_keys, tn), lambda t, e: (0, 0, t), pipeline_mode=once),
            pl.BlockSpec((nh, 1, tn), lambda t, e: (0, 0, t)),
            pl.BlockSpec((ec, d), lambda t, e: (jnp.minimum(e, n_blocks - 1), 0)),
            pl.BlockSpec((ec, d), lambda t, e: (jnp.maximum(e - 1, 0), 0)),
        ],
        out_specs=pl.BlockSpec((tn, d), lambda t, e: (t, 0)),
        out_shape=jax.ShapeDtypeStruct((n, d), F32),
        scratch_shapes=[pltpu.VMEM((tn, d), F32), pltpu.VMEM((ec, tn), F32), pltpu.VMEM((ec, tn), F32),
                        pltpu.VMEM((2, tn, ec), BF16)],
        compiler_params=_params(("parallel", "arbitrary")),
        name="peer_dense",
    )(ht, x1, scores, e1, e2, tau, u_bf, v_bf)


def _take_top(vals, k, payload=None):
    n_rows = vals.shape[0]
    rowid = lax.broadcasted_iota(jnp.int32, vals.shape, 0).astype(F32)
    cur = vals
    top, picked = [], []
    for _ in range(k):
        m = jnp.max(cur, axis=0, keepdims=True)
        sel = jnp.min(jnp.where(cur == m, rowid, float(n_rows)), axis=0, keepdims=True)
        hit = rowid == sel
        top.append(m)
        picked.append(sel if payload is None else jnp.sum(jnp.where(hit, payload, 0.0), axis=0, keepdims=True))
        cur = jnp.where(hit, -jnp.inf, cur)
    return top, picked


def _peer_pairs_kernel(s_ref, code_ref, g_ref, *, n_keys):
    s1 = s_ref[0]
    s2 = s_ref[1]
    a, ia = _take_top(s1, PEER_TOPK)
    b, ib = _take_top(s2, PEER_TOPK)
    b = jnp.concatenate(b, axis=0)
    ib = jnp.concatenate(ib, axis=0)
    counts = [PEER_TOPK // (i + 1) for i in range(PEER_TOPK)]
    cand = jnp.concatenate([a[i] + b[:c, :] for i, c in enumerate(counts)], axis=0)
    code = jnp.concatenate([ia[i] * float(n_keys) + ib[:c, :] for i, c in enumerate(counts)], axis=0)
    best, ids = _take_top(cand, PEER_TOPK, payload=code)
    ex = [jnp.exp(c - best[0]) for c in best]
    z = sum(ex)
    code_ref[...] = jnp.concatenate(ids, axis=0)
    g_ref[...] = jnp.concatenate(ex, axis=0) / z


def _peer_pairs(scores, tn):
    nhc, n_keys, n = scores.shape
    nh = nhc // 2
    blk = pl.BlockSpec((PEER_TOPK, tn), lambda t, h: (h, t))
    shape = jax.ShapeDtypeStruct((nh * PEER_TOPK, n), F32)
    return pl.pallas_call(
        functools.partial(_peer_pairs_kernel, n_keys=n_keys),
        grid=(n // tn, nh),
        in_specs=[pl.BlockSpec((2, n_keys, tn), lambda t, h: (h, 0, t))],
        out_specs=(blk, blk),
        out_shape=(shape, shape),
        compiler_params=_params(("parallel", "arbitrary")),
        name="peer_pairs",
    )(scores)


def _peer_gates_kernel(code_ref, g_ref, o_ref, i1t_ref, i2t_ref, gt_ref, *, n_keys):
    code = code_ref[...].T
    i1 = jnp.floor(code * (1.0 / n_keys))
    i1t_ref[...] = i1
    i2t_ref[...] = code - i1 * float(n_keys)
    gt_ref[...] = g_ref[...].T
    tg, n_pairs = i1t_ref.shape
    key = lax.broadcasted_iota(jnp.int32, (n_keys, n_pairs), 0).astype(F32)

    def body(n, carry):
        i1r = i1t_ref[pl.ds(n, 1), :]
        i2r = i2t_ref[pl.ds(n, 1), :]
        gr = gt_ref[pl.ds(n, 1), :]
        a_t = jnp.where(key == i1r, gr, 0.0).astype(BF16)
        b_t = jnp.where(key == i2r, 1.0, 0.0).astype(BF16)
        o_ref[n] = lax.dot_general(a_t, b_t, _NT, preferred_element_type=F32)
        return carry

    lax.fori_loop(0, tg, body, 0, unroll=32)


def _peer_gates(code, g, n_keys, tg):
    n_pairs, n = code.shape
    blk = pl.BlockSpec((n_pairs, tg), lambda t: (0, t))
    body = functools.partial(_peer_gates_kernel, n_keys=n_keys)
    return pl.pallas_call(
        body,
        grid=(n // tg,),
        in_specs=[blk, blk],
        out_specs=pl.BlockSpec((tg, n_keys, n_keys), lambda t: (t, 0, 0)),
        out_shape=jax.ShapeDtypeStruct((n, n_keys, n_keys), F32),
        scratch_shapes=[pltpu.VMEM((tg, n_pairs), F32)] * 3,
        compiler_params=_params(("parallel",)),
        name="peer_gates",
    )(code, g)


def _peer_experts_kernel(h_ref, x1_ref, gate_ref, u_ref, v_ref, o_ref, acc_ref, *, ib, n_keys):
    e_idx = pl.program_id(1)

    @pl.when(e_idx == 0)
    def _():
        acc_ref[...] = jnp.zeros_like(acc_ref)

    act = _gelu(lax.dot_general(h_ref[...], u_ref[...], _NT, preferred_element_type=F32))
    tn = h_ref.shape[0]
    del tn
    gates = [gate_ref[:, i, :] for i in range(ib)]
    w = (jnp.concatenate(gates, axis=1) * act).astype(BF16)
    acc_ref[...] += jnp.dot(w, v_ref[...], preferred_element_type=F32)

    @pl.when(e_idx == pl.num_programs(1) - 1)
    def _():
        o_ref[...] = x1_ref[...] + acc_ref[...]


def _peer_experts(h2, x1, gates, u_bf, v_bf, tn, ib):
    n, d = h2.shape
    n_keys = gates.shape[2]
    ec = ib * n_keys
    body = functools.partial(_peer_experts_kernel, ib=ib, n_keys=n_keys)
    return pl.pallas_call(
        body,
        grid=(n // tn, u_bf.shape[0] // ec),
        in_specs=[
            pl.BlockSpec((tn, d), lambda t, e: (t, 0)),
            pl.BlockSpec((tn, d), lambda t, e: (t, 0), pipeline_mode=pl.Buffered(1)),
            pl.BlockSpec((tn, ib, n_keys), lambda t, e: (t, e, 0)),
            pl.BlockSpec((ec, d), lambda t, e: (e, 0)),
            pl.BlockSpec((ec, d), lambda t, e: (e, 0)),
        ],
        out_specs=pl.BlockSpec((tn, d), lambda t, e: (t, 0)),
        out_shape=jax.ShapeDtypeStruct((n, d), F32),
        scratch_shapes=[pltpu.VMEM((tn, d), F32)],
        compiler_params=_params(("parallel", "arbitrary")),
        name="peer_experts",
    )(h2, x1, gates, u_bf, v_bf)


def _tile(n, pref):
    return pref if n % pref == 0 else n


def _layer(x, attend, shift0, s0, t_real, lam_init, lw):
    b, t, d = x.shape
    n = b * t
    xf = x.reshape(n, d)
    w_da = lw["w_q"].shape[1]
    tm = _tile(n, 512)
    h1 = _rms(xf, lw["norm1_g"], tm)
    tmp = _tile(n, 1024)
    qb = _proj("q", h1, lw["w_q"], tmp, _tile(w_da, 512), gain=lw["q_gain"], group=lw["dk"], scale=lw["dk"] ** -0.5)
    k, kb = _proj("k", h1, lw["w_k"], tmp, _tile(w_da, 512), gain=lw["k_gain"], group=lw["dk"])
    v, vb = _proj("dual", h1, lw["w_v"], tmp, _tile(w_da, 512))
    p_pad = lw["w_rw"].shape[1]
    rw = _proj("plain", h1, lw["w_rw"], tmp, p_pad // 3)

    o_da = attend(qb.reshape(b, t, w_da), kb.reshape(b, t, w_da), vb.reshape(b, t, w_da))

    rw3 = rw.reshape(b, t, p_pad)
    if t % RW_CHUNK:
        rw3 = jnp.pad(rw3, ((0, 0), (0, RW_CHUNK - t % RW_CHUNK), (0, 0)))
    o_rw, s_out = _rwkv(rw3, shift0, s0, lw["rwkv"], t_real)
    o_rw = o_rw[:, :t].reshape(n, -1)

    x1, h2, h2t = _outproj(xf, o_da.reshape(n, w_da), o_rw, lw["w_out_da"], lw["w_out_rw"], lw["norm2_g"],
                           _tile(n, 256))
    tn = _tile(n, 512)
    scores = _peer_scores(h2t, lw["wq_t"], lw["peer_keys"], tn)
    code, gate = _peer_pairs(scores, tn)
    gates = _peer_gates(code, gate, scores.shape[1], _tile(n, 128))
    out = _peer_experts(h2, x1, gates, lw["peer_u"], lw["peer_v"], tn, 8)
    return out.reshape(b, t, d), k, v, s_out, rw3[:, t - 1, :lw["rw_proj"]]


def kernel(x_prompt, x_sample, cache_k, cache_v, state_wkv, state_shift, page_table, norm1_g, w_in, da_qk_g,
           da_lambda, da_subln_g, rw_mu, rw_w0, rw_w2, rw_a0, rw_a2, rw_g2, rw_kk, rw_ka, rw_rk, rw_lnx,
           w_out, norm2_g, peer_wq, peer_keys, peer_u, peer_v):
    depth = w_in.shape[0]
    d = x_prompt.shape[-1]
    heads, dk = cache_k.shape[3], cache_k.shape[5]
    w_da = heads * 2 * dk
    rw_heads, head = state_wkv.shape[2], state_wkv.shape[3]
    wd = rw_heads * head
    r_w, r_a, r_g = rw_w2.shape[1], rw_a2.shape[1], rw_g2.shape[1]
    rw_proj = 3 * wd + r_w + r_a + r_g
    g_pad = 2 * LANES
    p_pad = 3 * wd + LANES + g_pad
    assert d == w_da + wd and head * 2 == LANES and wd % LANES == 0 and 2 * dk == LANES
    assert r_w + r_a == LANES and r_g <= g_pad and p_pad % 3 == 0 and (p_pad // 3) % LANES == 0
    assert cache_k.shape[2] == PAGE_SIZE
    nh, _, n_keys, qd = peer_keys.shape[1:]
    bsz = x_prompt.shape[0]
    dec_b, dec_t = x_sample.shape[:2]
    assert dec_t * heads <= PAGE_SIZE
    assert n_keys & (n_keys - 1) == 0 and peer_u.shape[1] == n_keys * n_keys

    xp, xs = x_prompt, x_sample
    outs = [[] for _ in range(8)]
    for l in range(depth):
        lam_init = 0.8 - 0.6 * math.exp(-0.3 * l)
        wl = w_in[l].astype(BF16)
        w_rw = jnp.pad(wl[:, 3 * w_da:], ((0, 0), (0, p_pad - rw_proj)))
        zero_wa = jnp.zeros((r_w, wd), F32)
        wwa = jnp.concatenate([jnp.concatenate([rw_w2[l], zero_wa], axis=1),
                               jnp.concatenate([zero_wa, rw_a2[l]], axis=1)], axis=0).astype(BF16)
        row = lambda z: z.reshape(1, -1)
        rwkv_prm = dict(
            mu=jnp.pad(row(rw_mu[l]), ((0, 0), (0, p_pad - rw_proj))),
            w0=row(rw_w0[l]), a0=row(rw_a0[l]), wwa=wwa,
            g2=jnp.pad(rw_g2[l], ((0, g_pad - r_g), (0, 0))).astype(BF16),
            kk=row(rw_kk[l]), ka=row(rw_ka[l]), rk=row(rw_rk[l]),
            lng=row(rw_lnx[l, 0]), lnb=row(rw_lnx[l, 1]),
            tril=jnp.tril(jnp.ones((RW_CHUNK, RW_CHUNK), BF16)),
            bd=_block_diag_ones(LANES, head), r_w=r_w, head=head)
        lw = dict(
            norm1_g=norm1_g[l], w_q=wl[:, :w_da], w_k=wl[:, w_da:2 * w_da], w_v=wl[:, 2 * w_da:3 * w_da],
            w_rw=w_rw, q_gain=jnp.tile(da_qk_g[l, 0], 2 * heads), k_gain=jnp.tile(da_qk_g[l, 1], 2 * heads),
            dk=dk, rwkv=rwkv_prm, rw_proj=rw_proj,
            w_out_da=w_out[l, :w_da].astype(BF16), w_out_rw=w_out[l, w_da:].astype(BF16),
            norm2_g=norm2_g[l], wq_t=peer_wq[l].T.astype(BF16),
            peer_keys=peer_keys[l].reshape(nh * 2, n_keys, qd).astype(BF16),
            peer_u=peer_u[l].astype(BF16), peer_v=peer_v[l].astype(BF16))

        attend_p = functools.partial(_attn_prompt, da_lambda=da_lambda[l], subln_g=da_subln_g[l], heads=heads,
                                     lam_init=lam_init, tq=_tile(x_prompt.shape[1], 512))
        n_pairs = wd // LANES
        xp, kp, vp, sp, shp = _layer(
            xp, attend_p, jnp.zeros((bsz, 1, p_pad), F32), jnp.zeros((bsz, n_pairs, LANES, LANES), F32),
            RW_CHUNK, lam_init, lw)

        attend_s = functools.partial(_attn_sample, cache_k=cache_k, cache_v=cache_v, layer=l,
                                     page_table=page_table, da_lambda=da_lambda[l], subln_g=da_subln_g[l],
                                     heads=heads, lam_init=lam_init,
                                     pp=8 if page_table.shape[1] % 8 == 0 else 4)
        shift_s = jnp.pad(state_shift[l], ((0, 0), (0, p_pad - rw_proj))).reshape(dec_b, 1, p_pad)
        xs, ksm, vsm, ssm, shs = _layer(xs, attend_s, shift_s, _pack_state(state_wkv[l], head), dec_t, lam_init, lw)

        outs[0].append(kp.reshape(bsz, -1, heads, 2, dk))
        outs[1].append(vp.reshape(bsz, -1, heads, 2 * dk))
        outs[2].append(_unpack_state(sp, head))
        outs[3].append(shp)
        outs[4].append(ksm.reshape(dec_b, dec_t, heads, 2, dk))
        outs[5].append(vsm.reshape(dec_b, dec_t, heads, 2 * dk))
        outs[6].append(_unpack_state(ssm, head))
        outs[7].append(shs)
    return (xp, xs) + tuple(jnp.stack(o) for o in outs)
```

```python
import functools
import math

import jax
import jax.numpy as jnp
from jax import lax
from jax.experimental import pallas as pl
from jax.experimental.pallas import tpu as pltpu

F32 = jnp.float32
BF16 = jnp.bfloat16

LANES = 128
RMS_EPS = 1e-6
LNX_EPS = 64e-5
PEER_TOPK = 16
PAGE_SIZE = 128
RW_CHUNK = 64
VMEM_LIMIT = 48 * 1024 * 1024

_NT = (((1,), (1,)), ((), ()))
_TN = (((0,), (0,)), ((), ()))


def _params(sem):
    return pltpu.CompilerParams(dimension_semantics=sem, vmem_limit_bytes=VMEM_LIMIT)


def _mm(a, b):
    return jnp.dot(a.astype(BF16), b.astype(BF16), preferred_element_type=F32)


def _mm_nt(a, b):
    return lax.dot_general(a.astype(BF16), b.astype(BF16), _NT, preferred_element_type=F32)


def _mm_tn(a, b):
    return lax.dot_general(a.astype(BF16), b.astype(BF16), _TN, preferred_element_type=F32)


def _mm_split(a_exact_bf16, b):
    hi = b.astype(BF16)
    lo = (b - hi.astype(F32)).astype(BF16)
    return (jnp.dot(a_exact_bf16, hi, preferred_element_type=F32)
            + jnp.dot(a_exact_bf16, lo, preferred_element_type=F32))


def _mm_split_lhs(a, b_exact_bf16):
    hi = a.astype(BF16)
    lo = (a - hi.astype(F32)).astype(BF16)
    return (jnp.dot(hi, b_exact_bf16, preferred_element_type=F32)
            + jnp.dot(lo, b_exact_bf16, preferred_element_type=F32))


def _block_diag_ones(n, group):
    i = jnp.arange(n)
    return (i[:, None] // group == i[None, :] // group).astype(BF16)


def _rms_kernel(x_ref, g_ref, o_ref):
    x = x_ref[...]
    ms = jnp.mean(x * x, axis=-1, keepdims=True)
    o_ref[...] = (x * lax.rsqrt(ms + RMS_EPS) * g_ref[...]).astype(o_ref.dtype)


def _rms(x, g, tm):
    n, d = x.shape
    return pl.pallas_call(
        _rms_kernel,
        grid=(n // tm,),
        in_specs=[pl.BlockSpec((tm, d), lambda i: (i, 0)), pl.BlockSpec((1, d), lambda i: (0, 0))],
        out_specs=pl.BlockSpec((tm, d), lambda i: (i, 0)),
        out_shape=jax.ShapeDtypeStruct((n, d), BF16),
        compiler_params=_params(("parallel",)),
        name="rms",
    )(x, g.reshape(1, d))


def _proj_plain_kernel(h_ref, w_ref, o_ref):
    o_ref[...] = jnp.dot(h_ref[...], w_ref[...], preferred_element_type=F32)


def _proj_dual_kernel(h_ref, w_ref, o_ref, ob_ref):
    y = jnp.dot(h_ref[...], w_ref[...], preferred_element_type=F32)
    o_ref[...] = y
    ob_ref[...] = y.astype(BF16)


def _qk_normed(h_ref, w_ref, bd_ref, g_ref, group):
    y = jnp.dot(h_ref[...], w_ref[...], preferred_element_type=F32)
    ms = _mm_split_lhs(y * y, bd_ref[...]) * (1.0 / group)
    return y * lax.rsqrt(ms + RMS_EPS) * g_ref[...]


def _proj_q_kernel(h_ref, w_ref, bd_ref, g_ref, ob_ref, *, group, scale):
    ob_ref[...] = (_qk_normed(h_ref, w_ref, bd_ref, g_ref, group) * scale).astype(BF16)


def _proj_k_kernel(h_ref, w_ref, bd_ref, g_ref, o_ref, ob_ref, *, group):
    yt = _qk_normed(h_ref, w_ref, bd_ref, g_ref, group).T
    o_ref[0] = yt
    ob_ref[0] = yt.astype(BF16)


def _proj_keys(h, w, gain, group, seqs, tm, tn):
    n, d = h.shape
    ncol = w.shape[1]
    s = n // seqs
    tiles = s // tm
    o_spec = pl.BlockSpec((1, tn, tm), lambda i, j: (i // tiles, j, i % tiles))
    return pl.pallas_call(
        functools.partial(_proj_k_kernel, group=group),
        grid=(n // tm, ncol // tn),
        in_specs=[pl.BlockSpec((tm, d), lambda i, j: (i, 0)), pl.BlockSpec((d, tn), lambda i, j: (0, j)),
                  pl.BlockSpec((tn, tn), lambda i, j: (0, 0)), pl.BlockSpec((1, tn), lambda i, j: (0, j))],
        out_specs=(o_spec, o_spec),
        out_shape=(jax.ShapeDtypeStruct((seqs, ncol, s), F32), jax.ShapeDtypeStruct((seqs, ncol, s), BF16)),
        compiler_params=_params(("parallel", "arbitrary")),
        name="proj_k",
    )(h, w, _block_diag_ones(tn, group), gain.reshape(1, ncol))


def _proj(kind, h, w, tm, tn, gain=None, group=None, scale=None):
    n, d = h.shape
    ncol = w.shape[1]
    grid = (n // tm, ncol // tn)
    h_spec = pl.BlockSpec((tm, d), lambda i, j: (i, 0))
    w_spec = pl.BlockSpec((d, tn), lambda i, j: (0, j))
    o_spec = pl.BlockSpec((tm, tn), lambda i, j: (i, j))
    f32_out = jax.ShapeDtypeStruct((n, ncol), F32)
    bf_out = jax.ShapeDtypeStruct((n, ncol), BF16)
    if kind == "q":
        bd = _block_diag_ones(tn, group)
        extra = [bd, gain.reshape(1, ncol)]
        extra_specs = [pl.BlockSpec((tn, tn), lambda i, j: (0, 0)), pl.BlockSpec((1, tn), lambda i, j: (0, j))]
    else:
        extra, extra_specs = [], []
    if kind == "q":
        body = functools.partial(_proj_q_kernel, group=group, scale=scale)
        out_shape, out_specs = bf_out, o_spec
    elif kind == "dual":
        body = _proj_dual_kernel
        out_shape, out_specs = (f32_out, bf_out), (o_spec, o_spec)
    else:
        body = _proj_plain_kernel
        out_shape, out_specs = f32_out, o_spec
    return pl.pallas_call(
        body,
        grid=grid,
        in_specs=[h_spec, w_spec] + extra_specs,
        out_specs=out_specs,
        out_shape=out_shape,
        compiler_params=_params(("parallel", "arbitrary")),
        name="proj_" + kind,
    )(h, w, *extra)


def _lambda(lam_ref, lam_init):
    ll = lam_ref[...]
    s01 = jnp.sum(ll[0:1, :] * ll[1:2, :], axis=-1, keepdims=True)
    s23 = jnp.sum(ll[2:3, :] * ll[3:4, :], axis=-1, keepdims=True)
    return jnp.exp(s01) - jnp.exp(s23) + lam_init


def _subln(o, g, lam_init):
    ms = jnp.mean(o * o, axis=-1, keepdims=True)
    return (o * lax.rsqrt(ms + RMS_EPS) * g) * (1.0 - lam_init)


def _attn_prompt_kernel(q_ref, k_ref, v_ref, lam_ref, g_ref, o_ref, acc0_ref, acc1_ref, *, tq, dk, lam_init):
    qi = pl.program_id(2)
    q = q_ref[0]
    lane = lax.broadcasted_iota(jnp.int32, q.shape, 1)
    qf = q.astype(F32)
    qc = (jnp.where(lane < dk, qf, 0.0).astype(BF16), jnp.where(lane >= dk, qf, 0.0).astype(BF16))
    accs = (acc0_ref, acc1_ref)
    acc0_ref[...] = jnp.zeros_like(acc0_ref)
    acc1_ref[...] = jnp.zeros_like(acc1_ref)
    causal = (lax.broadcasted_iota(jnp.int32, (tq, tq), 1) <= lax.broadcasted_iota(jnp.int32, (tq, tq), 0))

    def step(j, carry, diagonal):
        kt = k_ref[0, :, pl.ds(pl.multiple_of(j * tq, tq), tq)]
        vb = v_ref[0, pl.ds(pl.multiple_of(j * tq, tq), tq), :]
        new = []
        for c in range(2):
            m, l = carry[2 * c], carry[2 * c + 1]
            s = jnp.dot(qc[c], kt, preferred_element_type=F32)
            if diagonal:
                s = jnp.where(causal, s, -jnp.inf)
            m_new = jnp.maximum(m, jnp.max(s, axis=-1, keepdims=True))
            corr = jnp.exp(m - m_new)
            p = jnp.exp(s - m_new)
            l = l * corr + jnp.sum(p, axis=-1, keepdims=True)
            accs[c][...] = accs[c][...] * corr + jnp.dot(p.astype(BF16), vb, preferred_element_type=F32)
            new += [m_new, l]
        return tuple(new)

    minit = jnp.full((tq, 1), -jnp.inf, F32)
    linit = jnp.zeros((tq, 1), F32)
    carry = lax.fori_loop(0, qi, functools.partial(step, diagonal=False), (minit, linit, minit, linit))
    _, l0, _, l1 = step(qi, carry, True)
    lam = _lambda(lam_ref, lam_init)
    o = acc0_ref[...] / l0 - lam * (acc1_ref[...] / l1)
    o_ref[0] = _subln(o, g_ref[...], lam_init).astype(o_ref.dtype)


def _attn_prompt(qb, kbt, vb, da_lambda, subln_g, heads, lam_init, tq):
    b, s, w = qb.shape
    dv = w // heads
    body = functools.partial(_attn_prompt_kernel, tq=tq, dk=dv // 2, lam_init=lam_init)
    return pl.pallas_call(
        body,
        grid=(b, heads, s // tq),
        in_specs=[
            pl.BlockSpec((1, tq, dv), lambda bi, h, qi: (bi, qi, h)),
            pl.BlockSpec((1, dv, s), lambda bi, h, qi: (bi, h, 0)),
            pl.BlockSpec((1, s, dv), lambda bi, h, qi: (bi, 0, h)),
            pl.BlockSpec(da_lambda.shape, lambda bi, h, qi: (0, 0)),
            pl.BlockSpec((1, dv), lambda bi, h, qi: (0, 0)),
        ],
        out_specs=pl.BlockSpec((1, tq, dv), lambda bi, h, qi: (bi, qi, h)),
        out_shape=jax.ShapeDtypeStruct((b, s, w), BF16),
        scratch_shapes=[pltpu.VMEM((tq, dv), F32), pltpu.VMEM((tq, dv), F32)],
        compiler_params=_params(("parallel", "parallel", "arbitrary")),
        name="attn_prompt",
    )(qb, kbt, vb, da_lambda, subln_g.reshape(1, dv))


def _attn_sample_kernel(pt_ref, qrep_ref, kst_ref, vs_ref, lam_ref, g_ref, *rest, pp, heads, t, dk, lam_init):
    kc_refs = rest[:pp]
    vc_refs = rest[pp:2 * pp]
    o_ref = rest[2 * pp]
    qbd_ref, m_ref, l_ref, acc_ref = rest[2 * pp + 1:]
    del pt_ref
    p_idx = pl.program_id(1)
    rows = 2 * t * heads
    dv = 2 * dk
    w = heads * dv

    @pl.when(p_idx == 0)
    def _():
        r = lax.broadcasted_iota(jnp.int32, (rows, w), 0)
        col = lax.broadcasted_iota(jnp.int32, (rows, w), 1)
        keep = (col // dk) == ((r % heads) * 2 + r // (t * heads))
        qbd_ref[...] = jnp.where(keep, qrep_ref[0].astype(F32), 0.0).astype(BF16)
        m_ref[...] = jnp.full(m_ref.shape, -jnp.inf, F32)
        l_ref[...] = jnp.zeros(l_ref.shape, F32)
        acc_ref[...] = jnp.zeros(acc_ref.shape, F32)

    qbd = qbd_ref[...]

    def online(s, vbs):
        m = m_ref[...]
        m_new = jnp.maximum(m, jnp.max(s, axis=-1, keepdims=True))
        corr = jnp.exp(m - m_new)
        p = jnp.exp(s - m_new)
        l_ref[...] = l_ref[...] * corr + jnp.sum(p, axis=-1, keepdims=True)
        pv = None
        for i, vb in enumerate(vbs):
            n = vb.shape[0]
            d = jnp.dot(p[:, i * n:(i + 1) * n].astype(BF16), vb, preferred_element_type=F32)
            pv = d if pv is None else pv + d
        acc_ref[...] = acc_ref[...] * corr + pv
        m_ref[...] = m_new

    kt = jnp.concatenate([kc[0] for kc in kc_refs], axis=1).astype(BF16)
    vwide = [jnp.concatenate([vc[0, pl.ds(h, PAGE_SIZE, stride=heads), :] for h in range(heads)], axis=1)
             .astype(BF16) for vc in vc_refs]
    online(jnp.dot(qbd, kt, preferred_element_type=F32), vwide)

    @pl.when(p_idx == pl.num_programs(1) - 1)
    def _():
        s = jnp.dot(qbd, kst_ref[0], preferred_element_type=F32)
        r = lax.broadcasted_iota(jnp.int32, s.shape, 0)
        col = lax.broadcasted_iota(jnp.int32, s.shape, 1)
        online(jnp.where(col <= (r // heads) % t, s, -jnp.inf), [vs_ref[0]])
        out = acc_ref[...] / l_ref[...]
        hrow = lax.broadcasted_iota(jnp.int32, (heads, w), 0)
        hcol = lax.broadcasted_iota(jnp.int32, (heads, w), 1) // dv
        pick = hrow == hcol
        lam = _lambda(lam_ref, lam_init)
        toks = []
        for tok_i in range(t):
            maps = []
            for c in range(2):
                base = (c * t + tok_i) * heads
                maps.append(jnp.sum(jnp.where(pick, out[base:base + heads, :], 0.0), axis=0, keepdims=True))
            toks.append(maps[0] - lam * maps[1])
        o = jnp.concatenate(toks, axis=0)
        g = g_ref[...]
        outs = [_subln(o[:, h * dv:(h + 1) * dv], g, lam_init) for h in range(heads)]
        o_ref[0] = jnp.concatenate(outs, axis=1).astype(o_ref.dtype)


def _attn_sample(qb, kbt, vb, cache_k, cache_v, layer, page_table, da_lambda, subln_g, heads, lam_init, pp):
    bd, t, w = qb.shape
    kb = jnp.transpose(kbt.reshape(w, bd, t), (1, 2, 0))
    dv = w // heads
    n_pool = cache_k.shape[1]
    n_pages = page_table.shape[1]
    rows = 2 * t * heads
    qrep = jnp.tile(jnp.repeat(qb, heads, axis=1), (1, 2, 1))
    kst_pad = jnp.pad(jnp.swapaxes(kb, 1, 2), ((0, 0), (0, 0), (0, PAGE_SIZE - t)))
    vs_pad = jnp.pad(vb, ((0, 0), (0, PAGE_SIZE - t), (0, 0)))
    ckt = jnp.transpose(cache_k, (0, 1, 3, 4, 5, 2)).reshape(-1, w, PAGE_SIZE)
    cv = cache_v.reshape(-1, PAGE_SIZE * heads, dv)
    base = layer * n_pool

    def k_spec(i):
        return pl.BlockSpec((1, w, PAGE_SIZE), lambda b, p, pt: (base + pt[b, p * pp + i], 0, 0))

    def v_spec(i):
        return pl.BlockSpec((1, PAGE_SIZE * heads, dv), lambda b, p, pt: (base + pt[b, p * pp + i], 0, 0))

    body = functools.partial(_attn_sample_kernel, pp=pp, heads=heads, t=t, dk=dv // 2, lam_init=lam_init)
    grid_spec = pltpu.PrefetchScalarGridSpec(
        num_scalar_prefetch=1,
        grid=(bd, n_pages // pp),
        in_specs=[
            pl.BlockSpec((1, rows, w), lambda b, p, pt: (b, 0, 0)),
            pl.BlockSpec((1, w, PAGE_SIZE), lambda b, p, pt: (b, 0, 0)),
            pl.BlockSpec((1, PAGE_SIZE, w), lambda b, p, pt: (b, 0, 0)),
            pl.BlockSpec(da_lambda.shape, lambda b, p, pt: (0, 0)),
            pl.BlockSpec((1, dv), lambda b, p, pt: (0, 0)),
        ] + [k_spec(i) for i in range(pp)] + [v_spec(i) for i in range(pp)],
        out_specs=pl.BlockSpec((1, t, w), lambda b, p, pt: (b, 0, 0)),
        scratch_shapes=[pltpu.VMEM((rows, w), BF16), pltpu.VMEM((rows, 1), F32),
                        pltpu.VMEM((rows, 1), F32), pltpu.VMEM((rows, w), F32)],
    )
    return pl.pallas_call(
        body,
        grid_spec=grid_spec,
        out_shape=jax.ShapeDtypeStruct((bd, t, w), BF16),
        compiler_params=_params(("parallel", "arbitrary")),
        name="attn_sample",
    )(page_table, qrep, kst_pad, vs_pad, da_lambda, subln_g.reshape(1, dv), *([ckt] * pp), *([cv] * pp))


def _softplus(z):
    return jnp.maximum(z, 0.0) + jnp.log1p(jnp.exp(-jnp.abs(z)))


def _sigmoid(z):
    return 1.0 / (1.0 + jnp.exp(-z))


def _split_heads(x, first):
    zero = jnp.zeros_like(x)
    return jnp.concatenate([jnp.where(first, x, zero), jnp.where(first, zero, x)], axis=0)


def _rwkv_kernel(rw_ref, shift_ref, s0_ref, mu_ref, w0_ref, a0_ref, wwa_ref, g2_ref, kk_ref, ka_ref,
                 rk_ref, lng_ref, lnb_ref, tril_ref, bd_ref, o_ref, sout_ref, state_ref, prev_ref,
                 *, wd, r_w, t_real, head):
    c_idx = pl.program_id(1)
    cl = RW_CHUNK
    pairs = range(wd // LANES)

    @pl.when(c_idx == 0)
    def _():
        state_ref[...] = s0_ref[0]
        prev_ref[...] = shift_ref[0]

    x = rw_ref[0]
    rowp = lax.broadcasted_iota(jnp.int32, x.shape, 0)
    prevx = jnp.where(rowp == 0, prev_ref[...], pltpu.roll(x, 1, axis=0))
    prev_ref[...] = x[cl - 1:cl, :]
    xm = x + (prevx - x) * mu_ref[...]
    r = xm[:, 0:wd]
    kr = xm[:, wd:2 * wd]
    vr = xm[:, 2 * wd:3 * wd]
    slab = xm[:, 3 * wd:3 * wd + LANES]
    lane_s = lax.broadcasted_iota(jnp.int32, slab.shape, 1)
    wa = _mm(jnp.where(lane_s < r_w, jnp.tanh(slab), slab), wwa_ref[...])
    gd = xm[:, 3 * wd + LANES:]
    g = _mm(_sigmoid(gd), g2_ref[...])
    w = -_softplus(-(w0_ref[...] + wa[:, :wd])) - 0.5
    lw = -jnp.exp(w)
    a = _sigmoid(a0_ref[...] + wa[:, wd:])
    kk = kr * kk_ref[...]
    kh = kr * (1.0 + (a - 1.0) * ka_ref[...])
    beta_scale = a
    if t_real < cl:
        valid = lax.broadcasted_iota(jnp.int32, (cl, wd), 0) < t_real
        lw = jnp.where(valid, lw, 0.0)
        beta_scale = jnp.where(valid, a, 0.0)
        kh_state = jnp.where(valid, kh, 0.0)
    else:
        kh_state = kh
    cum = _mm_split(tril_ref[...], lw)
    cum_last = cum[cl - 1:cl, :]
    e_neg = jnp.exp(-cum)
    e_end = jnp.exp(cum_last - cum)
    w_end = jnp.exp(cum_last)
    rt_all = r * jnp.exp(cum)
    ex_all = jnp.exp(cum - lw)
    bonus_in = r * kh * rk_ref[...]

    lane = lax.broadcasted_iota(jnp.int32, (cl, LANES), 1)
    first = lane < head
    tt = lax.broadcasted_iota(jnp.int32, (cl, 2 * cl), 0)
    ss = lax.broadcasted_iota(jnp.int32, (cl, 2 * cl), 1) % cl
    strict = ss < tt
    incl = ss <= tt
    firstw = lax.broadcasted_iota(jnp.int32, (cl, 2 * cl), 1) < cl
    r2 = lax.broadcasted_iota(jnp.int32, (2 * cl, 2 * cl), 0)
    c2 = lax.broadcasted_iota(jnp.int32, (2 * cl, 2 * cl), 1)
    eye = (r2 == c2).astype(F32)
    rs = lax.broadcasted_iota(jnp.int32, (LANES, LANES), 0) // head
    cs = lax.broadcasted_iota(jnp.int32, (LANES, LANES), 1) // head
    same_head = rs == cs
    bd = bd_ref[...]
    n_levels = int(math.log2(cl))
    sls = [slice(p * LANES, (p + 1) * LANES) for p in pairs]

    kk_p = [kk[:, sl] for sl in sls]
    nrm = [jnp.sqrt(_mm(k * k, bd)) for k in kk_p]
    kkn = [k / jnp.maximum(n, 1e-12) for k, n in zip(kk_p, nrm)]
    beta = [kkn[p] * beta_scale[:, sls[p]] for p in pairs]
    v_p = [vr[:, sl] for sl in sls]
    st = [state_ref[p] for p in pairs]
    lhs = [jnp.concatenate([-ex_all[:, sls[p]] * kkn[p], rt_all[:, sls[p]]], axis=0) for p in pairs]
    rhs = [jnp.concatenate([_split_heads(beta[p] * e_neg[:, sls[p]], first),
                            _split_heads(kh_state[:, sls[p]] * e_neg[:, sls[p]], first)], axis=0) for p in pairs]
    m = [_mm_nt(lhs[p], rhs[p]) for p in pairs]
    sa = [_mm_nt(lhs[p], st[p]) for p in pairs]
    aab = [jnp.where(strict, x_[:cl, :2 * cl], 0.0) for x_ in m]
    aak = [jnp.where(strict, x_[:cl, 2 * cl:], 0.0) for x_ in m]
    arbk = [jnp.concatenate([jnp.where(incl, x_[cl:, :2 * cl], 0.0), jnp.where(incl, x_[cl:, 2 * cl:], 0.0)], axis=1)
            for x_ in m]
    abd = [_split_heads(x_, firstw) for x_ in aab]
    vbd = [_split_heads(x_, first) for x_ in v_p]
    rhs_u = [sa[p][:cl, :] + _mm(aak[p], vbd[p]) for p in pairs]
    tinv = [eye + x_ for x_ in abd]
    xp = [_mm(x_, x_) for x_ in abd]
    for lvl in range(1, n_levels):
        if lvl < n_levels - 1:
            both = [_mm(xp[p], jnp.concatenate([tinv[p], xp[p]], axis=1)) for p in pairs]
            tinv = [tinv[p] + both[p][:, :2 * cl] for p in pairs]
            xp = [both[p][:, 2 * cl:] for p in pairs]
        else:
            tinv = [tinv[p] + _mm(xp[p], tinv[p]) for p in pairs]
    tw = [x_[:cl, :] + x_[cl:, :] for x_ in tinv]
    u = [_mm(tw[p], _split_heads(rhs_u[p], first)) for p in pairs]
    y = [sa[p][cl:, :] + _mm(arbk[p], jnp.concatenate([_split_heads(u[p], first), vbd[p]], axis=0)) for p in pairs]
    ds = [_mm_tn(jnp.concatenate([u[p], v_p[p]], axis=0),
                 jnp.concatenate([beta[p] * e_end[:, sls[p]], kh_state[:, sls[p]] * e_end[:, sls[p]]], axis=0))
          for p in pairs]
    mean = [_mm(x_, bd) * (1.0 / head) for x_ in y]
    dlt = [y[p] - mean[p] for p in pairs]
    var = [_mm(x_ * x_, bd) * (1.0 / head) for x_ in dlt]
    bonus = [_mm(bonus_in[:, sls[p]], bd) * v_p[p] for p in pairs]
    for p in pairs:
        sl = sls[p]
        state_ref[p] = st[p] * w_end[:, sl] + jnp.where(same_head, ds[p], 0.0)
        yn = dlt[p] * lax.rsqrt(var[p] + LNX_EPS) * lng_ref[:, sl] + lnb_ref[:, sl]
        o_ref[0, :, sl] = ((yn + bonus[p]) * g[:, sl]).astype(o_ref.dtype)

    @pl.when(c_idx == pl.num_programs(1) - 1)
    def _():
        sout_ref[0] = state_ref[...]


def _rwkv(rw, shift0, s0, prm, t_real):
    b, s, p = rw.shape
    wd = prm["w0"].shape[1]
    n_pairs = wd // LANES
    cl = RW_CHUNK
    body = functools.partial(_rwkv_kernel, wd=wd, r_w=prm["r_w"], t_real=t_real, head=prm["head"])

    def const(shape):
        return pl.BlockSpec(shape, lambda bi, ci: (0,) * len(shape))

    names = ["mu", "w0", "a0", "wwa", "g2", "kk", "ka", "rk", "lng", "lnb", "tril", "bd"]
    return pl.pallas_call(
        body,
        grid=(b, s // cl),
        in_specs=[
            pl.BlockSpec((1, cl, p), lambda bi, ci: (bi, ci, 0)),
            pl.BlockSpec((1, 1, p), lambda bi, ci: (bi, 0, 0)),
            pl.BlockSpec((1, n_pairs, LANES, LANES), lambda bi, ci: (bi, 0, 0, 0)),
        ] + [const(prm[nm].shape) for nm in names],
        out_specs=(pl.BlockSpec((1, cl, wd), lambda bi, ci: (bi, ci, 0)),
                   pl.BlockSpec((1, n_pairs, LANES, LANES), lambda bi, ci: (bi, 0, 0, 0))),
        out_shape=(jax.ShapeDtypeStruct((b, s, wd), BF16),
                   jax.ShapeDtypeStruct((b, n_pairs, LANES, LANES), F32)),
        scratch_shapes=[pltpu.VMEM((n_pairs, LANES, LANES), F32), pltpu.VMEM((1, p), F32)],
        compiler_params=_params(("parallel", "arbitrary")),
        name="rwkv",
    )(rw, shift0, s0, *[prm[nm] for nm in names])


def _pack_state(wkv, head):
    b, h = wkv.shape[:2]
    pairs = wkv.reshape(b, h // 2, 2, head, head)
    z = jnp.zeros((b, h // 2, head, head), wkv.dtype)
    top = jnp.concatenate([pairs[:, :, 0], z], axis=-1)
    bot = jnp.concatenate([z, pairs[:, :, 1]], axis=-1)
    return jnp.concatenate([top, bot], axis=-2)


def _unpack_state(sbd, head):
    b, n_pairs = sbd.shape[:2]
    h0 = sbd[:, :, :head, :head]
    h1 = sbd[:, :, head:, head:]
    return jnp.stack([h0, h1], axis=2).reshape(b, 2 * n_pairs, head, head)


def _outproj_kernel(x_ref, oda_ref, orw_ref, w1_ref, w2_ref, g_ref, x1_ref, h_ref, ht_ref):
    y = (jnp.dot(oda_ref[...], w1_ref[...], preferred_element_type=F32)
         + jnp.dot(orw_ref[...], w2_ref[...], preferred_element_type=F32))
    x1 = x_ref[...] + y
    x1_ref[...] = x1
    ms = jnp.mean(x1 * x1, axis=-1, keepdims=True)
    h2 = x1 * lax.rsqrt(ms + RMS_EPS) * g_ref[...]
    h_ref[...] = h2.astype(BF16)
    ht_ref[...] = h2.T.astype(BF16)


def _outproj(x, oda, orw, w1, w2, g, tm):
    n, d = x.shape
    wa, wr = oda.shape[1], orw.shape[1]
    return pl.pallas_call(
        _outproj_kernel,
        grid=(n // tm,),
        in_specs=[
            pl.BlockSpec((tm, d), lambda i: (i, 0)),
            pl.BlockSpec((tm, wa), lambda i: (i, 0)),
            pl.BlockSpec((tm, wr), lambda i: (i, 0)),
            pl.BlockSpec((wa, d), lambda i: (0, 0)),
            pl.BlockSpec((wr, d), lambda i: (0, 0)),
            pl.BlockSpec((1, d), lambda i: (0, 0)),
        ],
        out_specs=(pl.BlockSpec((tm, d), lambda i: (i, 0)), pl.BlockSpec((tm, d), lambda i: (i, 0)),
                   pl.BlockSpec((d, tm), lambda i: (0, i))),
        out_shape=(jax.ShapeDtypeStruct((n, d), F32), jax.ShapeDtypeStruct((n, d), BF16),
                   jax.ShapeDtypeStruct((d, n), BF16)),
        compiler_params=_params(("parallel",)),
        name="outproj",
    )(x, oda, orw, w1, w2, g.reshape(1, d))


def _peer_scores_kernel(ht_ref, wqt_ref, keys_ref, s_ref, q_ref):
    q_ref[...] = jnp.dot(wqt_ref[...], ht_ref[...], preferred_element_type=F32).astype(BF16)
    nhc, _, qd = keys_ref.shape
    for j in range(nhc):
        s_ref[j] = jnp.dot(keys_ref[j], q_ref[j * qd:(j + 1) * qd, :], preferred_element_type=F32)


def _peer_scores(ht, wqt, keys, tn):
    d, n = ht.shape
    nhc, n_keys, qd = keys.shape
    once = pl.Buffered(1)
    return pl.pallas_call(
        _peer_scores_kernel,
        grid=(n // tn,),
        in_specs=[
            pl.BlockSpec((d, tn), lambda t: (0, t)),
            pl.BlockSpec((nhc * qd, d), lambda t: (0, 0), pipeline_mode=once),
            pl.BlockSpec((nhc, n_keys, qd), lambda t: (0, 0, 0), pipeline_mode=once),
        ],
        out_specs=pl.BlockSpec((nhc, n_keys, tn), lambda t: (0, 0, t)),
        out_shape=jax.ShapeDtypeStruct((nhc, n_keys, n), F32),
        scratch_shapes=[pltpu.VMEM((nhc * qd, tn), BF16)],
        compiler_params=_params(("parallel",)),
        name="peer_scores",
    )(ht, wqt, keys)


def _top_desc(s, k):
    cur = jnp.max(s, axis=0, keepdims=True)
    out = [cur]
    for _ in range(k - 1):
        cur = jnp.max(jnp.where(s < cur, s, -jnp.inf), axis=0, keepdims=True)
        out.append(cur)
    return out


def _peer_route_kernel(s_ref, e1_ref, e2_ref, tau_ref):
    s1 = s_ref[0]
    s2 = s_ref[1]
    a = _top_desc(s1, PEER_TOPK)
    b = jnp.concatenate(_top_desc(s2, PEER_TOPK), axis=0)
    cand = jnp.concatenate([ai + b[:PEER_TOPK // (i + 1), :] for i, ai in enumerate(a)], axis=0)
    best = _top_desc(cand, PEER_TOPK)
    tau = best[-1]
    z = sum(jnp.exp(c - best[0]) for c in best)
    e1_ref[0] = jnp.exp(s1 - a[0]) / z
    e2_ref[0] = jnp.exp(s2 - b[0:1, :])
    tau_ref[0] = tau


def _peer_route(scores, tn):
    nhc, n_keys, n = scores.shape
    nh = nhc // 2
    blk = pl.BlockSpec((1, n_keys, tn), lambda t, h: (h, 0, t))
    return pl.pallas_call(
        _peer_route_kernel,
        grid=(n // tn, nh),
        in_specs=[pl.BlockSpec((2, n_keys, tn), lambda t, h: (h, 0, t))],
        out_specs=(blk, blk, pl.BlockSpec((1, 1, tn), lambda t, h: (h, 0, t))),
        out_shape=(jax.ShapeDtypeStruct((nh, n_keys, n), F32), jax.ShapeDtypeStruct((nh, n_keys, n), F32),
                   jax.ShapeDtypeStruct((nh, 1, n), F32)),
        compiler_params=_params(("parallel", "arbitrary")),
        name="peer_route",
    )(scores)


def _gelu(x):
    return 0.5 * x * (1.0 + lax.erf(x * (2.0 ** -0.5)))


def _peer_dense_kernel(ht_ref, x1_ref, s_ref, e1_ref, e2_ref, tau_ref, u_ref, v_ref, o_ref,
                       acc_ref, act_ref, wt_ref, wtt_ref, *, ib, nh, n_keys, lane_chunk):
    e_idx = pl.program_id(1)
    n_blocks = pl.num_programs(1) - 1
    tn = ht_ref.shape[1]

    @pl.when(e_idx == 0)
    def _():
        acc_ref[...] = jnp.zeros_like(acc_ref)
        wtt_ref[1] = jnp.zeros(wtt_ref.shape[1:], wtt_ref.dtype)

    blk = jnp.minimum(e_idx, n_blocks - 1)
    d = v_ref.shape[1]
    dc = d // ib
    prev_slot = (e_idx + 1) % 2
    slot = e_idx % 2

    def first_matmul(i):
        rs = slice(i * n_keys, (i + 1) * n_keys)
        act_ref[rs, :] = _gelu(jnp.dot(u_ref[rs, :], ht_ref[...], preferred_element_type=F32))

    def gate_tile(i, ch):
        i1 = blk * ib + i
        rs = slice(i * n_keys, (i + 1) * n_keys)
        cs = slice(ch * lane_chunk, (ch + 1) * lane_chunk)
        gsum = None
        for h in range(nh):
            s1row = s_ref[2 * h, pl.ds(i1, 1), cs]
            e1row = e1_ref[h, pl.ds(i1, 1), cs]
            keep = (s1row + s_ref[2 * h + 1, :, cs]) >= tau_ref[h, :, cs]
            term = jnp.where(keep, e1row * e2_ref[h, :, cs], 0.0)
            gsum = term if gsum is None else gsum + term
        wt_ref[rs, cs] = gsum * act_ref[rs, cs]

    def second_matmul(i):
        cols = slice(i * dc, (i + 1) * dc)
        acc_ref[:, cols] += jnp.dot(wtt_ref[prev_slot], v_ref[:, cols], preferred_element_type=F32)

    n_ch = tn // lane_chunk
    first_matmul(0)
    for i in range(ib):
        if i + 1 < ib:
            first_matmul(i + 1)
        for ch in range(n_ch):
            gate_tile(i, ch)
            if ch == 0:
                second_matmul(i)
        rs = slice(i * n_keys, (i + 1) * n_keys)
        wtt_ref[slot, :, rs] = wt_ref[rs, :].T.astype(BF16)

    @pl.when(e_idx == n_blocks)
    def _():
        o_ref[...] = x1_ref[...] + acc_ref[...]


def _peer_dense(ht, x1, scores, e1, e2, tau, u_bf, v_bf, tn, ib):
    d, n = ht.shape
    nh, n_keys, _ = e1.shape
    n_exp = u_bf.shape[0]
    ec = ib * n_keys
    n_blocks = n_exp // ec
    once = pl.Buffered(1)
    body = functools.partial(_peer_dense_kernel, ib=ib, nh=nh, n_keys=n_keys, lane_chunk=min(tn, 2 * LANES))
    return pl.pallas_call(
        body,
        grid=(n // tn, n_blocks + 1),
        in_specs=[
            pl.BlockSpec((d, tn), lambda t, e: (0, t)),
            pl.BlockSpec((tn, d), lambda t, e: (t, 0), pipeline_mode=once),
            pl.BlockSpec((2 * nh, n_keys, tn), lambda t, e: (0, 0, t), pipeline_mode=once),
            pl.BlockSpec((nh, n_keys, tn), lambda t, e: (0, 0, t), pipeline_mode=once),
            pl.BlockSpec((nh, n_keys, tn), lambda t, e: (0, 0, t), pipeline_mode=once),
            pl.BlockSpec((nh, 1, tn), lambda t, e: (0, 0, t)),
            pl.BlockSpec((ec, d), lambda t, e: (jnp.minimum(e, n_blocks - 1), 0)),
            pl.BlockSpec((ec, d), lambda t, e: (jnp.maximum(e - 1, 0), 0)),
        ],
        out_specs=pl.BlockSpec((tn, d), lambda t, e: (t, 0)),
        out_shape=jax.ShapeDtypeStruct((n, d), F32),
        scratch_shapes=[pltpu.VMEM((tn, d), F32), pltpu.VMEM((ec, tn), F32), pltpu.VMEM((ec, tn), F32),
                        pltpu.VMEM((2, tn, ec), BF16)],
        compiler_params=_params(("parallel", "arbitrary")),
        name="peer_dense",
    )(ht, x1, scores, e1, e2, tau, u_bf, v_bf)


def _take_top(vals, k, payload=None):
    n_rows = vals.shape[0]
    rowid = lax.broadcasted_iota(jnp.int32, vals.shape, 0).astype(F32)
    cur = vals
    top, picked = [], []
    for _ in range(k):
        m = jnp.max(cur, axis=0, keepdims=True)
        sel = jnp.min(jnp.where(cur == m, rowid, float(n_rows)), axis=0, keepdims=True)
        hit = rowid == sel
        top.append(m)
        picked.append(sel if payload is None else jnp.sum(jnp.where(hit, payload, 0.0), axis=0, keepdims=True))
        cur = jnp.where(hit, -jnp.inf, cur)
    return top, picked


def _peer_pairs_kernel(s_ref, code_ref, g_ref, *, n_keys):
    s1 = s_ref[0]
    s2 = s_ref[1]
    a, ia = _take_top(s1, PEER_TOPK)
    b, ib = _take_top(s2, PEER_TOPK)
    b = jnp.concatenate(b, axis=0)
    ib = jnp.concatenate(ib, axis=0)
    counts = [PEER_TOPK // (i + 1) for i in range(PEER_TOPK)]
    cand = jnp.concatenate([a[i] + b[:c, :] for i, c in enumerate(counts)], axis=0)
    code = jnp.concatenate([ia[i] * float(n_keys) + ib[:c, :] for i, c in enumerate(counts)], axis=0)
    best, ids = _take_top(cand, PEER_TOPK, payload=code)
    ex = [jnp.exp(c - best[0]) for c in best]
    z = sum(ex)
    code_ref[...] = jnp.concatenate(ids, axis=0)
    g_ref[...] = jnp.concatenate(ex, axis=0) / z


def _peer_pairs(scores, tn):
    nhc, n_keys, n = scores.shape
    nh = nhc // 2
    blk = pl.BlockSpec((PEER_TOPK, tn), lambda t, h: (h, t))
    shape = jax.ShapeDtypeStruct((nh * PEER_TOPK, n), F32)
    return pl.pallas_call(
        functools.partial(_peer_pairs_kernel, n_keys=n_keys),
        grid=(n // tn, nh),
        in_specs=[pl.BlockSpec((2, n_keys, tn), lambda t, h: (h, 0, t))],
        out_specs=(blk, blk),
        out_shape=(shape, shape),
        compiler_params=_params(("parallel", "arbitrary")),
        name="peer_pairs",
    )(scores)


def _peer_gates_kernel(code_ref, g_ref, o_ref, i1t_ref, i2t_ref, gt_ref, *, n_keys, ib):
    code = code_ref[...].T
    i1 = jnp.floor(code * (1.0 / n_keys))
    i1t_ref[...] = i1
    i2t_ref[...] = code - i1 * float(n_keys)
    gt_ref[...] = g_ref[...].T
    tg, n_pairs = i1t_ref.shape
    key = lax.broadcasted_iota(jnp.int32, (n_keys, n_pairs), 0).astype(F32)

    def body(n, carry):
        i1r = i1t_ref[pl.ds(n, 1), :]
        i2r = i2t_ref[pl.ds(n, 1), :]
        gr = gt_ref[pl.ds(n, 1), :]
        a_t = jnp.where(key == i1r, gr, 0.0).astype(BF16)
        b_t = jnp.where(key == i2r, 1.0, 0.0).astype(BF16)
        grid_n = lax.dot_general(a_t, b_t, _NT, preferred_element_type=F32)
        for blk in range(o_ref.shape[0]):
            o_ref[blk, n] = grid_n[blk * ib:(blk + 1) * ib, :]
        return carry

    lax.fori_loop(0, tg, body, 0, unroll=32)


def _peer_gates(code, g, n_keys, tg, ib):
    n_pairs, n = code.shape
    blk = pl.BlockSpec((n_pairs, tg), lambda t: (0, t))
    body = functools.partial(_peer_gates_kernel, n_keys=n_keys, ib=ib)
    return pl.pallas_call(
        body,
        grid=(n // tg,),
        in_specs=[blk, blk],
        out_specs=pl.BlockSpec((n_keys // ib, tg, ib, n_keys), lambda t: (0, t, 0, 0)),
        out_shape=jax.ShapeDtypeStruct((n_keys // ib, n, ib, n_keys), F32),
        scratch_shapes=[pltpu.VMEM((tg, n_pairs), F32)] * 3,
        compiler_params=_params(("parallel",)),
        name="peer_gates",
    )(code, g)


def _peer_experts_kernel(h_ref, x1_ref, gate_ref, u_ref, v_ref, o_ref, acc_ref, *, ib, n_keys):
    e_idx = pl.program_id(1)

    @pl.when(e_idx == 0)
    def _():
        acc_ref[...] = jnp.zeros_like(acc_ref)

    parts = 2 if ib % 2 == 0 else 1
    isub = ib // parts
    rows = [slice(p * isub * n_keys, (p + 1) * isub * n_keys) for p in range(parts)]
    pre = [lax.dot_general(h_ref[...], u_ref[r, :], _NT, preferred_element_type=F32) for r in rows]
    for p in range(parts):
        gates = jnp.concatenate([gate_ref[0, :, p * isub + i, :] for i in range(isub)], axis=1)
        w = (gates * _gelu(pre[p])).astype(BF16)
        acc_ref[...] += jnp.dot(w, v_ref[rows[p], :], preferred_element_type=F32)

    @pl.when(e_idx == pl.num_programs(1) - 1)
    def _():
        o_ref[...] = x1_ref[...] + acc_ref[...]


def _peer_experts(h2, x1, gates, u_bf, v_bf, tn, ib):
    n, d = h2.shape
    n_keys = gates.shape[3]
    ec = ib * n_keys
    body = functools.partial(_peer_experts_kernel, ib=ib, n_keys=n_keys)
    return pl.pallas_call(
        body,
        grid=(n // tn, v_bf.shape[0] // ec),
        in_specs=[
            pl.BlockSpec((tn, d), lambda t, e: (t, 0)),
            pl.BlockSpec((tn, d), lambda t, e: (t, 0), pipeline_mode=pl.Buffered(1)),
            pl.BlockSpec((1, tn, ib, n_keys), lambda t, e: (e, t, 0, 0)),
            pl.BlockSpec((ec, d), lambda t, e: (e, 0)),
            pl.BlockSpec((ec, d), lambda t, e: (e, 0)),
        ],
        out_specs=pl.BlockSpec((tn, d), lambda t, e: (t, 0)),
        out_shape=jax.ShapeDtypeStruct((n, d), F32),
        scratch_shapes=[pltpu.VMEM((tn, d), F32)],
        compiler_params=_params(("parallel", "arbitrary")),
        name="peer_experts",
    )(h2, x1, gates, u_bf, v_bf)


def _tile(n, pref):
    return pref if n % pref == 0 else n


def _layer(x, attend, shift0, s0, t_real, lam_init, lw):
    b, t, d = x.shape
    n = b * t
    xf = x.reshape(n, d)
    w_da = lw["w_q"].shape[1]
    tm = _tile(n, 512)
    h1 = _rms(xf, lw["norm1_g"], tm)
    tmp = _tile(n, 1024)
    qb = _proj("q", h1, lw["w_q"], tmp, _tile(w_da, 512), gain=lw["q_gain"], group=lw["dk"], scale=lw["dk"] ** -0.5)
    if t % LANES == 0:
        seqs, tmk = b, _tile(t, 1024)
    else:
        seqs, tmk = 1, n
    k, kbt = _proj_keys(h1, lw["w_k"], lw["k_gain"], lw["dk"], seqs, tmk, _tile(w_da, 512))
    v, vb = _proj("dual", h1, lw["w_v"], tmp, _tile(w_da, 512))
    p_pad = lw["w_rw"].shape[1]
    rw = _proj("plain", h1, lw["w_rw"], tmp, p_pad // 3)

    o_da = attend(qb.reshape(b, t, w_da), kbt, vb.reshape(b, t, w_da))

    rw3 = rw.reshape(b, t, p_pad)
    if t % RW_CHUNK:
        rw3 = jnp.pad(rw3, ((0, 0), (0, RW_CHUNK - t % RW_CHUNK), (0, 0)))
    o_rw, s_out = _rwkv(rw3, shift0, s0, lw["rwkv"], t_real)
    o_rw = o_rw[:, :t].reshape(n, -1)

    x1, h2, h2t = _outproj(xf, o_da.reshape(n, w_da), o_rw, lw["w_out_da"], lw["w_out_rw"], lw["norm2_g"],
                           _tile(n, 256))
    tn = _tile(n, 512)
    scores = _peer_scores(h2t, lw["wq_t"], lw["peer_keys"], tn)
    code, gate = _peer_pairs(scores, tn)
    ib = 8
    gates = _peer_gates(code, gate, scores.shape[1], _tile(n, 128), ib)
    out = _peer_experts(h2, x1, gates, lw["peer_u"], lw["peer_v"], tn, ib)
    return out.reshape(b, t, d), k, v, s_out, rw3[:, t - 1, :lw["rw_proj"]]


def kernel(x_prompt, x_sample, cache_k, cache_v, state_wkv, state_shift, page_table, norm1_g, w_in, da_qk_g,
           da_lambda, da_subln_g, rw_mu, rw_w0, rw_w2, rw_a0, rw_a2, rw_g2, rw_kk, rw_ka, rw_rk, rw_lnx,
           w_out, norm2_g, peer_wq, peer_keys, peer_u, peer_v):
    depth = w_in.shape[0]
    d = x_prompt.shape[-1]
    heads, dk = cache_k.shape[3], cache_k.shape[5]
    w_da = heads * 2 * dk
    rw_heads, head = state_wkv.shape[2], state_wkv.shape[3]
    wd = rw_heads * head
    r_w, r_a, r_g = rw_w2.shape[1], rw_a2.shape[1], rw_g2.shape[1]
    rw_proj = 3 * wd + r_w + r_a + r_g
    g_pad = 2 * LANES
    p_pad = 3 * wd + LANES + g_pad
    assert d == w_da + wd and head * 2 == LANES and wd % LANES == 0 and 2 * dk == LANES
    assert r_w + r_a == LANES and r_g <= g_pad and p_pad % 3 == 0 and (p_pad // 3) % LANES == 0
    assert cache_k.shape[2] == PAGE_SIZE
    nh, _, n_keys, qd = peer_keys.shape[1:]
    bsz = x_prompt.shape[0]
    dec_b, dec_t = x_sample.shape[:2]
    assert dec_t * heads <= PAGE_SIZE and x_prompt.shape[1] % LANES == 0
    assert n_keys & (n_keys - 1) == 0 and peer_u.shape[1] == n_keys * n_keys

    xp, xs = x_prompt, x_sample
    outs = [[] for _ in range(8)]
    for l in range(depth):
        lam_init = 0.8 - 0.6 * math.exp(-0.3 * l)
        wl = w_in[l].astype(BF16)
        w_rw = jnp.pad(wl[:, 3 * w_da:], ((0, 0), (0, p_pad - rw_proj)))
        zero_wa = jnp.zeros((r_w, wd), F32)
        wwa = jnp.concatenate([jnp.concatenate([rw_w2[l], zero_wa], axis=1),
                               jnp.concatenate([zero_wa, rw_a2[l]], axis=1)], axis=0).astype(BF16)
        row = lambda z: z.reshape(1, -1)
        rwkv_prm = dict(
            mu=jnp.pad(row(rw_mu[l]), ((0, 0), (0, p_pad - rw_proj))),
            w0=row(rw_w0[l]), a0=row(rw_a0[l]), wwa=wwa,
            g2=jnp.pad(rw_g2[l], ((0, g_pad - r_g), (0, 0))).astype(BF16),
            kk=row(rw_kk[l]), ka=row(rw_ka[l]), rk=row(rw_rk[l]),
            lng=row(rw_lnx[l, 0]), lnb=row(rw_lnx[l, 1]),
            tril=jnp.tril(jnp.ones((RW_CHUNK, RW_CHUNK), BF16)),
            bd=_block_diag_ones(LANES, head), r_w=r_w, head=head)
        lw = dict(
            norm1_g=norm1_g[l], w_q=wl[:, :w_da], w_k=wl[:, w_da:2 * w_da], w_v=wl[:, 2 * w_da:3 * w_da],
            w_rw=w_rw, q_gain=jnp.tile(da_qk_g[l, 0], 2 * heads), k_gain=jnp.tile(da_qk_g[l, 1], 2 * heads),
            dk=dk, rwkv=rwkv_prm, rw_proj=rw_proj,
            w_out_da=w_out[l, :w_da].astype(BF16), w_out_rw=w_out[l, w_da:].astype(BF16),
            norm2_g=norm2_g[l], wq_t=peer_wq[l].T.astype(BF16),
            peer_keys=peer_keys[l].reshape(nh * 2, n_keys, qd).astype(BF16),
            peer_u=peer_u[l].astype(BF16), peer_v=peer_v[l].astype(BF16))

        attend_p = functools.partial(_attn_prompt, da_lambda=da_lambda[l], subln_g=da_subln_g[l], heads=heads,
                                     lam_init=lam_init, tq=_tile(x_prompt.shape[1], 512))
        n_pairs = wd // LANES
        xp, kp, vp, sp, shp = _layer(
            xp, attend_p, jnp.zeros((bsz, 1, p_pad), F32), jnp.zeros((bsz, n_pairs, LANES, LANES), F32),
            RW_CHUNK, lam_init, lw)

        attend_s = functools.partial(_attn_sample, cache_k=cache_k, cache_v=cache_v, layer=l,
                                     page_table=page_table, da_lambda=da_lambda[l], subln_g=da_subln_g[l],
                                     heads=heads, lam_init=lam_init,
                                     pp=8 if page_table.shape[1] % 8 == 0 else 4)
        shift_s = jnp.pad(state_shift[l], ((0, 0), (0, p_pad - rw_proj))).reshape(dec_b, 1, p_pad)
        xs, ksm, vsm, ssm, shs = _layer(xs, attend_s, shift_s, _pack_state(state_wkv[l], head), dec_t, lam_init, lw)

        outs[0].append(jnp.transpose(kp.reshape(bsz, heads, 2, dk, -1), (0, 4, 1, 2, 3)))
        outs[1].append(vp.reshape(bsz, -1, heads, 2 * dk))
        outs[2].append(_unpack_state(sp, head))
        outs[3].append(shp)
        outs[4].append(jnp.transpose(ksm.reshape(heads, 2, dk, dec_b, dec_t), (3, 4, 0, 1, 2)))
        outs[5].append(vsm.reshape(dec_b, dec_t, heads, 2 * dk))
        outs[6].append(_unpack_state(ssm, head))
        outs[7].append(shs)
    return (xp, xs) + tuple(jnp.stack(o) for o in outs)
```

```python
import functools
import math

import jax
import jax.numpy as jnp
from jax import lax
from jax.experimental import pallas as pl
from jax.experimental.pallas import tpu as pltpu

F32 = jnp.float32
BF16 = jnp.bfloat16

LANES = 128
RMS_EPS = 1e-6
LNX_EPS = 64e-5
PEER_TOPK = 16
PAGE_SIZE = 128
RW_CHUNK = 64
VMEM_LIMIT = 48 * 1024 * 1024

_NT = (((1,), (1,)), ((), ()))
_TN = (((0,), (0,)), ((), ()))


def _params(sem):
    return pltpu.CompilerParams(dimension_semantics=sem, vmem_limit_bytes=VMEM_LIMIT)


def _mm(a, b):
    return jnp.dot(a.astype(BF16), b.astype(BF16), preferred_element_type=F32)


def _mm_nt(a, b):
    return lax.dot_general(a.astype(BF16), b.astype(BF16), _NT, preferred_element_type=F32)


def _mm_tn(a, b):
    return lax.dot_general(a.astype(BF16), b.astype(BF16), _TN, preferred_element_type=F32)


def _mm_split(a_exact_bf16, b):
    hi = b.astype(BF16)
    lo = (b - hi.astype(F32)).astype(BF16)
    return (jnp.dot(a_exact_bf16, hi, preferred_element_type=F32)
            + jnp.dot(a_exact_bf16, lo, preferred_element_type=F32))


def _mm_split_lhs(a, b_exact_bf16):
    hi = a.astype(BF16)
    lo = (a - hi.astype(F32)).astype(BF16)
    return (jnp.dot(hi, b_exact_bf16, preferred_element_type=F32)
            + jnp.dot(lo, b_exact_bf16, preferred_element_type=F32))


def _block_diag_ones(n, group):
    i = jnp.arange(n)
    return (i[:, None] // group == i[None, :] // group).astype(BF16)


def _rms_kernel(x_ref, g_ref, o_ref):
    x = x_ref[...]
    ms = jnp.mean(x * x, axis=-1, keepdims=True)
    o_ref[...] = (x * lax.rsqrt(ms + RMS_EPS) * g_ref[...]).astype(o_ref.dtype)


def _rms(x, g, tm):
    n, d = x.shape
    return pl.pallas_call(
        _rms_kernel,
        grid=(n // tm,),
        in_specs=[pl.BlockSpec((tm, d), lambda i: (i, 0)), pl.BlockSpec((1, d), lambda i: (0, 0))],
        out_specs=pl.BlockSpec((tm, d), lambda i: (i, 0)),
        out_shape=jax.ShapeDtypeStruct((n, d), BF16),
        compiler_params=_params(("parallel",)),
        name="rms",
    )(x, g.reshape(1, d))


def _proj_plain_kernel(h_ref, w_ref, o_ref):
    o_ref[...] = jnp.dot(h_ref[...], w_ref[...], preferred_element_type=F32)


def _proj_dual_kernel(h_ref, w_ref, o_ref, ob_ref):
    y = jnp.dot(h_ref[...], w_ref[...], preferred_element_type=F32)
    o_ref[...] = y
    ob_ref[...] = y.astype(BF16)


def _qk_normed(h_ref, w_ref, bd_ref, g_ref, group):
    y = jnp.dot(h_ref[...], w_ref[...], preferred_element_type=F32)
    yy = y * y
    bd = bd_ref[...]
    sums = [_mm_split_lhs(yy[:, c * LANES:(c + 1) * LANES], bd) for c in range(y.shape[1] // LANES)]
    ms = jnp.concatenate(sums, axis=1) * (1.0 / group)
    return y * lax.rsqrt(ms + RMS_EPS) * g_ref[...]


def _proj_q_kernel(h_ref, w_ref, bd_ref, g_ref, ob_ref, *, group, scale):
    ob_ref[...] = (_qk_normed(h_ref, w_ref, bd_ref, g_ref, group) * scale).astype(BF16)


def _proj_k_kernel(h_ref, w_ref, bd_ref, g_ref, o_ref, ob_ref, *, group):
    yt = _qk_normed(h_ref, w_ref, bd_ref, g_ref, group).T
    o_ref[0] = yt
    ob_ref[0] = yt.astype(BF16)


def _proj_keys(h, w, gain, group, seqs, tm, tn):
    n, d = h.shape
    ncol = w.shape[1]
    s = n // seqs
    tiles = s // tm
    o_spec = pl.BlockSpec((1, tn, tm), lambda i, j: (i // tiles, j, i % tiles))
    return pl.pallas_call(
        functools.partial(_proj_k_kernel, group=group),
        grid=(n // tm, ncol // tn),
        in_specs=[pl.BlockSpec((tm, d), lambda i, j: (i, 0)), pl.BlockSpec((d, tn), lambda i, j: (0, j)),
                  pl.BlockSpec((LANES, LANES), lambda i, j: (0, 0)), pl.BlockSpec((1, tn), lambda i, j: (0, j))],
        out_specs=(o_spec, o_spec),
        out_shape=(jax.ShapeDtypeStruct((seqs, ncol, s), F32), jax.ShapeDtypeStruct((seqs, ncol, s), BF16)),
        compiler_params=_params(("parallel", "arbitrary")),
        name="proj_k",
    )(h, w, _block_diag_ones(LANES, group), gain.reshape(1, ncol))


def _proj(kind, h, w, tm, tn, gain=None, group=None, scale=None):
    n, d = h.shape
    ncol = w.shape[1]
    grid = (n // tm, ncol // tn)
    h_spec = pl.BlockSpec((tm, d), lambda i, j: (i, 0))
    w_spec = pl.BlockSpec((d, tn), lambda i, j: (0, j))
    o_spec = pl.BlockSpec((tm, tn), lambda i, j: (i, j))
    f32_out = jax.ShapeDtypeStruct((n, ncol), F32)
    bf_out = jax.ShapeDtypeStruct((n, ncol), BF16)
    if kind == "q":
        bd = _block_diag_ones(LANES, group)
        extra = [bd, gain.reshape(1, ncol)]
        extra_specs = [pl.BlockSpec((LANES, LANES), lambda i, j: (0, 0)), pl.BlockSpec((1, tn), lambda i, j: (0, j))]
    else:
        extra, extra_specs = [], []
    if kind == "q":
        body = functools.partial(_proj_q_kernel, group=group, scale=scale)
        out_shape, out_specs = bf_out, o_spec
    elif kind == "dual":
        body = _proj_dual_kernel
        out_shape, out_specs = (f32_out, bf_out), (o_spec, o_spec)
    else:
        body = _proj_plain_kernel
        out_shape, out_specs = f32_out, o_spec
    return pl.pallas_call(
        body,
        grid=grid,
        in_specs=[h_spec, w_spec] + extra_specs,
        out_specs=out_specs,
        out_shape=out_shape,
        compiler_params=_params(("parallel", "arbitrary")),
        name="proj_" + kind,
    )(h, w, *extra)


def _lambda(lam_ref, lam_init):
    ll = lam_ref[...]
    s01 = jnp.sum(ll[0:1, :] * ll[1:2, :], axis=-1, keepdims=True)
    s23 = jnp.sum(ll[2:3, :] * ll[3:4, :], axis=-1, keepdims=True)
    return jnp.exp(s01) - jnp.exp(s23) + lam_init


def _subln(o, g, lam_init):
    ms = jnp.mean(o * o, axis=-1, keepdims=True)
    return (o * lax.rsqrt(ms + RMS_EPS) * g) * (1.0 - lam_init)


def _attn_prompt_kernel(q_ref, k_ref, v_ref, lam_ref, g_ref, o_ref, acc0_ref, acc1_ref, *, tq, dk, lam_init):
    qi = pl.program_id(2)
    q = q_ref[0]
    lane = lax.broadcasted_iota(jnp.int32, q.shape, 1)
    qf = q.astype(F32)
    qc = (jnp.where(lane < dk, qf, 0.0).astype(BF16), jnp.where(lane >= dk, qf, 0.0).astype(BF16))
    accs = (acc0_ref, acc1_ref)
    acc0_ref[...] = jnp.zeros_like(acc0_ref)
    acc1_ref[...] = jnp.zeros_like(acc1_ref)
    causal = (lax.broadcasted_iota(jnp.int32, (tq, tq), 1) <= lax.broadcasted_iota(jnp.int32, (tq, tq), 0))

    def step(j, carry, diagonal):
        kt = k_ref[0, :, pl.ds(pl.multiple_of(j * tq, tq), tq)]
        vb = v_ref[0, pl.ds(pl.multiple_of(j * tq, tq), tq), :]
        new = []
        for c in range(2):
            m, l = carry[2 * c], carry[2 * c + 1]
            s = jnp.dot(qc[c], kt, preferred_element_type=F32)
            if diagonal:
                s = jnp.where(causal, s, -jnp.inf)
            m_new = jnp.maximum(m, jnp.max(s, axis=-1, keepdims=True))
            corr = jnp.exp(m - m_new)
            p = jnp.exp(s - m_new)
            l = l * corr + jnp.sum(p, axis=-1, keepdims=True)
            accs[c][...] = accs[c][...] * corr + jnp.dot(p.astype(BF16), vb, preferred_element_type=F32)
            new += [m_new, l]
        return tuple(new)

    minit = jnp.full((tq, 1), -jnp.inf, F32)
    linit = jnp.zeros((tq, 1), F32)
    carry = lax.fori_loop(0, qi, functools.partial(step, diagonal=False), (minit, linit, minit, linit))
    _, l0, _, l1 = step(qi, carry, True)
    lam = _lambda(lam_ref, lam_init)
    o = acc0_ref[...] / l0 - lam * (acc1_ref[...] / l1)
    o_ref[0] = _subln(o, g_ref[...], lam_init).astype(o_ref.dtype)


def _attn_prompt(qb, kbt, vb, da_lambda, subln_g, heads, lam_init, tq):
    b, s, w = qb.shape
    dv = w // heads
    body = functools.partial(_attn_prompt_kernel, tq=tq, dk=dv // 2, lam_init=lam_init)
    return pl.pallas_call(
        body,
        grid=(b, heads, s // tq),
        in_specs=[
            pl.BlockSpec((1, tq, dv), lambda bi, h, qi: (bi, qi, h)),
            pl.BlockSpec((1, dv, s), lambda bi, h, qi: (bi, h, 0)),
            pl.BlockSpec((1, s, dv), lambda bi, h, qi: (bi, 0, h)),
            pl.BlockSpec(da_lambda.shape, lambda bi, h, qi: (0, 0)),
            pl.BlockSpec((1, dv), lambda bi, h, qi: (0, 0)),
        ],
        out_specs=pl.BlockSpec((1, tq, dv), lambda bi, h, qi: (bi, qi, h)),
        out_shape=jax.ShapeDtypeStruct((b, s, w), BF16),
        scratch_shapes=[pltpu.VMEM((tq, dv), F32), pltpu.VMEM((tq, dv), F32)],
        compiler_params=_params(("parallel", "parallel", "arbitrary")),
        name="attn_prompt",
    )(qb, kbt, vb, da_lambda, subln_g.reshape(1, dv))


def _attn_sample_kernel(pt_ref, qrep_ref, kst_ref, vs_ref, lam_ref, g_ref, *rest, pp, heads, t, dk, lam_init):
    kc_refs = rest[:pp]
    vc_refs = rest[pp:2 * pp]
    o_ref = rest[2 * pp]
    qbd_ref, m_ref, l_ref, acc_ref = rest[2 * pp + 1:]
    del pt_ref
    p_idx = pl.program_id(1)
    rows = 2 * t * heads
    dv = 2 * dk
    w = heads * dv

    @pl.when(p_idx == 0)
    def _():
        r = lax.broadcasted_iota(jnp.int32, (rows, w), 0)
        col = lax.broadcasted_iota(jnp.int32, (rows, w), 1)
        keep = (col // dk) == ((r % heads) * 2 + r // (t * heads))
        qbd_ref[...] = jnp.where(keep, qrep_ref[0].astype(F32), 0.0).astype(BF16)
        m_ref[...] = jnp.full(m_ref.shape, -jnp.inf, F32)
        l_ref[...] = jnp.zeros(l_ref.shape, F32)
        acc_ref[...] = jnp.zeros(acc_ref.shape, F32)

    qbd = qbd_ref[...]

    def online(s, vbs):
        m = m_ref[...]
        m_new = jnp.maximum(m, jnp.max(s, axis=-1, keepdims=True))
        corr = jnp.exp(m - m_new)
        p = jnp.exp(s - m_new)
        l_ref[...] = l_ref[...] * corr + jnp.sum(p, axis=-1, keepdims=True)
        pv = None
        for i, vb in enumerate(vbs):
            n = vb.shape[0]
            d = jnp.dot(p[:, i * n:(i + 1) * n].astype(BF16), vb, preferred_element_type=F32)
            pv = d if pv is None else pv + d
        acc_ref[...] = acc_ref[...] * corr + pv
        m_ref[...] = m_new

    kt = jnp.concatenate([kc[0] for kc in kc_refs], axis=1).astype(BF16)
    vwide = [jnp.concatenate([vc[0, pl.ds(h, PAGE_SIZE, stride=heads), :] for h in range(heads)], axis=1)
             .astype(BF16) for vc in vc_refs]
    online(jnp.dot(qbd, kt, preferred_element_type=F32), vwide)

    @pl.when(p_idx == pl.num_programs(1) - 1)
    def _():
        s = jnp.dot(qbd, kst_ref[0], preferred_element_type=F32)
        r = lax.broadcasted_iota(jnp.int32, s.shape, 0)
        col = lax.broadcasted_iota(jnp.int32, s.shape, 1)
        online(jnp.where(col <= (r // heads) % t, s, -jnp.inf), [vs_ref[0]])
        out = acc_ref[...] / l_ref[...]
        hrow = lax.broadcasted_iota(jnp.int32, (heads, w), 0)
        hcol = lax.broadcasted_iota(jnp.int32, (heads, w), 1) // dv
        pick = hrow == hcol
        lam = _lambda(lam_ref, lam_init)
        toks = []
        for tok_i in range(t):
            maps = []
            for c in range(2):
                base = (c * t + tok_i) * heads
                maps.append(jnp.sum(jnp.where(pick, out[base:base + heads, :], 0.0), axis=0, keepdims=True))
            toks.append(maps[0] - lam * maps[1])
        o = jnp.concatenate(toks, axis=0)
        g = g_ref[...]
        outs = [_subln(o[:, h * dv:(h + 1) * dv], g, lam_init) for h in range(heads)]
        o_ref[0] = jnp.concatenate(outs, axis=1).astype(o_ref.dtype)


def _attn_sample(qb, kbt, vb, cache_k, cache_v, layer, page_table, da_lambda, subln_g, heads, lam_init, pp):
    bd, t, w = qb.shape
    kb = jnp.transpose(kbt.reshape(w, bd, t), (1, 2, 0))
    dv = w // heads
    n_pool = cache_k.shape[1]
    n_pages = page_table.shape[1]
    rows = 2 * t * heads
    qrep = jnp.tile(jnp.repeat(qb, heads, axis=1), (1, 2, 1))
    kst_pad = jnp.pad(jnp.swapaxes(kb, 1, 2), ((0, 0), (0, 0), (0, PAGE_SIZE - t)))
    vs_pad = jnp.pad(vb, ((0, 0), (0, PAGE_SIZE - t), (0, 0)))
    ckt = jnp.transpose(cache_k, (0, 1, 3, 4, 5, 2)).reshape(-1, w, PAGE_SIZE)
    cv = cache_v.reshape(-1, PAGE_SIZE * heads, dv)
    base = layer * n_pool

    def k_spec(i):
        return pl.BlockSpec((1, w, PAGE_SIZE), lambda b, p, pt: (base + pt[b, p * pp + i], 0, 0))

    def v_spec(i):
        return pl.BlockSpec((1, PAGE_SIZE * heads, dv), lambda b, p, pt: (base + pt[b, p * pp + i], 0, 0))

    body = functools.partial(_attn_sample_kernel, pp=pp, heads=heads, t=t, dk=dv // 2, lam_init=lam_init)
    grid_spec = pltpu.PrefetchScalarGridSpec(
        num_scalar_prefetch=1,
        grid=(bd, n_pages // pp),
        in_specs=[
            pl.BlockSpec((1, rows, w), lambda b, p, pt: (b, 0, 0)),
            pl.BlockSpec((1, w, PAGE_SIZE), lambda b, p, pt: (b, 0, 0)),
            pl.BlockSpec((1, PAGE_SIZE, w), lambda b, p, pt: (b, 0, 0)),
            pl.BlockSpec(da_lambda.shape, lambda b, p, pt: (0, 0)),
            pl.BlockSpec((1, dv), lambda b, p, pt: (0, 0)),
        ] + [k_spec(i) for i in range(pp)] + [v_spec(i) for i in range(pp)],
        out_specs=pl.BlockSpec((1, t, w), lambda b, p, pt: (b, 0, 0)),
        scratch_shapes=[pltpu.VMEM((rows, w), BF16), pltpu.VMEM((rows, 1), F32),
                        pltpu.VMEM((rows, 1), F32), pltpu.VMEM((rows, w), F32)],
    )
    return pl.pallas_call(
        body,
        grid_spec=grid_spec,
        out_shape=jax.ShapeDtypeStruct((bd, t, w), BF16),
        compiler_params=_params(("parallel", "arbitrary")),
        name="attn_sample",
    )(page_table, qrep, kst_pad, vs_pad, da_lambda, subln_g.reshape(1, dv), *([ckt] * pp), *([cv] * pp))


def _softplus(z):
    return jnp.maximum(z, 0.0) + jnp.log1p(jnp.exp(-jnp.abs(z)))


def _sigmoid(z):
    return 1.0 / (1.0 + jnp.exp(-z))


def _split_heads(x, first):
    zero = jnp.zeros_like(x)
    return jnp.concatenate([jnp.where(first, x, zero), jnp.where(first, zero, x)], axis=0)


def _rwkv_kernel(rw_ref, shift_ref, s0_ref, mu_ref, w0_ref, a0_ref, wwa_ref, g2_ref, kk_ref, ka_ref,
                 rk_ref, lng_ref, lnb_ref, tril_ref, bd_ref, o_ref, sout_ref, state_ref, prev_ref,
                 *, wd, r_w, t_real, head):
    c_idx = pl.program_id(1)
    cl = RW_CHUNK
    pairs = range(wd // LANES)

    @pl.when(c_idx == 0)
    def _():
        state_ref[...] = s0_ref[0]
        prev_ref[...] = shift_ref[0]

    x = rw_ref[0]
    rowp = lax.broadcasted_iota(jnp.int32, x.shape, 0)
    prevx = jnp.where(rowp == 0, prev_ref[...], pltpu.roll(x, 1, axis=0))
    prev_ref[...] = x[cl - 1:cl, :]
    xm = x + (prevx - x) * mu_ref[...]
    r = xm[:, 0:wd]
    kr = xm[:, wd:2 * wd]
    vr = xm[:, 2 * wd:3 * wd]
    slab = xm[:, 3 * wd:3 * wd + LANES]
    lane_s = lax.broadcasted_iota(jnp.int32, slab.shape, 1)
    wa = _mm(jnp.where(lane_s < r_w, jnp.tanh(slab), slab), wwa_ref[...])
    gd = xm[:, 3 * wd + LANES:]
    g = _mm(_sigmoid(gd), g2_ref[...])
    w = -_softplus(-(w0_ref[...] + wa[:, :wd])) - 0.5
    lw = -jnp.exp(w)
    a = _sigmoid(a0_ref[...] + wa[:, wd:])
    kk = kr * kk_ref[...]
    kh = kr * (1.0 + (a - 1.0) * ka_ref[...])
    beta_scale = a
    if t_real < cl:
        valid = lax.broadcasted_iota(jnp.int32, (cl, wd), 0) < t_real
        lw = jnp.where(valid, lw, 0.0)
        beta_scale = jnp.where(valid, a, 0.0)
        kh_state = jnp.where(valid, kh, 0.0)
    else:
        kh_state = kh
    cum = _mm_split(tril_ref[...], lw)
    cum_last = cum[cl - 1:cl, :]
    e_neg = jnp.exp(-cum)
    e_end = jnp.exp(cum_last - cum)
    w_end = jnp.exp(cum_last)
    rt_all = r * jnp.exp(cum)
    ex_all = jnp.exp(cum - lw)
    bonus_in = r * kh * rk_ref[...]

    lane = lax.broadcasted_iota(jnp.int32, (cl, LANES), 1)
    first = lane < head
    tt = lax.broadcasted_iota(jnp.int32, (cl, 2 * cl), 0)
    ss = lax.broadcasted_iota(jnp.int32, (cl, 2 * cl), 1) % cl
    strict = ss < tt
    incl = ss <= tt
    firstw = lax.broadcasted_iota(jnp.int32, (cl, 2 * cl), 1) < cl
    r2 = lax.broadcasted_iota(jnp.int32, (2 * cl, 2 * cl), 0)
    c2 = lax.broadcasted_iota(jnp.int32, (2 * cl, 2 * cl), 1)
    eye = (r2 == c2).astype(F32)
    rs = lax.broadcasted_iota(jnp.int32, (LANES, LANES), 0) // head
    cs = lax.broadcasted_iota(jnp.int32, (LANES, LANES), 1) // head
    same_head = rs == cs
    bd = bd_ref[...]
    n_levels = int(math.log2(cl))
    sls = [slice(p * LANES, (p + 1) * LANES) for p in pairs]

    kk_p = [kk[:, sl] for sl in sls]
    nrm = [jnp.sqrt(_mm(k * k, bd)) for k in kk_p]
    kkn = [k / jnp.maximum(n, 1e-12) for k, n in zip(kk_p, nrm)]
    beta = [kkn[p] * beta_scale[:, sls[p]] for p in pairs]
    v_p = [vr[:, sl] for sl in sls]
    st = [state_ref[p] for p in pairs]
    lhs = [jnp.concatenate([-ex_all[:, sls[p]] * kkn[p], rt_all[:, sls[p]]], axis=0) for p in pairs]
    rhs = [jnp.concatenate([_split_heads(beta[p] * e_neg[:, sls[p]], first),
                            _split_heads(kh_state[:, sls[p]] * e_neg[:, sls[p]], first)], axis=0) for p in pairs]
    m = [_mm_nt(lhs[p], rhs[p]) for p in pairs]
    sa = [_mm_nt(lhs[p], st[p]) for p in pairs]
    aab = [jnp.where(strict, x_[:cl, :2 * cl], 0.0) for x_ in m]
    aak = [jnp.where(strict, x_[:cl, 2 * cl:], 0.0) for x_ in m]
    arbk = [jnp.concatenate([jnp.where(incl, x_[cl:, :2 * cl], 0.0), jnp.where(incl, x_[cl:, 2 * cl:], 0.0)], axis=1)
            for x_ in m]
    abd = [_split_heads(x_, firstw) for x_ in aab]
    vbd = [_split_heads(x_, first) for x_ in v_p]
    rhs_u = [sa[p][:cl, :] + _mm(aak[p], vbd[p]) for p in pairs]
    tinv = [eye + x_ for x_ in abd]
    xp = [_mm(x_, x_) for x_ in abd]
    for lvl in range(1, n_levels):
        if lvl < n_levels - 1:
            both = [_mm(xp[p], jnp.concatenate([tinv[p], xp[p]], axis=1)) for p in pairs]
            tinv = [tinv[p] + both[p][:, :2 * cl] for p in pairs]
            xp = [both[p][:, 2 * cl:] for p in pairs]
        else:
            tinv = [tinv[p] + _mm(xp[p], tinv[p]) for p in pairs]
    tw = [x_[:cl, :] + x_[cl:, :] for x_ in tinv]
    u = [_mm(tw[p], _split_heads(rhs_u[p], first)) for p in pairs]
    y = [sa[p][cl:, :] + _mm(arbk[p], jnp.concatenate([_split_heads(u[p], first), vbd[p]], axis=0)) for p in pairs]
    ds = [_mm_tn(jnp.concatenate([u[p], v_p[p]], axis=0),
                 jnp.concatenate([beta[p] * e_end[:, sls[p]], kh_state[:, sls[p]] * e_end[:, sls[p]]], axis=0))
          for p in pairs]
    mean = [_mm(x_, bd) * (1.0 / head) for x_ in y]
    dlt = [y[p] - mean[p] for p in pairs]
    var = [_mm(x_ * x_, bd) * (1.0 / head) for x_ in dlt]
    bonus = [_mm(bonus_in[:, sls[p]], bd) * v_p[p] for p in pairs]
    for p in pairs:
        sl = sls[p]
        state_ref[p] = st[p] * w_end[:, sl] + jnp.where(same_head, ds[p], 0.0)
        yn = dlt[p] * lax.rsqrt(var[p] + LNX_EPS) * lng_ref[:, sl] + lnb_ref[:, sl]
        o_ref[0, :, sl] = ((yn + bonus[p]) * g[:, sl]).astype(o_ref.dtype)

    @pl.when(c_idx == pl.num_programs(1) - 1)
    def _():
        sout_ref[0] = state_ref[...]


def _rwkv(rw, shift0, s0, prm, t_real):
    b, s, p = rw.shape
    wd = prm["w0"].shape[1]
    n_pairs = wd // LANES
    cl = RW_CHUNK
    body = functools.partial(_rwkv_kernel, wd=wd, r_w=prm["r_w"], t_real=t_real, head=prm["head"])

    def const(shape):
        return pl.BlockSpec(shape, lambda bi, ci: (0,) * len(shape))

    names = ["mu", "w0", "a0", "wwa", "g2", "kk", "ka", "rk", "lng", "lnb", "tril", "bd"]
    return pl.pallas_call(
        body,
        grid=(b, s // cl),
        in_specs=[
            pl.BlockSpec((1, cl, p), lambda bi, ci: (bi, ci, 0)),
            pl.BlockSpec((1, 1, p), lambda bi, ci: (bi, 0, 0)),
            pl.BlockSpec((1, n_pairs, LANES, LANES), lambda bi, ci: (bi, 0, 0, 0)),
        ] + [const(prm[nm].shape) for nm in names],
        out_specs=(pl.BlockSpec((1, cl, wd), lambda bi, ci: (bi, ci, 0)),
                   pl.BlockSpec((1, n_pairs, LANES, LANES), lambda bi, ci: (bi, 0, 0, 0))),
        out_shape=(jax.ShapeDtypeStruct((b, s, wd), BF16),
                   jax.ShapeDtypeStruct((b, n_pairs, LANES, LANES), F32)),
        scratch_shapes=[pltpu.VMEM((n_pairs, LANES, LANES), F32), pltpu.VMEM((1, p), F32)],
        compiler_params=_params(("parallel", "arbitrary")),
        name="rwkv",
    )(rw, shift0, s0, *[prm[nm] for nm in names])


def _pack_state(wkv, head):
    b, h = wkv.shape[:2]
    pairs = wkv.reshape(b, h // 2, 2, head, head)
    z = jnp.zeros((b, h // 2, head, head), wkv.dtype)
    top = jnp.concatenate([pairs[:, :, 0], z], axis=-1)
    bot = jnp.concatenate([z, pairs[:, :, 1]], axis=-1)
    return jnp.concatenate([top, bot], axis=-2)


def _unpack_state(sbd, head):
    b, n_pairs = sbd.shape[:2]
    h0 = sbd[:, :, :head, :head]
    h1 = sbd[:, :, head:, head:]
    return jnp.stack([h0, h1], axis=2).reshape(b, 2 * n_pairs, head, head)


def _outproj_kernel(x_ref, oda_ref, orw_ref, w1_ref, w2_ref, g_ref, x1_ref, h_ref, ht_ref):
    y = (jnp.dot(oda_ref[...], w1_ref[...], preferred_element_type=F32)
         + jnp.dot(orw_ref[...], w2_ref[...], preferred_element_type=F32))
    x1 = x_ref[...] + y
    x1_ref[...] = x1
    ms = jnp.mean(x1 * x1, axis=-1, keepdims=True)
    h2 = x1 * lax.rsqrt(ms + RMS_EPS) * g_ref[...]
    h_ref[...] = h2.astype(BF16)
    ht_ref[...] = h2.T.astype(BF16)


def _outproj(x, oda, orw, w1, w2, g, tm):
    n, d = x.shape
    wa, wr = oda.shape[1], orw.shape[1]
    return pl.pallas_call(
        _outproj_kernel,
        grid=(n // tm,),
        in_specs=[
            pl.BlockSpec((tm, d), lambda i: (i, 0)),
            pl.BlockSpec((tm, wa), lambda i: (i, 0)),
            pl.BlockSpec((tm, wr), lambda i: (i, 0)),
            pl.BlockSpec((wa, d), lambda i: (0, 0)),
            pl.BlockSpec((wr, d), lambda i: (0, 0)),
            pl.BlockSpec((1, d), lambda i: (0, 0)),
        ],
        out_specs=(pl.BlockSpec((tm, d), lambda i: (i, 0)), pl.BlockSpec((tm, d), lambda i: (i, 0)),
                   pl.BlockSpec((d, tm), lambda i: (0, i))),
        out_shape=(jax.ShapeDtypeStruct((n, d), F32), jax.ShapeDtypeStruct((n, d), BF16),
                   jax.ShapeDtypeStruct((d, n), BF16)),
        compiler_params=_params(("parallel",)),
        name="outproj",
    )(x, oda, orw, w1, w2, g.reshape(1, d))


def _peer_scores_kernel(ht_ref, wqt_ref, keys_ref, s_ref, q_ref):
    q_ref[...] = jnp.dot(wqt_ref[...], ht_ref[...], preferred_element_type=F32).astype(BF16)
    nhc, _, qd = keys_ref.shape
    for j in range(nhc):
        s_ref[j] = jnp.dot(keys_ref[j], q_ref[j * qd:(j + 1) * qd, :], preferred_element_type=F32)


def _peer_scores(ht, wqt, keys, tn):
    d, n = ht.shape
    nhc, n_keys, qd = keys.shape
    once = pl.Buffered(1)
    return pl.pallas_call(
        _peer_scores_kernel,
        grid=(n // tn,),
        in_specs=[
            pl.BlockSpec((d, tn), lambda t: (0, t)),
            pl.BlockSpec((nhc * qd, d), lambda t: (0, 0), pipeline_mode=once),
            pl.BlockSpec((nhc, n_keys, qd), lambda t: (0, 0, 0), pipeline_mode=once),
        ],
        out_specs=pl.BlockSpec((nhc, n_keys, tn), lambda t: (0, 0, t)),
        out_shape=jax.ShapeDtypeStruct((nhc, n_keys, n), F32),
        scratch_shapes=[pltpu.VMEM((nhc * qd, tn), BF16)],
        compiler_params=_params(("parallel",)),
        name="peer_scores",
    )(ht, wqt, keys)


def _gelu(x):
    return 0.5 * x * (1.0 + lax.erf(x * (2.0 ** -0.5)))


def _take_top(vals, k, payload=None):
    n_rows = vals.shape[0]
    rowid = lax.broadcasted_iota(jnp.int32, vals.shape, 0).astype(F32)
    cur = vals
    top, picked = [], []
    for _ in range(k):
        m = jnp.max(cur, axis=0, keepdims=True)
        sel = jnp.min(jnp.where(cur == m, rowid, float(n_rows)), axis=0, keepdims=True)
        hit = rowid == sel
        top.append(m)
        picked.append(sel if payload is None else jnp.sum(jnp.where(hit, payload, 0.0), axis=0, keepdims=True))
        cur = jnp.where(hit, -jnp.inf, cur)
    return top, picked


def _peer_pairs_kernel(s_ref, code_ref, g_ref, *, n_keys):
    s1 = s_ref[0]
    s2 = s_ref[1]
    a, ia = _take_top(s1, PEER_TOPK)
    b, ib = _take_top(s2, PEER_TOPK)
    b = jnp.concatenate(b, axis=0)
    ib = jnp.concatenate(ib, axis=0)
    counts = [PEER_TOPK // (i + 1) for i in range(PEER_TOPK)]
    cand = jnp.concatenate([a[i] + b[:c, :] for i, c in enumerate(counts)], axis=0)
    code = jnp.concatenate([ia[i] * float(n_keys) + ib[:c, :] for i, c in enumerate(counts)], axis=0)
    best, ids = _take_top(cand, PEER_TOPK, payload=code)
    ex = [jnp.exp(c - best[0]) for c in best]
    z = sum(ex)
    code_ref[...] = jnp.concatenate(ids, axis=0)
    g_ref[...] = jnp.concatenate(ex, axis=0) / z


def _peer_pairs(scores, tn):
    nhc, n_keys, n = scores.shape
    nh = nhc // 2
    blk = pl.BlockSpec((PEER_TOPK, tn), lambda t, h: (h, t))
    shape = jax.ShapeDtypeStruct((nh * PEER_TOPK, n), F32)
    return pl.pallas_call(
        functools.partial(_peer_pairs_kernel, n_keys=n_keys),
        grid=(n // tn, nh),
        in_specs=[pl.BlockSpec((2, n_keys, tn), lambda t, h: (h, 0, t))],
        out_specs=(blk, blk),
        out_shape=(shape, shape),
        compiler_params=_params(("parallel", "arbitrary")),
        name="peer_pairs",
    )(scores)


def _peer_gates_kernel(code_ref, g_ref, o_ref, i1t_ref, i2t_ref, gt_ref, *, n_keys, ib):
    code = code_ref[...].T
    i1 = jnp.floor(code * (1.0 / n_keys))
    i1t_ref[...] = i1
    i2t_ref[...] = code - i1 * float(n_keys)
    gt_ref[...] = g_ref[...].T
    tg, n_pairs = i1t_ref.shape
    key = lax.broadcasted_iota(jnp.int32, (n_keys, n_pairs), 0).astype(F32)

    def body(n, carry):
        i1r = i1t_ref[pl.ds(n, 1), :]
        i2r = i2t_ref[pl.ds(n, 1), :]
        gr = gt_ref[pl.ds(n, 1), :]
        a_t = jnp.where(key == i1r, gr, 0.0).astype(BF16)
        b_t = jnp.where(key == i2r, 1.0, 0.0).astype(BF16)
        grid_n = lax.dot_general(a_t, b_t, _NT, preferred_element_type=F32)
        for blk in range(o_ref.shape[0]):
            o_ref[blk, n] = grid_n[blk * ib:(blk + 1) * ib, :]
        return carry

    lax.fori_loop(0, tg, body, 0, unroll=32)


def _peer_gates(code, g, n_keys, tg, ib):
    n_pairs, n = code.shape
    blk = pl.BlockSpec((n_pairs, tg), lambda t: (0, t))
    body = functools.partial(_peer_gates_kernel, n_keys=n_keys, ib=ib)
    return pl.pallas_call(
        body,
        grid=(n // tg,),
        in_specs=[blk, blk],
        out_specs=pl.BlockSpec((n_keys // ib, tg, ib, n_keys), lambda t: (0, t, 0, 0)),
        out_shape=jax.ShapeDtypeStruct((n_keys // ib, n, ib, n_keys), F32),
        scratch_shapes=[pltpu.VMEM((tg, n_pairs), F32)] * 3,
        compiler_params=_params(("parallel",)),
        name="peer_gates",
    )(code, g)


def _peer_experts_kernel(h_ref, x1_ref, gate_ref, u_ref, v_ref, o_ref, *, ib, n_keys):
    @pl.when(pl.program_id(1) == 0)
    def _():
        o_ref[...] = x1_ref[...]

    act = _gelu(lax.dot_general(h_ref[...], u_ref[...], _NT, preferred_element_type=F32))
    gates = jnp.concatenate([gate_ref[0, :, i, :] for i in range(ib)], axis=1)
    o_ref[...] += jnp.dot((gates * act).astype(BF16), v_ref[...], preferred_element_type=F32)


def _peer_experts(h2, x1, gates, u_bf, v_bf, tn, ib):
    n, d = h2.shape
    n_keys = gates.shape[3]
    ec = ib * n_keys
    body = functools.partial(_peer_experts_kernel, ib=ib, n_keys=n_keys)
    return pl.pallas_call(
        body,
        grid=(n // tn, v_bf.shape[0] // ec),
        in_specs=[
            pl.BlockSpec((tn, d), lambda t, e: (t, 0)),
            pl.BlockSpec((tn, d), lambda t, e: (t, 0), pipeline_mode=pl.Buffered(1)),
            pl.BlockSpec((1, tn, ib, n_keys), lambda t, e: (e, t, 0, 0)),
            pl.BlockSpec((ec, d), lambda t, e: (e, 0)),
            pl.BlockSpec((ec, d), lambda t, e: (e, 0)),
        ],
        out_specs=pl.BlockSpec((tn, d), lambda t, e: (t, 0)),
        out_shape=jax.ShapeDtypeStruct((n, d), F32),
        compiler_params=_params(("parallel", "arbitrary")),
        name="peer_experts",
    )(h2, x1, gates, u_bf, v_bf)


def _tile(n, pref):
    return pref if n % pref == 0 else n


def _layer(x, attend, shift0, s0, t_real, lam_init, lw):
    b, t, d = x.shape
    n = b * t
    xf = x.reshape(n, d)
    w_da = lw["w_q"].shape[1]
    tm = _tile(n, 512)
    h1 = _rms(xf, lw["norm1_g"], tm)
    tmp = _tile(n, 1024)
    qb = _proj("q", h1, lw["w_q"], tmp, _tile(w_da, 512), gain=lw["q_gain"], group=lw["dk"], scale=lw["dk"] ** -0.5)
    if t % LANES == 0:
        seqs, tmk = b, _tile(t, 1024)
    else:
        seqs, tmk = 1, n
    k, kbt = _proj_keys(h1, lw["w_k"], lw["k_gain"], lw["dk"], seqs, tmk, _tile(w_da, 512))
    v, vb = _proj("dual", h1, lw["w_v"], tmp, _tile(w_da, 512))
    p_pad = lw["w_rw"].shape[1]
    rw = _proj("plain", h1, lw["w_rw"], tmp, p_pad // 3)

    o_da = attend(qb.reshape(b, t, w_da), kbt, vb.reshape(b, t, w_da))

    rw3 = rw.reshape(b, t, p_pad)
    if t % RW_CHUNK:
        rw3 = jnp.pad(rw3, ((0, 0), (0, RW_CHUNK - t % RW_CHUNK), (0, 0)))
    o_rw, s_out = _rwkv(rw3, shift0, s0, lw["rwkv"], t_real)
    o_rw = o_rw[:, :t].reshape(n, -1)

    x1, h2, h2t = _outproj(xf, o_da.reshape(n, w_da), o_rw, lw["w_out_da"], lw["w_out_rw"], lw["norm2_g"],
                           _tile(n, 256))
    tn = _tile(n, 512)
    scores = _peer_scores(h2t, lw["wq_t"], lw["peer_keys"], tn)
    code, gate = _peer_pairs(scores, tn)
    ib = 8
    gates = _peer_gates(code, gate, scores.shape[1], _tile(n, 128), ib)
    out = _peer_experts(h2, x1, gates, lw["peer_u"], lw["peer_v"], tn, ib)
    return out.reshape(b, t, d), k, v, s_out, rw3[:, t - 1, :lw["rw_proj"]]


def kernel(x_prompt, x_sample, cache_k, cache_v, state_wkv, state_shift, page_table, norm1_g, w_in, da_qk_g,
           da_lambda, da_subln_g, rw_mu, rw_w0, rw_w2, rw_a0, rw_a2, rw_g2, rw_kk, rw_ka, rw_rk, rw_lnx,
           w_out, norm2_g, peer_wq, peer_keys, peer_u, peer_v):
    depth = w_in.shape[0]
    d = x_prompt.shape[-1]
    heads, dk = cache_k.shape[3], cache_k.shape[5]
    w_da = heads * 2 * dk
    rw_heads, head = state_wkv.shape[2], state_wkv.shape[3]
    wd = rw_heads * head
    r_w, r_a, r_g = rw_w2.shape[1], rw_a2.shape[1], rw_g2.shape[1]
    rw_proj = 3 * wd + r_w + r_a + r_g
    g_pad = 2 * LANES
    p_pad = 3 * wd + LANES + g_pad
    assert d == w_da + wd and head * 2 == LANES and wd % LANES == 0 and 2 * dk == LANES
    assert r_w + r_a == LANES and r_g <= g_pad and p_pad % 3 == 0 and (p_pad // 3) % LANES == 0
    assert cache_k.shape[2] == PAGE_SIZE
    nh, _, n_keys, qd = peer_keys.shape[1:]
    bsz = x_prompt.shape[0]
    dec_b, dec_t = x_sample.shape[:2]
    assert dec_t * heads <= PAGE_SIZE and x_prompt.shape[1] % LANES == 0
    assert n_keys & (n_keys - 1) == 0 and peer_u.shape[1] == n_keys * n_keys

    xp, xs = x_prompt, x_sample
    outs = [[] for _ in range(8)]
    for l in range(depth):
        lam_init = 0.8 - 0.6 * math.exp(-0.3 * l)
        wl = w_in[l].astype(BF16)
        w_rw = jnp.pad(wl[:, 3 * w_da:], ((0, 0), (0, p_pad - rw_proj)))
        zero_wa = jnp.zeros((r_w, wd), F32)
        wwa = jnp.concatenate([jnp.concatenate([rw_w2[l], zero_wa], axis=1),
                               jnp.concatenate([zero_wa, rw_a2[l]], axis=1)], axis=0).astype(BF16)
        row = lambda z: z.reshape(1, -1)
        rwkv_prm = dict(
            mu=jnp.pad(row(rw_mu[l]), ((0, 0), (0, p_pad - rw_proj))),
            w0=row(rw_w0[l]), a0=row(rw_a0[l]), wwa=wwa,
            g2=jnp.pad(rw_g2[l], ((0, g_pad - r_g), (0, 0))).astype(BF16),
            kk=row(rw_kk[l]), ka=row(rw_ka[l]), rk=row(rw_rk[l]),
            lng=row(rw_lnx[l, 0]), lnb=row(rw_lnx[l, 1]),
            tril=jnp.tril(jnp.ones((RW_CHUNK, RW_CHUNK), BF16)),
            bd=_block_diag_ones(LANES, head), r_w=r_w, head=head)
        lw = dict(
            norm1_g=norm1_g[l], w_q=wl[:, :w_da], w_k=wl[:, w_da:2 * w_da], w_v=wl[:, 2 * w_da:3 * w_da],
            w_rw=w_rw, q_gain=jnp.tile(da_qk_g[l, 0], 2 * heads), k_gain=jnp.tile(da_qk_g[l, 1], 2 * heads),
            dk=dk, rwkv=rwkv_prm, rw_proj=rw_proj,
            w_out_da=w_out[l, :w_da].astype(BF16), w_out_rw=w_out[l, w_da:].astype(BF16),
            norm2_g=norm2_g[l], wq_t=peer_wq[l].T.astype(BF16),
            peer_keys=peer_keys[l].reshape(nh * 2, n_keys, qd).astype(BF16),
            peer_u=peer_u[l].astype(BF16), peer_v=peer_v[l].astype(BF16))

        attend_p = functools.partial(_attn_prompt, da_lambda=da_lambda[l], subln_g=da_subln_g[l], heads=heads,
                                     lam_init=lam_init, tq=_tile(x_prompt.shape[1], 512))
        n_pairs = wd // LANES
        xp, kp, vp, sp, shp = _layer(
            xp, attend_p, jnp.zeros((bsz, 1, p_pad), F32), jnp.zeros((bsz, n_pairs, LANES, LANES), F32),
            RW_CHUNK, lam_init, lw)

        attend_s = functools.partial(_attn_sample, cache_k=cache_k, cache_v=cache_v, layer=l,
                                     page_table=page_table, da_lambda=da_lambda[l], subln_g=da_subln_g[l],
                                     heads=heads, lam_init=lam_init,
                                     pp=8 if page_table.shape[1] % 8 == 0 else 4)
        shift_s = jnp.pad(state_shift[l], ((0, 0), (0, p_pad - rw_proj))).reshape(dec_b, 1, p_pad)
        xs, ksm, vsm, ssm, shs = _layer(xs, attend_s, shift_s, _pack_state(state_wkv[l], head), dec_t, lam_init, lw)

        outs[0].append(jnp.transpose(kp.reshape(bsz, heads, 2, dk, -1), (0, 4, 1, 2, 3)))
        outs[1].append(vp.reshape(bsz, -1, heads, 2 * dk))
        outs[2].append(_unpack_state(sp, head))
        outs[3].append(shp)
        outs[4].append(jnp.transpose(ksm.reshape(heads, 2, dk, dec_b, dec_t), (3, 4, 0, 1, 2)))
        outs[5].append(vsm.reshape(dec_b, dec_t, heads, 2 * dk))
        outs[6].append(_unpack_state(ssm, head))
        outs[7].append(shs)
    return (xp, xs) + tuple(jnp.stack(o) for o in outs)
```

```python
import functools
import math

import jax
import jax.numpy as jnp
from jax import lax
from jax.experimental import pallas as pl
from jax.experimental.pallas import tpu as pltpu

F32 = jnp.float32
BF16 = jnp.bfloat16

LANES = 128
RMS_EPS = 1e-6
LNX_EPS = 64e-5
PEER_TOPK = 16
PAGE_SIZE = 128
RW_CHUNK = 64
VMEM_LIMIT = 48 * 1024 * 1024

_NT = (((1,), (1,)), ((), ()))
_TN = (((0,), (0,)), ((), ()))


def _params(sem):
    return pltpu.CompilerParams(dimension_semantics=sem, vmem_limit_bytes=VMEM_LIMIT)


def _mm(a, b):
    return jnp.dot(a.astype(BF16), b.astype(BF16), preferred_element_type=F32)


def _mm_nt(a, b):
    return lax.dot_general(a.astype(BF16), b.astype(BF16), _NT, preferred_element_type=F32)


def _mm_tn(a, b):
    return lax.dot_general(a.astype(BF16), b.astype(BF16), _TN, preferred_element_type=F32)


def _mm_split(a_exact_bf16, b):
    hi = b.astype(BF16)
    lo = (b - hi.astype(F32)).astype(BF16)
    return (jnp.dot(a_exact_bf16, hi, preferred_element_type=F32)
            + jnp.dot(a_exact_bf16, lo, preferred_element_type=F32))


def _mm_split_lhs(a, b_exact_bf16):
    hi = a.astype(BF16)
    lo = (a - hi.astype(F32)).astype(BF16)
    return (jnp.dot(hi, b_exact_bf16, preferred_element_type=F32)
            + jnp.dot(lo, b_exact_bf16, preferred_element_type=F32))


def _block_diag_ones(n, group):
    i = jnp.arange(n)
    return (i[:, None] // group == i[None, :] // group).astype(BF16)


def _rms_kernel(x_ref, g_ref, o_ref):
    x = x_ref[...]
    ms = jnp.mean(x * x, axis=-1, keepdims=True)
    o_ref[...] = (x * lax.rsqrt(ms + RMS_EPS) * g_ref[...]).astype(o_ref.dtype)


def _rms(x, g, tm):
    n, d = x.shape
    return pl.pallas_call(
        _rms_kernel,
        grid=(n // tm,),
        in_specs=[pl.BlockSpec((tm, d), lambda i: (i, 0)), pl.BlockSpec((1, d), lambda i: (0, 0))],
        out_specs=pl.BlockSpec((tm, d), lambda i: (i, 0)),
        out_shape=jax.ShapeDtypeStruct((n, d), BF16),
        compiler_params=_params(("parallel",)),
        name="rms",
    )(x, g.reshape(1, d))


def _proj_plain_kernel(h_ref, w_ref, o_ref):
    o_ref[...] = jnp.dot(h_ref[...], w_ref[...], preferred_element_type=F32)


def _proj_dual_kernel(h_ref, w_ref, o_ref, ob_ref):
    y = jnp.dot(h_ref[...], w_ref[...], preferred_element_type=F32)
    o_ref[...] = y
    ob_ref[...] = y.astype(BF16)


def _qk_normed(h_ref, w_ref, bd_ref, g_ref, group):
    y = jnp.dot(h_ref[...], w_ref[...], preferred_element_type=F32)
    yy = y * y
    bd = bd_ref[...]
    sums = [_mm_split_lhs(yy[:, c * LANES:(c + 1) * LANES], bd) for c in range(y.shape[1] // LANES)]
    ms = jnp.concatenate(sums, axis=1) * (1.0 / group)
    return y * lax.rsqrt(ms + RMS_EPS) * g_ref[...]


def _proj_q_kernel(h_ref, w_ref, bd_ref, g_ref, ob_ref, *, group, scale):
    ob_ref[...] = (_qk_normed(h_ref, w_ref, bd_ref, g_ref, group) * scale).astype(BF16)


def _proj_k_kernel(h_ref, w_ref, bd_ref, g_ref, o_ref, ob_ref, *, group):
    yt = _qk_normed(h_ref, w_ref, bd_ref, g_ref, group).T
    o_ref[0] = yt
    ob_ref[0] = yt.astype(BF16)


def _proj_keys(h, w, gain, group, seqs, tm, tn):
    n, d = h.shape
    ncol = w.shape[1]
    s = n // seqs
    tiles = s // tm
    o_spec = pl.BlockSpec((1, tn, tm), lambda i, j: (i // tiles, j, i % tiles))
    return pl.pallas_call(
        functools.partial(_proj_k_kernel, group=group),
        grid=(n // tm, ncol // tn),
        in_specs=[pl.BlockSpec((tm, d), lambda i, j: (i, 0)), pl.BlockSpec((d, tn), lambda i, j: (0, j)),
                  pl.BlockSpec((LANES, LANES), lambda i, j: (0, 0)), pl.BlockSpec((1, tn), lambda i, j: (0, j))],
        out_specs=(o_spec, o_spec),
        out_shape=(jax.ShapeDtypeStruct((seqs, ncol, s), F32), jax.ShapeDtypeStruct((seqs, ncol, s), BF16)),
        compiler_params=_params(("parallel", "arbitrary")),
        name="proj_k",
    )(h, w, _block_diag_ones(LANES, group), gain.reshape(1, ncol))


def _proj(kind, h, w, tm, tn, gain=None, group=None, scale=None):
    n, d = h.shape
    ncol = w.shape[1]
    grid = (n // tm, ncol // tn)
    h_spec = pl.BlockSpec((tm, d), lambda i, j: (i, 0))
    w_spec = pl.BlockSpec((d, tn), lambda i, j: (0, j))
    o_spec = pl.BlockSpec((tm, tn), lambda i, j: (i, j))
    f32_out = jax.ShapeDtypeStruct((n, ncol), F32)
    bf_out = jax.ShapeDtypeStruct((n, ncol), BF16)
    if kind == "q":
        bd = _block_diag_ones(LANES, group)
        extra = [bd, gain.reshape(1, ncol)]
        extra_specs = [pl.BlockSpec((LANES, LANES), lambda i, j: (0, 0)), pl.BlockSpec((1, tn), lambda i, j: (0, j))]
    else:
        extra, extra_specs = [], []
    if kind == "q":
        body = functools.partial(_proj_q_kernel, group=group, scale=scale)
        out_shape, out_specs = bf_out, o_spec
    elif kind == "dual":
        body = _proj_dual_kernel
        out_shape, out_specs = (f32_out, bf_out), (o_spec, o_spec)
    else:
        body = _proj_plain_kernel
        out_shape, out_specs = f32_out, o_spec
    return pl.pallas_call(
        body,
        grid=grid,
        in_specs=[h_spec, w_spec] + extra_specs,
        out_specs=out_specs,
        out_shape=out_shape,
        compiler_params=_params(("parallel", "arbitrary")),
        name="proj_" + kind,
    )(h, w, *extra)


def _lambda(lam_ref, lam_init):
    ll = lam_ref[...]
    s01 = jnp.sum(ll[0:1, :] * ll[1:2, :], axis=-1, keepdims=True)
    s23 = jnp.sum(ll[2:3, :] * ll[3:4, :], axis=-1, keepdims=True)
    return jnp.exp(s01) - jnp.exp(s23) + lam_init


def _subln(o, g, lam_init):
    ms = jnp.mean(o * o, axis=-1, keepdims=True)
    return (o * lax.rsqrt(ms + RMS_EPS) * g) * (1.0 - lam_init)


def _attn_prompt_kernel(q_ref, k_ref, v_ref, lam_ref, g_ref, o_ref, acc0_ref, acc1_ref, *, tq, dk, lam_init):
    qi = pl.program_id(2)
    q = q_ref[0]
    lane = lax.broadcasted_iota(jnp.int32, q.shape, 1)
    qf = q.astype(F32)
    qc = (jnp.where(lane < dk, qf, 0.0).astype(BF16), jnp.where(lane >= dk, qf, 0.0).astype(BF16))
    accs = (acc0_ref, acc1_ref)
    acc0_ref[...] = jnp.zeros_like(acc0_ref)
    acc1_ref[...] = jnp.zeros_like(acc1_ref)
    causal = (lax.broadcasted_iota(jnp.int32, (tq, tq), 1) <= lax.broadcasted_iota(jnp.int32, (tq, tq), 0))

    def step(j, carry, diagonal):
        kt = k_ref[0, :, pl.ds(pl.multiple_of(j * tq, tq), tq)]
        vb = v_ref[0, pl.ds(pl.multiple_of(j * tq, tq), tq), :]
        new = []
        for c in range(2):
            m, l = carry[2 * c], carry[2 * c + 1]
            s = jnp.dot(qc[c], kt, preferred_element_type=F32)
            if diagonal:
                s = jnp.where(causal, s, -jnp.inf)
            m_new = jnp.maximum(m, jnp.max(s, axis=-1, keepdims=True))
            corr = jnp.exp(m - m_new)
            p = jnp.exp(s - m_new)
            l = l * corr + jnp.sum(p, axis=-1, keepdims=True)
            accs[c][...] = accs[c][...] * corr + jnp.dot(p.astype(BF16), vb, preferred_element_type=F32)
            new += [m_new, l]
        return tuple(new)

    minit = jnp.full((tq, 1), -jnp.inf, F32)
    linit = jnp.zeros((tq, 1), F32)
    carry = lax.fori_loop(0, qi, functools.partial(step, diagonal=False), (minit, linit, minit, linit))
    _, l0, _, l1 = step(qi, carry, True)
    lam = _lambda(lam_ref, lam_init)
    o = acc0_ref[...] / l0 - lam * (acc1_ref[...] / l1)
    o_ref[0] = _subln(o, g_ref[...], lam_init).astype(o_ref.dtype)


def _attn_prompt(qb, kbt, vb, da_lambda, subln_g, heads, lam_init, tq):
    b, s, w = qb.shape
    dv = w // heads
    body = functools.partial(_attn_prompt_kernel, tq=tq, dk=dv // 2, lam_init=lam_init)
    return pl.pallas_call(
        body,
        grid=(b, heads, s // tq),
        in_specs=[
            pl.BlockSpec((1, tq, dv), lambda bi, h, qi: (bi, qi, h)),
            pl.BlockSpec((1, dv, s), lambda bi, h, qi: (bi, h, 0)),
            pl.BlockSpec((1, s, dv), lambda bi, h, qi: (bi, 0, h)),
            pl.BlockSpec(da_lambda.shape, lambda bi, h, qi: (0, 0)),
            pl.BlockSpec((1, dv), lambda bi, h, qi: (0, 0)),
        ],
        out_specs=pl.BlockSpec((1, tq, dv), lambda bi, h, qi: (bi, qi, h)),
        out_shape=jax.ShapeDtypeStruct((b, s, w), BF16),
        scratch_shapes=[pltpu.VMEM((tq, dv), F32), pltpu.VMEM((tq, dv), F32)],
        compiler_params=_params(("parallel", "parallel", "arbitrary")),
        name="attn_prompt",
    )(qb, kbt, vb, da_lambda, subln_g.reshape(1, dv))


def _attn_sample_kernel(pt_ref, qrep_ref, kst_ref, vs_ref, lam_ref, g_ref, *rest, pp, heads, t, dk, lam_init):
    kc_refs = rest[:pp]
    vc_refs = rest[pp:2 * pp]
    o_ref = rest[2 * pp]
    qbd_ref, m_ref, l_ref, acc_ref = rest[2 * pp + 1:]
    del pt_ref
    p_idx = pl.program_id(1)
    rows = 2 * t * heads
    dv = 2 * dk
    w = heads * dv

    @pl.when(p_idx == 0)
    def _():
        r = lax.broadcasted_iota(jnp.int32, (rows, w), 0)
        col = lax.broadcasted_iota(jnp.int32, (rows, w), 1)
        keep = (col // dk) == ((r % heads) * 2 + r // (t * heads))
        qbd_ref[...] = jnp.where(keep, qrep_ref[0].astype(F32), 0.0).astype(BF16)
        m_ref[...] = jnp.full(m_ref.shape, -jnp.inf, F32)
        l_ref[...] = jnp.zeros(l_ref.shape, F32)
        acc_ref[...] = jnp.zeros(acc_ref.shape, F32)

    qbd = qbd_ref[...]

    def online(s, vbs):
        m = m_ref[...]
        m_new = jnp.maximum(m, jnp.max(s, axis=-1, keepdims=True))
        corr = jnp.exp(m - m_new)
        p = jnp.exp(s - m_new)
        l_ref[...] = l_ref[...] * corr + jnp.sum(p, axis=-1, keepdims=True)
        pv = None
        for i, vb in enumerate(vbs):
            n = vb.shape[0]
            d = jnp.dot(p[:, i * n:(i + 1) * n].astype(BF16), vb, preferred_element_type=F32)
            pv = d if pv is None else pv + d
        acc_ref[...] = acc_ref[...] * corr + pv
        m_ref[...] = m_new

    kt = jnp.concatenate([kc[0] for kc in kc_refs], axis=1).astype(BF16)
    vwide = [jnp.concatenate([vc[0, pl.ds(h, PAGE_SIZE, stride=heads), :] for h in range(heads)], axis=1)
             .astype(BF16) for vc in vc_refs]
    online(jnp.dot(qbd, kt, preferred_element_type=F32), vwide)

    @pl.when(p_idx == pl.num_programs(1) - 1)
    def _():
        s = jnp.dot(qbd, kst_ref[0], preferred_element_type=F32)
        r = lax.broadcasted_iota(jnp.int32, s.shape, 0)
        col = lax.broadcasted_iota(jnp.int32, s.shape, 1)
        online(jnp.where(col <= (r // heads) % t, s, -jnp.inf), [vs_ref[0]])
        out = acc_ref[...] / l_ref[...]
        hrow = lax.broadcasted_iota(jnp.int32, (heads, w), 0)
        hcol = lax.broadcasted_iota(jnp.int32, (heads, w), 1) // dv
        pick = hrow == hcol
        lam = _lambda(lam_ref, lam_init)
        toks = []
        for tok_i in range(t):
            maps = []
            for c in range(2):
                base = (c * t + tok_i) * heads
                maps.append(jnp.sum(jnp.where(pick, out[base:base + heads, :], 0.0), axis=0, keepdims=True))
            toks.append(maps[0] - lam * maps[1])
        o = jnp.concatenate(toks, axis=0)
        g = g_ref[...]
        outs = [_subln(o[:, h * dv:(h + 1) * dv], g, lam_init) for h in range(heads)]
        o_ref[0] = jnp.concatenate(outs, axis=1).astype(o_ref.dtype)


def _attn_sample(qb, kbt, vb, cache_k, cache_v, layer, page_table, da_lambda, subln_g, heads, lam_init, pp):
    bd, t, w = qb.shape
    kb = jnp.transpose(kbt.reshape(w, bd, t), (1, 2, 0))
    dv = w // heads
    n_pool = cache_k.shape[1]
    n_pages = page_table.shape[1]
    rows = 2 * t * heads
    qrep = jnp.tile(jnp.repeat(qb, heads, axis=1), (1, 2, 1))
    kst_pad = jnp.pad(jnp.swapaxes(kb, 1, 2), ((0, 0), (0, 0), (0, PAGE_SIZE - t)))
    vs_pad = jnp.pad(vb, ((0, 0), (0, PAGE_SIZE - t), (0, 0)))
    ckt = jnp.transpose(cache_k, (0, 1, 3, 4, 5, 2)).reshape(-1, w, PAGE_SIZE)
    cv = cache_v.reshape(-1, PAGE_SIZE * heads, dv)
    base = layer * n_pool

    def k_spec(i):
        return pl.BlockSpec((1, w, PAGE_SIZE), lambda b, p, pt: (base + pt[b, p * pp + i], 0, 0))

    def v_spec(i):
        return pl.BlockSpec((1, PAGE_SIZE * heads, dv), lambda b, p, pt: (base + pt[b, p * pp + i], 0, 0))

    body = functools.partial(_attn_sample_kernel, pp=pp, heads=heads, t=t, dk=dv // 2, lam_init=lam_init)
    grid_spec = pltpu.PrefetchScalarGridSpec(
        num_scalar_prefetch=1,
        grid=(bd, n_pages // pp),
        in_specs=[
            pl.BlockSpec((1, rows, w), lambda b, p, pt: (b, 0, 0)),
            pl.BlockSpec((1, w, PAGE_SIZE), lambda b, p, pt: (b, 0, 0)),
            pl.BlockSpec((1, PAGE_SIZE, w), lambda b, p, pt: (b, 0, 0)),
            pl.BlockSpec(da_lambda.shape, lambda b, p, pt: (0, 0)),
            pl.BlockSpec((1, dv), lambda b, p, pt: (0, 0)),
        ] + [k_spec(i) for i in range(pp)] + [v_spec(i) for i in range(pp)],
        out_specs=pl.BlockSpec((1, t, w), lambda b, p, pt: (b, 0, 0)),
        scratch_shapes=[pltpu.VMEM((rows, w), BF16), pltpu.VMEM((rows, 1), F32),
                        pltpu.VMEM((rows, 1), F32), pltpu.VMEM((rows, w), F32)],
    )
    return pl.pallas_call(
        body,
        grid_spec=grid_spec,
        out_shape=jax.ShapeDtypeStruct((bd, t, w), BF16),
        compiler_params=_params(("parallel", "arbitrary")),
        name="attn_sample",
    )(page_table, qrep, kst_pad, vs_pad, da_lambda, subln_g.reshape(1, dv), *([ckt] * pp), *([cv] * pp))


def _softplus(z):
    return jnp.maximum(z, 0.0) + jnp.log1p(jnp.exp(-jnp.abs(z)))


def _sigmoid(z):
    return 1.0 / (1.0 + jnp.exp(-z))


def _split_heads(x, first):
    zero = jnp.zeros_like(x)
    return jnp.concatenate([jnp.where(first, x, zero), jnp.where(first, zero, x)], axis=0)


def _rwkv_kernel(rw_ref, shift_ref, s0_ref, mu_ref, w0_ref, a0_ref, wwa_ref, g2_ref, kk_ref, ka_ref,
                 rk_ref, lng_ref, lnb_ref, tril_ref, bd_ref, o_ref, sout_ref, state_ref, prev_ref,
                 *, wd, r_w, t_real, head):
    c_idx = pl.program_id(1)
    cl = RW_CHUNK
    pairs = range(wd // LANES)

    @pl.when(c_idx == 0)
    def _():
        state_ref[...] = s0_ref[0]
        prev_ref[...] = shift_ref[0]

    x = rw_ref[0]
    rowp = lax.broadcasted_iota(jnp.int32, x.shape, 0)
    prevx = jnp.where(rowp == 0, prev_ref[...], pltpu.roll(x, 1, axis=0))
    prev_ref[...] = x[cl - 1:cl, :]
    xm = x + (prevx - x) * mu_ref[...]
    r = xm[:, 0:wd]
    kr = xm[:, wd:2 * wd]
    vr = xm[:, 2 * wd:3 * wd]
    slab = xm[:, 3 * wd:3 * wd + LANES]
    lane_s = lax.broadcasted_iota(jnp.int32, slab.shape, 1)
    wa = _mm(jnp.where(lane_s < r_w, jnp.tanh(slab), slab), wwa_ref[...])
    gd = xm[:, 3 * wd + LANES:]
    g = _mm(_sigmoid(gd), g2_ref[...])
    w = -_softplus(-(w0_ref[...] + wa[:, :wd])) - 0.5
    lw = -jnp.exp(w)
    a = _sigmoid(a0_ref[...] + wa[:, wd:])
    kk = kr * kk_ref[...]
    kh = kr * (1.0 + (a - 1.0) * ka_ref[...])
    beta_scale = a
    if t_real < cl:
        valid = lax.broadcasted_iota(jnp.int32, (cl, wd), 0) < t_real
        lw = jnp.where(valid, lw, 0.0)
        beta_scale = jnp.where(valid, a, 0.0)
        kh_state = jnp.where(valid, kh, 0.0)
    else:
        kh_state = kh
    cum = _mm_split(tril_ref[...], lw)
    cum_last = cum[cl - 1:cl, :]
    e_neg = jnp.exp(-cum)
    e_end = jnp.exp(cum_last - cum)
    w_end = jnp.exp(cum_last)
    rt_all = r * jnp.exp(cum)
    ex_all = jnp.exp(cum - lw)
    bonus_in = r * kh * rk_ref[...]

    lane = lax.broadcasted_iota(jnp.int32, (cl, LANES), 1)
    first = lane < head
    tt = lax.broadcasted_iota(jnp.int32, (cl, 2 * cl), 0)
    ss = lax.broadcasted_iota(jnp.int32, (cl, 2 * cl), 1) % cl
    strict = ss < tt
    incl = ss <= tt
    firstw = lax.broadcasted_iota(jnp.int32, (cl, 2 * cl), 1) < cl
    r2 = lax.broadcasted_iota(jnp.int32, (2 * cl, 2 * cl), 0)
    c2 = lax.broadcasted_iota(jnp.int32, (2 * cl, 2 * cl), 1)
    eye = (r2 == c2).astype(F32)
    rs = lax.broadcasted_iota(jnp.int32, (LANES, LANES), 0) // head
    cs = lax.broadcasted_iota(jnp.int32, (LANES, LANES), 1) // head
    same_head = rs == cs
    bd = bd_ref[...]
    n_levels = int(math.log2(cl))
    sls = [slice(p * LANES, (p + 1) * LANES) for p in pairs]

    kk_p = [kk[:, sl] for sl in sls]
    nrm = [jnp.sqrt(_mm(k * k, bd)) for k in kk_p]
    kkn = [k / jnp.maximum(n, 1e-12) for k, n in zip(kk_p, nrm)]
    beta = [kkn[p] * beta_scale[:, sls[p]] for p in pairs]
    v_p = [vr[:, sl] for sl in sls]
    st = [state_ref[p] for p in pairs]
    lhs = [jnp.concatenate([-ex_all[:, sls[p]] * kkn[p], rt_all[:, sls[p]]], axis=0) for p in pairs]
    rhs = [jnp.concatenate([_split_heads(beta[p] * e_neg[:, sls[p]], first),
                            _split_heads(kh_state[:, sls[p]] * e_neg[:, sls[p]], first)], axis=0) for p in pairs]
    m = [_mm_nt(lhs[p], rhs[p]) for p in pairs]
    sa = [_mm_nt(lhs[p], st[p]) for p in pairs]
    aab = [jnp.where(strict, x_[:cl, :2 * cl], 0.0) for x_ in m]
    aak = [jnp.where(strict, x_[:cl, 2 * cl:], 0.0) for x_ in m]
    arbk = [jnp.concatenate([jnp.where(incl, x_[cl:, :2 * cl], 0.0), jnp.where(incl, x_[cl:, 2 * cl:], 0.0)], axis=1)
            for x_ in m]
    abd = [_split_heads(x_, firstw) for x_ in aab]
    vbd = [_split_heads(x_, first) for x_ in v_p]
    rhs_u = [sa[p][:cl, :] + _mm(aak[p], vbd[p]) for p in pairs]
    tinv = [eye + x_ for x_ in abd]
    xp = [_mm(x_, x_) for x_ in abd]
    for lvl in range(1, n_levels):
        if lvl < n_levels - 1:
            both = [_mm(xp[p], jnp.concatenate([tinv[p], xp[p]], axis=1)) for p in pairs]
            tinv = [tinv[p] + both[p][:, :2 * cl] for p in pairs]
            xp = [both[p][:, 2 * cl:] for p in pairs]
        else:
            tinv = [tinv[p] + _mm(xp[p], tinv[p]) for p in pairs]
    tw = [x_[:cl, :] + x_[cl:, :] for x_ in tinv]
    u = [_mm(tw[p], _split_heads(rhs_u[p], first)) for p in pairs]
    y = [sa[p][cl:, :] + _mm(arbk[p], jnp.concatenate([_split_heads(u[p], first), vbd[p]], axis=0)) for p in pairs]
    ds = [_mm_tn(jnp.concatenate([u[p], v_p[p]], axis=0),
                 jnp.concatenate([beta[p] * e_end[:, sls[p]], kh_state[:, sls[p]] * e_end[:, sls[p]]], axis=0))
          for p in pairs]
    mean = [_mm(x_, bd) * (1.0 / head) for x_ in y]
    dlt = [y[p] - mean[p] for p in pairs]
    var = [_mm(x_ * x_, bd) * (1.0 / head) for x_ in dlt]
    bonus = [_mm(bonus_in[:, sls[p]], bd) * v_p[p] for p in pairs]
    for p in pairs:
        sl = sls[p]
        state_ref[p] = st[p] * w_end[:, sl] + jnp.where(same_head, ds[p], 0.0)
        yn = dlt[p] * lax.rsqrt(var[p] + LNX_EPS) * lng_ref[:, sl] + lnb_ref[:, sl]
        o_ref[0, :, sl] = ((yn + bonus[p]) * g[:, sl]).astype(o_ref.dtype)

    @pl.when(c_idx == pl.num_programs(1) - 1)
    def _():
        sout_ref[0] = state_ref[...]


def _rwkv(rw, shift0, s0, prm, t_real):
    b, s, p = rw.shape
    wd = prm["w0"].shape[1]
    n_pairs = wd // LANES
    cl = RW_CHUNK
    body = functools.partial(_rwkv_kernel, wd=wd, r_w=prm["r_w"], t_real=t_real, head=prm["head"])

    def const(shape):
        return pl.BlockSpec(shape, lambda bi, ci: (0,) * len(shape))

    names = ["mu", "w0", "a0", "wwa", "g2", "kk", "ka", "rk", "lng", "lnb", "tril", "bd"]
    return pl.pallas_call(
        body,
        grid=(b, s // cl),
        in_specs=[
            pl.BlockSpec((1, cl, p), lambda bi, ci: (bi, ci, 0)),
            pl.BlockSpec((1, 1, p), lambda bi, ci: (bi, 0, 0)),
            pl.BlockSpec((1, n_pairs, LANES, LANES), lambda bi, ci: (bi, 0, 0, 0)),
        ] + [const(prm[nm].shape) for nm in names],
        out_specs=(pl.BlockSpec((1, cl, wd), lambda bi, ci: (bi, ci, 0)),
                   pl.BlockSpec((1, n_pairs, LANES, LANES), lambda bi, ci: (bi, 0, 0, 0))),
        out_shape=(jax.ShapeDtypeStruct((b, s, wd), BF16),
                   jax.ShapeDtypeStruct((b, n_pairs, LANES, LANES), F32)),
        scratch_shapes=[pltpu.VMEM((n_pairs, LANES, LANES), F32), pltpu.VMEM((1, p), F32)],
        compiler_params=_params(("parallel", "arbitrary")),
        name="rwkv",
    )(rw, shift0, s0, *[prm[nm] for nm in names])


def _pack_state(wkv, head):
    b, h = wkv.shape[:2]
    pairs = wkv.reshape(b, h // 2, 2, head, head)
    z = jnp.zeros((b, h // 2, head, head), wkv.dtype)
    top = jnp.concatenate([pairs[:, :, 0], z], axis=-1)
    bot = jnp.concatenate([z, pairs[:, :, 1]], axis=-1)
    return jnp.concatenate([top, bot], axis=-2)


def _unpack_state(sbd, head):
    b, n_pairs = sbd.shape[:2]
    h0 = sbd[:, :, :head, :head]
    h1 = sbd[:, :, head:, head:]
    return jnp.stack([h0, h1], axis=2).reshape(b, 2 * n_pairs, head, head)


def _outproj_kernel(x_ref, oda_ref, orw_ref, w1_ref, w2_ref, g_ref, x1_ref, h_ref, ht_ref):
    y = (jnp.dot(oda_ref[...], w1_ref[...], preferred_element_type=F32)
         + jnp.dot(orw_ref[...], w2_ref[...], preferred_element_type=F32))
    x1 = x_ref[...] + y
    x1_ref[...] = x1
    ms = jnp.mean(x1 * x1, axis=-1, keepdims=True)
    h2 = x1 * lax.rsqrt(ms + RMS_EPS) * g_ref[...]
    h_ref[...] = h2.astype(BF16)
    ht_ref[...] = h2.T.astype(BF16)


def _outproj(x, oda, orw, w1, w2, g, tm):
    n, d = x.shape
    wa, wr = oda.shape[1], orw.shape[1]
    return pl.pallas_call(
        _outproj_kernel,
        grid=(n // tm,),
        in_specs=[
            pl.BlockSpec((tm, d), lambda i: (i, 0)),
            pl.BlockSpec((tm, wa), lambda i: (i, 0)),
            pl.BlockSpec((tm, wr), lambda i: (i, 0)),
            pl.BlockSpec((wa, d), lambda i: (0, 0)),
            pl.BlockSpec((wr, d), lambda i: (0, 0)),
            pl.BlockSpec((1, d), lambda i: (0, 0)),
        ],
        out_specs=(pl.BlockSpec((tm, d), lambda i: (i, 0)), pl.BlockSpec((tm, d), lambda i: (i, 0)),
                   pl.BlockSpec((d, tm), lambda i: (0, i))),
        out_shape=(jax.ShapeDtypeStruct((n, d), F32), jax.ShapeDtypeStruct((n, d), BF16),
                   jax.ShapeDtypeStruct((d, n), BF16)),
        compiler_params=_params(("parallel",)),
        name="outproj",
    )(x, oda, orw, w1, w2, g.reshape(1, d))


def _peer_scores_kernel(ht_ref, wqt_ref, keys_ref, s_ref, q_ref):
    q_ref[...] = jnp.dot(wqt_ref[...], ht_ref[...], preferred_element_type=F32).astype(BF16)
    nhc, _, qd = keys_ref.shape
    for j in range(nhc):
        s_ref[j] = jnp.dot(keys_ref[j], q_ref[j * qd:(j + 1) * qd, :], preferred_element_type=F32)


def _peer_scores(ht, wqt, keys, tn):
    d, n = ht.shape
    nhc, n_keys, qd = keys.shape
    once = pl.Buffered(1)
    return pl.pallas_call(
        _peer_scores_kernel,
        grid=(n // tn,),
        in_specs=[
            pl.BlockSpec((d, tn), lambda t: (0, t)),
            pl.BlockSpec((nhc * qd, d), lambda t: (0, 0), pipeline_mode=once),
            pl.BlockSpec((nhc, n_keys, qd), lambda t: (0, 0, 0), pipeline_mode=once),
        ],
        out_specs=pl.BlockSpec((nhc, n_keys, tn), lambda t: (0, 0, t)),
        out_shape=jax.ShapeDtypeStruct((nhc, n_keys, n), F32),
        scratch_shapes=[pltpu.VMEM((nhc * qd, tn), BF16)],
        compiler_params=_params(("parallel",)),
        name="peer_scores",
    )(ht, wqt, keys)


def _gelu(x):
    return 0.5 * x * (1.0 + lax.erf(x * (2.0 ** -0.5)))


def _take_top(vals, k, payload=None):
    n_rows = vals.shape[0]
    rowid = lax.broadcasted_iota(jnp.int32, vals.shape, 0).astype(F32)
    cur = vals
    top, picked = [], []
    for _ in range(k):
        m = jnp.max(cur, axis=0, keepdims=True)
        sel = jnp.min(jnp.where(cur == m, rowid, float(n_rows)), axis=0, keepdims=True)
        hit = rowid == sel
        top.append(m)
        picked.append(sel if payload is None else jnp.sum(jnp.where(hit, payload, 0.0), axis=0, keepdims=True))
        cur = jnp.where(hit, -jnp.inf, cur)
    return top, picked


def _peer_pairs_kernel(s_ref, code_ref, g_ref, *, n_keys):
    s1 = s_ref[0]
    s2 = s_ref[1]
    a, ia = _take_top(s1, PEER_TOPK)
    b, ib = _take_top(s2, PEER_TOPK)
    b = jnp.concatenate(b, axis=0)
    ib = jnp.concatenate(ib, axis=0)
    counts = [PEER_TOPK // (i + 1) for i in range(PEER_TOPK)]
    cand = jnp.concatenate([a[i] + b[:c, :] for i, c in enumerate(counts)], axis=0)
    code = jnp.concatenate([ia[i] * float(n_keys) + ib[:c, :] for i, c in enumerate(counts)], axis=0)
    best, ids = _take_top(cand, PEER_TOPK, payload=code)
    ex = [jnp.exp(c - best[0]) for c in best]
    z = sum(ex)
    code_ref[...] = jnp.concatenate(ids, axis=0)
    g_ref[...] = jnp.concatenate(ex, axis=0) / z


def _peer_pairs(scores, tn):
    nhc, n_keys, n = scores.shape
    nh = nhc // 2
    blk = pl.BlockSpec((PEER_TOPK, tn), lambda t, h: (h, t))
    shape = jax.ShapeDtypeStruct((nh * PEER_TOPK, n), F32)
    return pl.pallas_call(
        functools.partial(_peer_pairs_kernel, n_keys=n_keys),
        grid=(n // tn, nh),
        in_specs=[pl.BlockSpec((2, n_keys, tn), lambda t, h: (h, 0, t))],
        out_specs=(blk, blk),
        out_shape=(shape, shape),
        compiler_params=_params(("parallel", "arbitrary")),
        name="peer_pairs",
    )(scores)


def _peer_gates_kernel(code_ref, g_ref, o_ref, i1t_ref, i2t_ref, gt_ref, *, n_keys, ib):
    code = code_ref[...].T
    i1 = jnp.floor(code * (1.0 / n_keys))
    i1t_ref[...] = i1
    i2t_ref[...] = code - i1 * float(n_keys)
    gt_ref[...] = g_ref[...].T
    tg, n_pairs = i1t_ref.shape
    key = lax.broadcasted_iota(jnp.int32, (n_keys, n_pairs), 0).astype(F32)

    def body(n, carry):
        i1r = i1t_ref[pl.ds(n, 1), :]
        i2r = i2t_ref[pl.ds(n, 1), :]
        gr = gt_ref[pl.ds(n, 1), :]
        a_t = jnp.where(key == i1r, gr, 0.0).astype(BF16)
        b_t = jnp.where(key == i2r, 1.0, 0.0).astype(BF16)
        grid_n = lax.dot_general(a_t, b_t, _NT, preferred_element_type=F32)
        for blk in range(o_ref.shape[0]):
            o_ref[blk, n] = grid_n[blk * ib:(blk + 1) * ib, :]
        return carry

    lax.fori_loop(0, tg, body, 0, unroll=32)


def _peer_gates(code, g, n_keys, tg, ib):
    n_pairs, n = code.shape
    blk = pl.BlockSpec((n_pairs, tg), lambda t: (0, t))
    body = functools.partial(_peer_gates_kernel, n_keys=n_keys, ib=ib)
    return pl.pallas_call(
        body,
        grid=(n // tg,),
        in_specs=[blk, blk],
        out_specs=pl.BlockSpec((n_keys // ib, tg, ib, n_keys), lambda t: (0, t, 0, 0)),
        out_shape=jax.ShapeDtypeStruct((n_keys // ib, n, ib, n_keys), F32),
        scratch_shapes=[pltpu.VMEM((tg, n_pairs), F32)] * 3,
        compiler_params=_params(("parallel",)),
        name="peer_gates",
    )(code, g)


def _peer_experts_kernel(h_ref, x1_ref, gate_ref, u_ref, v_ref, o_ref, *, ib):
    @pl.when(pl.program_id(1) == 0)
    def _():
        o_ref[...] = x1_ref[...]

    act = _gelu(lax.dot_general(h_ref[...], u_ref[...], _NT, preferred_element_type=F32))
    tn = h_ref.shape[0]
    gates = jnp.concatenate([gate_ref[0, pl.ds(i, tn, stride=ib), :] for i in range(ib)], axis=1)
    o_ref[...] += jnp.dot((gates * act).astype(BF16), v_ref[...], preferred_element_type=F32)


def _peer_experts(h2, x1, gates, u_bf, v_bf, tn, ib):
    n, d = h2.shape
    n_blocks, _, _, n_keys = gates.shape
    ec = ib * n_keys
    body = functools.partial(_peer_experts_kernel, ib=ib)
    return pl.pallas_call(
        body,
        grid=(n // tn, n_blocks),
        in_specs=[
            pl.BlockSpec((tn, d), lambda t, e: (t, 0)),
            pl.BlockSpec((tn, d), lambda t, e: (t, 0), pipeline_mode=pl.Buffered(1)),
            pl.BlockSpec((1, tn * ib, n_keys), lambda t, e: (e, t, 0)),
            pl.BlockSpec((ec, d), lambda t, e: (e, 0)),
            pl.BlockSpec((ec, d), lambda t, e: (e, 0)),
        ],
        out_specs=pl.BlockSpec((tn, d), lambda t, e: (t, 0)),
        out_shape=jax.ShapeDtypeStruct((n, d), F32),
        compiler_params=_params(("parallel", "arbitrary")),
        name="peer_experts",
    )(h2, x1, gates.reshape(n_blocks, n * ib, n_keys), u_bf, v_bf)


def _tile(n, pref):
    return pref if n % pref == 0 else n


def _layer(x, attend, shift0, s0, t_real, lam_init, lw):
    b, t, d = x.shape
    n = b * t
    xf = x.reshape(n, d)
    w_da = lw["w_q"].shape[1]
    tm = _tile(n, 512)
    h1 = _rms(xf, lw["norm1_g"], tm)
    tmp = _tile(n, 1024)
    qb = _proj("q", h1, lw["w_q"], tmp, _tile(w_da, 512), gain=lw["q_gain"], group=lw["dk"], scale=lw["dk"] ** -0.5)
    if t % LANES == 0:
        seqs, tmk = b, _tile(t, 1024)
    else:
        seqs, tmk = 1, n
    k, kbt = _proj_keys(h1, lw["w_k"], lw["k_gain"], lw["dk"], seqs, tmk, _tile(w_da, 512))
    v, vb = _proj("dual", h1, lw["w_v"], tmp, _tile(w_da, 512))
    p_pad = lw["w_rw"].shape[1]
    rw = _proj("plain", h1, lw["w_rw"], tmp, p_pad // 3)

    o_da = attend(qb.reshape(b, t, w_da), kbt, vb.reshape(b, t, w_da))

    rw3 = rw.reshape(b, t, p_pad)
    if t % RW_CHUNK:
        rw3 = jnp.pad(rw3, ((0, 0), (0, RW_CHUNK - t % RW_CHUNK), (0, 0)))
    o_rw, s_out = _rwkv(rw3, shift0, s0, lw["rwkv"], t_real)
    o_rw = o_rw[:, :t].reshape(n, -1)

    x1, h2, h2t = _outproj(xf, o_da.reshape(n, w_da), o_rw, lw["w_out_da"], lw["w_out_rw"], lw["norm2_g"],
                           _tile(n, 256))
    tn = _tile(n, 512)
    scores = _peer_scores(h2t, lw["wq_t"], lw["peer_keys"], tn)
    code, gate = _peer_pairs(scores, tn)
    ib = 8
    gates = _peer_gates(code, gate, scores.shape[1], _tile(n, 128), ib)
    out = _peer_experts(h2, x1, gates, lw["peer_u"], lw["peer_v"], tn, ib)
    return out.reshape(b, t, d), k, v, s_out, rw3[:, t - 1, :lw["rw_proj"]]


def kernel(x_prompt, x_sample, cache_k, cache_v, state_wkv, state_shift, page_table, norm1_g, w_in, da_qk_g,
           da_lambda, da_subln_g, rw_mu, rw_w0, rw_w2, rw_a0, rw_a2, rw_g2, rw_kk, rw_ka, rw_rk, rw_lnx,
           w_out, norm2_g, peer_wq, peer_keys, peer_u, peer_v):
    depth = w_in.shape[0]
    d = x_prompt.shape[-1]
    heads, dk = cache_k.shape[3], cache_k.shape[5]
    w_da = heads * 2 * dk
    rw_heads, head = state_wkv.shape[2], state_wkv.shape[3]
    wd = rw_heads * head
    r_w, r_a, r_g = rw_w2.shape[1], rw_a2.shape[1], rw_g2.shape[1]
    rw_proj = 3 * wd + r_w + r_a + r_g
    g_pad = 2 * LANES
    p_pad = 3 * wd + LANES + g_pad
    assert d == w_da + wd and head * 2 == LANES and wd % LANES == 0 and 2 * dk == LANES
    assert r_w + r_a == LANES and r_g <= g_pad and p_pad % 3 == 0 and (p_pad // 3) % LANES == 0
    assert cache_k.shape[2] == PAGE_SIZE
    nh, _, n_keys, qd = peer_keys.shape[1:]
    bsz = x_prompt.shape[0]
    dec_b, dec_t = x_sample.shape[:2]
    assert dec_t * heads <= PAGE_SIZE and x_prompt.shape[1] % LANES == 0
    assert n_keys & (n_keys - 1) == 0 and peer_u.shape[1] == n_keys * n_keys

    xp, xs = x_prompt, x_sample
    outs = [[] for _ in range(8)]
    for l in range(depth):
        lam_init = 0.8 - 0.6 * math.exp(-0.3 * l)
        wl = w_in[l].astype(BF16)
        w_rw = jnp.pad(wl[:, 3 * w_da:], ((0, 0), (0, p_pad - rw_proj)))
        zero_wa = jnp.zeros((r_w, wd), F32)
        wwa = jnp.concatenate([jnp.concatenate([rw_w2[l], zero_wa], axis=1),
                               jnp.concatenate([zero_wa, rw_a2[l]], axis=1)], axis=0).astype(BF16)
        row = lambda z: z.reshape(1, -1)
        rwkv_prm = dict(
            mu=jnp.pad(row(rw_mu[l]), ((0, 0), (0, p_pad - rw_proj))),
            w0=row(rw_w0[l]), a0=row(rw_a0[l]), wwa=wwa,
            g2=jnp.pad(rw_g2[l], ((0, g_pad - r_g), (0, 0))).astype(BF16),
            kk=row(rw_kk[l]), ka=row(rw_ka[l]), rk=row(rw_rk[l]),
            lng=row(rw_lnx[l, 0]), lnb=row(rw_lnx[l, 1]),
            tril=jnp.tril(jnp.ones((RW_CHUNK, RW_CHUNK), BF16)),
            bd=_block_diag_ones(LANES, head), r_w=r_w, head=head)
        lw = dict(
            norm1_g=norm1_g[l], w_q=wl[:, :w_da], w_k=wl[:, w_da:2 * w_da], w_v=wl[:, 2 * w_da:3 * w_da],
            w_rw=w_rw, q_gain=jnp.tile(da_qk_g[l, 0], 2 * heads), k_gain=jnp.tile(da_qk_g[l, 1], 2 * heads),
            dk=dk, rwkv=rwkv_prm, rw_proj=rw_proj,
            w_out_da=w_out[l, :w_da].astype(BF16), w_out_rw=w_out[l, w_da:].astype(BF16),
            norm2_g=norm2_g[l], wq_t=peer_wq[l].T.astype(BF16),
            peer_keys=peer_keys[l].reshape(nh * 2, n_keys, qd).astype(BF16),
            peer_u=peer_u[l].astype(BF16), peer_v=peer_v[l].astype(BF16))

        attend_p = functools.partial(_attn_prompt, da_lambda=da_lambda[l], subln_g=da_subln_g[l], heads=heads,
                                     lam_init=lam_init, tq=_tile(x_prompt.shape[1], 512))
        n_pairs = wd // LANES
        xp, kp, vp, sp, shp = _layer(
            xp, attend_p, jnp.zeros((bsz, 1, p_pad), F32), jnp.zeros((bsz, n_pairs, LANES, LANES), F32),
            RW_CHUNK, lam_init, lw)

        attend_s = functools.partial(_attn_sample, cache_k=cache_k, cache_v=cache_v, layer=l,
                                     page_table=page_table, da_lambda=da_lambda[l], subln_g=da_subln_g[l],
                                     heads=heads, lam_init=lam_init,
                                     pp=8 if page_table.shape[1] % 8 == 0 else 4)
        shift_s = jnp.pad(state_shift[l], ((0, 0), (0, p_pad - rw_proj))).reshape(dec_b, 1, p_pad)
        xs, ksm, vsm, ssm, shs = _layer(xs, attend_s, shift_s, _pack_state(state_wkv[l], head), dec_t, lam_init, lw)

        outs[0].append(jnp.transpose(kp.reshape(bsz, heads, 2, dk, -1), (0, 4, 1, 2, 3)))
        outs[1].append(vp.reshape(bsz, -1, heads, 2 * dk))
        outs[2].append(_unpack_state(sp, head))
        outs[3].append(shp)
        outs[4].append(jnp.transpose(ksm.reshape(heads, 2, dk, dec_b, dec_t), (3, 4, 0, 1, 2)))
        outs[5].append(vsm.reshape(dec_b, dec_t, heads, 2 * dk))
        outs[6].append(_unpack_state(ssm, head))
        outs[7].append(shs)
    return (xp, xs) + tuple(jnp.stack(o) for o in outs)
```

```python
import functools
import math

import jax
import jax.numpy as jnp
from jax import lax
from jax.experimental import pallas as pl
from jax.experimental.pallas import tpu as pltpu

F32 = jnp.float32
BF16 = jnp.bfloat16

LANES = 128
RMS_EPS = 1e-6
LNX_EPS = 64e-5
PEER_TOPK = 16
PAGE_SIZE = 128
RW_CHUNK = 64
VMEM_LIMIT = 48 * 1024 * 1024

_NT = (((1,), (1,)), ((), ()))
_TN = (((0,), (0,)), ((), ()))


def _params(sem):
    return pltpu.CompilerParams(dimension_semantics=sem, vmem_limit_bytes=VMEM_LIMIT)


def _mm(a, b):
    return jnp.dot(a.astype(BF16), b.astype(BF16), preferred_element_type=F32)


def _mm_nt(a, b):
    return lax.dot_general(a.astype(BF16), b.astype(BF16), _NT, preferred_element_type=F32)


def _mm_tn(a, b):
    return lax.dot_general(a.astype(BF16), b.astype(BF16), _TN, preferred_element_type=F32)


def _mm_split(a_exact_bf16, b):
    hi = b.astype(BF16)
    lo = (b - hi.astype(F32)).astype(BF16)
    return (jnp.dot(a_exact_bf16, hi, preferred_element_type=F32)
            + jnp.dot(a_exact_bf16, lo, preferred_element_type=F32))


def _mm_split_lhs(a, b_exact_bf16):
    hi = a.astype(BF16)
    lo = (a - hi.astype(F32)).astype(BF16)
    return (jnp.dot(hi, b_exact_bf16, preferred_element_type=F32)
            + jnp.dot(lo, b_exact_bf16, preferred_element_type=F32))


def _block_diag_ones(n, group):
    i = jnp.arange(n)
    return (i[:, None] // group == i[None, :] // group).astype(BF16)


def _rms_kernel(x_ref, g_ref, o_ref):
    x = x_ref[...]
    ms = jnp.mean(x * x, axis=-1, keepdims=True)
    o_ref[...] = (x * lax.rsqrt(ms + RMS_EPS) * g_ref[...]).astype(o_ref.dtype)


def _rms(x, g, tm):
    n, d = x.shape
    return pl.pallas_call(
        _rms_kernel,
        grid=(n // tm,),
        in_specs=[pl.BlockSpec((tm, d), lambda i: (i, 0)), pl.BlockSpec((1, d), lambda i: (0, 0))],
        out_specs=pl.BlockSpec((tm, d), lambda i: (i, 0)),
        out_shape=jax.ShapeDtypeStruct((n, d), BF16),
        compiler_params=_params(("parallel",)),
        name="rms",
    )(x, g.reshape(1, d))


def _proj_plain_kernel(h_ref, w_ref, o_ref):
    o_ref[...] = jnp.dot(h_ref[...], w_ref[...], preferred_element_type=F32)


def _proj_dual_kernel(h_ref, w_ref, o_ref, ob_ref):
    y = jnp.dot(h_ref[...], w_ref[...], preferred_element_type=F32)
    o_ref[...] = y
    ob_ref[...] = y.astype(BF16)


def _qk_normed(h_ref, w_ref, bd_ref, g_ref, group):
    y = jnp.dot(h_ref[...], w_ref[...], preferred_element_type=F32)
    yy = y * y
    bd = bd_ref[...]
    sums = [_mm_split_lhs(yy[:, c * LANES:(c + 1) * LANES], bd) for c in range(y.shape[1] // LANES)]
    ms = jnp.concatenate(sums, axis=1) * (1.0 / group)
    return y * lax.rsqrt(ms + RMS_EPS) * g_ref[...]


def _proj_q_kernel(h_ref, w_ref, bd_ref, g_ref, ob_ref, *, group, scale):
    ob_ref[...] = (_qk_normed(h_ref, w_ref, bd_ref, g_ref, group) * scale).astype(BF16)


def _proj_k_kernel(h_ref, w_ref, bd_ref, g_ref, o_ref, ob_ref, *, group):
    yt = _qk_normed(h_ref, w_ref, bd_ref, g_ref, group).T
    o_ref[0] = yt
    ob_ref[0] = yt.astype(BF16)


def _proj_keys(h, w, gain, group, seqs, tm, tn):
    n, d = h.shape
    ncol = w.shape[1]
    s = n // seqs
    tiles = s // tm
    o_spec = pl.BlockSpec((1, tn, tm), lambda i, j: (i // tiles, j, i % tiles))
    return pl.pallas_call(
        functools.partial(_proj_k_kernel, group=group),
        grid=(n // tm, ncol // tn),
        in_specs=[pl.BlockSpec((tm, d), lambda i, j: (i, 0)), pl.BlockSpec((d, tn), lambda i, j: (0, j)),
                  pl.BlockSpec((LANES, LANES), lambda i, j: (0, 0)), pl.BlockSpec((1, tn), lambda i, j: (0, j))],
        out_specs=(o_spec, o_spec),
        out_shape=(jax.ShapeDtypeStruct((seqs, ncol, s), F32), jax.ShapeDtypeStruct((seqs, ncol, s), BF16)),
        compiler_params=_params(("parallel", "arbitrary")),
        name="proj_k",
    )(h, w, _block_diag_ones(LANES, group), gain.reshape(1, ncol))


def _proj(kind, h, w, tm, tn, gain=None, group=None, scale=None):
    n, d = h.shape
    ncol = w.shape[1]
    grid = (n // tm, ncol // tn)
    h_spec = pl.BlockSpec((tm, d), lambda i, j: (i, 0))
    w_spec = pl.BlockSpec((d, tn), lambda i, j: (0, j))
    o_spec = pl.BlockSpec((tm, tn), lambda i, j: (i, j))
    f32_out = jax.ShapeDtypeStruct((n, ncol), F32)
    bf_out = jax.ShapeDtypeStruct((n, ncol), BF16)
    if kind == "q":
        bd = _block_diag_ones(LANES, group)
        extra = [bd, gain.reshape(1, ncol)]
        extra_specs = [pl.BlockSpec((LANES, LANES), lambda i, j: (0, 0)), pl.BlockSpec((1, tn), lambda i, j: (0, j))]
    else:
        extra, extra_specs = [], []
    if kind == "q":
        body = functools.partial(_proj_q_kernel, group=group, scale=scale)
        out_shape, out_specs = bf_out, o_spec
    elif kind == "dual":
        body = _proj_dual_kernel
        out_shape, out_specs = (f32_out, bf_out), (o_spec, o_spec)
    else:
        body = _proj_plain_kernel
        out_shape, out_specs = f32_out, o_spec
    return pl.pallas_call(
        body,
        grid=grid,
        in_specs=[h_spec, w_spec] + extra_specs,
        out_specs=out_specs,
        out_shape=out_shape,
        compiler_params=_params(("parallel", "arbitrary")),
        name="proj_" + kind,
    )(h, w, *extra)


def _lambda(lam_ref, lam_init):
    ll = lam_ref[...]
    s01 = jnp.sum(ll[0:1, :] * ll[1:2, :], axis=-1, keepdims=True)
    s23 = jnp.sum(ll[2:3, :] * ll[3:4, :], axis=-1, keepdims=True)
    return jnp.exp(s01) - jnp.exp(s23) + lam_init


def _subln(o, g, lam_init):
    ms = jnp.mean(o * o, axis=-1, keepdims=True)
    return (o * lax.rsqrt(ms + RMS_EPS) * g) * (1.0 - lam_init)


def _attn_prompt_kernel(q_ref, k_ref, v_ref, lam_ref, g_ref, o_ref, acc0_ref, acc1_ref, *, tq, tk, dk, lam_init):
    qi = pl.program_id(2)
    q = q_ref[0]
    lane = lax.broadcasted_iota(jnp.int32, q.shape, 1)
    qf = q.astype(F32)
    qc = (jnp.where(lane < dk, qf, 0.0).astype(BF16), jnp.where(lane >= dk, qf, 0.0).astype(BF16))
    accs = (acc0_ref, acc1_ref)
    acc0_ref[...] = jnp.zeros_like(acc0_ref)
    acc1_ref[...] = jnp.zeros_like(acc1_ref)
    row = lax.broadcasted_iota(jnp.int32, (tq, tk), 0)
    col = lax.broadcasted_iota(jnp.int32, (tq, tk), 1)
    n_sub = tq // tk

    def step(j, carry, diag_offset=None):
        kt = k_ref[0, :, pl.ds(pl.multiple_of(j * tk, tk), tk)]
        vb = v_ref[0, pl.ds(pl.multiple_of(j * tk, tk), tk), :]
        new = []
        for c in range(2):
            m, l = carry[2 * c], carry[2 * c + 1]
            s = jnp.dot(qc[c], kt, preferred_element_type=F32)
            if diag_offset is not None:
                s = jnp.where(col + diag_offset <= row, s, -jnp.inf)
            m_new = jnp.maximum(m, jnp.max(s, axis=-1, keepdims=True))
            corr = jnp.exp(m - m_new)
            p = jnp.exp(s - m_new)
            l = l * corr + jnp.sum(p, axis=-1, keepdims=True)
            accs[c][...] = accs[c][...] * corr + jnp.dot(p.astype(BF16), vb, preferred_element_type=F32)
            new += [m_new, l]
        return tuple(new)

    minit = jnp.full((tq, 1), -jnp.inf, F32)
    linit = jnp.zeros((tq, 1), F32)
    carry = lax.fori_loop(0, qi * n_sub, step, (minit, linit, minit, linit))
    for sub in range(n_sub):
        carry = step(qi * n_sub + sub, carry, diag_offset=sub * tk)
    _, l0, _, l1 = carry
    lam = _lambda(lam_ref, lam_init)
    o = acc0_ref[...] / l0 - lam * (acc1_ref[...] / l1)
    o_ref[0] = _subln(o, g_ref[...], lam_init).astype(o_ref.dtype)


def _attn_prompt(qb, kbt, vb, da_lambda, subln_g, heads, lam_init, tq):
    b, s, w = qb.shape
    dv = w // heads
    body = functools.partial(_attn_prompt_kernel, tq=tq, tk=tq, dk=dv // 2, lam_init=lam_init)
    return pl.pallas_call(
        body,
        grid=(b, heads, s // tq),
        in_specs=[
            pl.BlockSpec((1, tq, dv), lambda bi, h, qi: (bi, qi, h)),
            pl.BlockSpec((1, dv, s), lambda bi, h, qi: (bi, h, 0)),
            pl.BlockSpec((1, s, dv), lambda bi, h, qi: (bi, 0, h)),
            pl.BlockSpec(da_lambda.shape, lambda bi, h, qi: (0, 0)),
            pl.BlockSpec((1, dv), lambda bi, h, qi: (0, 0)),
        ],
        out_specs=pl.BlockSpec((1, tq, dv), lambda bi, h, qi: (bi, qi, h)),
        out_shape=jax.ShapeDtypeStruct((b, s, w), BF16),
        scratch_shapes=[pltpu.VMEM((tq, dv), F32), pltpu.VMEM((tq, dv), F32)],
        compiler_params=_params(("parallel", "parallel", "arbitrary")),
        name="attn_prompt",
    )(qb, kbt, vb, da_lambda, subln_g.reshape(1, dv))


def _attn_sample_kernel(pt_ref, qrep_ref, kst_ref, vs_ref, lam_ref, g_ref, *rest, pp, heads, t, dk, lam_init):
    kc_refs = rest[:pp]
    vc_refs = rest[pp:2 * pp]
    o_ref = rest[2 * pp]
    qbd_ref, m_ref, l_ref, acc_ref = rest[2 * pp + 1:]
    del pt_ref
    p_idx = pl.program_id(1)
    rows = 2 * t * heads
    dv = 2 * dk
    w = heads * dv

    @pl.when(p_idx == 0)
    def _():
        r = lax.broadcasted_iota(jnp.int32, (rows, w), 0)
        col = lax.broadcasted_iota(jnp.int32, (rows, w), 1)
        keep = (col // dk) == ((r % heads) * 2 + r // (t * heads))
        qbd_ref[...] = jnp.where(keep, qrep_ref[0].astype(F32), 0.0).astype(BF16)
        m_ref[...] = jnp.full(m_ref.shape, -jnp.inf, F32)
        l_ref[...] = jnp.zeros(l_ref.shape, F32)
        acc_ref[...] = jnp.zeros(acc_ref.shape, F32)

    qbd = qbd_ref[...]

    def online(s, vbs):
        m = m_ref[...]
        m_new = jnp.maximum(m, jnp.max(s, axis=-1, keepdims=True))
        corr = jnp.exp(m - m_new)
        p = jnp.exp(s - m_new)
        l_ref[...] = l_ref[...] * corr + jnp.sum(p, axis=-1, keepdims=True)
        pv = None
        for i, vb in enumerate(vbs):
            n = vb.shape[0]
            d = jnp.dot(p[:, i * n:(i + 1) * n].astype(BF16), vb, preferred_element_type=F32)
            pv = d if pv is None else pv + d
        acc_ref[...] = acc_ref[...] * corr + pv
        m_ref[...] = m_new

    kt = jnp.concatenate([kc[0] for kc in kc_refs], axis=1).astype(BF16)
    vwide = [jnp.concatenate([vc[0, pl.ds(h, PAGE_SIZE, stride=heads), :] for h in range(heads)], axis=1)
             .astype(BF16) for vc in vc_refs]
    online(jnp.dot(qbd, kt, preferred_element_type=F32), vwide)

    @pl.when(p_idx == pl.num_programs(1) - 1)
    def _():
        s = jnp.dot(qbd, kst_ref[0], preferred_element_type=F32)
        r = lax.broadcasted_iota(jnp.int32, s.shape, 0)
        col = lax.broadcasted_iota(jnp.int32, s.shape, 1)
        online(jnp.where(col <= (r // heads) % t, s, -jnp.inf), [vs_ref[0]])
        out = acc_ref[...] / l_ref[...]
        hrow = lax.broadcasted_iota(jnp.int32, (heads, w), 0)
        hcol = lax.broadcasted_iota(jnp.int32, (heads, w), 1) // dv
        pick = hrow == hcol
        lam = _lambda(lam_ref, lam_init)
        toks = []
        for tok_i in range(t):
            maps = []
            for c in range(2):
                base = (c * t + tok_i) * heads
                maps.append(jnp.sum(jnp.where(pick, out[base:base + heads, :], 0.0), axis=0, keepdims=True))
            toks.append(maps[0] - lam * maps[1])
        o = jnp.concatenate(toks, axis=0)
        g = g_ref[...]
        outs = [_subln(o[:, h * dv:(h + 1) * dv], g, lam_init) for h in range(heads)]
        o_ref[0] = jnp.concatenate(outs, axis=1).astype(o_ref.dtype)


def _attn_sample(qb, kbt, vb, cache_k, cache_v, layer, page_table, da_lambda, subln_g, heads, lam_init, pp):
    bd, t, w = qb.shape
    kb = jnp.transpose(kbt.reshape(w, bd, t), (1, 2, 0))
    dv = w // heads
    n_pool = cache_k.shape[1]
    n_pages = page_table.shape[1]
    rows = 2 * t * heads
    qrep = jnp.tile(jnp.repeat(qb, heads, axis=1), (1, 2, 1))
    kst_pad = jnp.pad(jnp.swapaxes(kb, 1, 2), ((0, 0), (0, 0), (0, PAGE_SIZE - t)))
    vs_pad = jnp.pad(vb, ((0, 0), (0, PAGE_SIZE - t), (0, 0)))
    ckt = jnp.transpose(cache_k, (0, 1, 3, 4, 5, 2)).reshape(-1, w, PAGE_SIZE)
    cv = cache_v.reshape(-1, PAGE_SIZE * heads, dv)
    base = layer * n_pool

    def k_spec(i):
        return pl.BlockSpec((1, w, PAGE_SIZE), lambda b, p, pt: (base + pt[b, p * pp + i], 0, 0))

    def v_spec(i):
        return pl.BlockSpec((1, PAGE_SIZE * heads, dv), lambda b, p, pt: (base + pt[b, p * pp + i], 0, 0))

    body = functools.partial(_attn_sample_kernel, pp=pp, heads=heads, t=t, dk=dv // 2, lam_init=lam_init)
    grid_spec = pltpu.PrefetchScalarGridSpec(
        num_scalar_prefetch=1,
        grid=(bd, n_pages // pp),
        in_specs=[
            pl.BlockSpec((1, rows, w), lambda b, p, pt: (b, 0, 0)),
            pl.BlockSpec((1, w, PAGE_SIZE), lambda b, p, pt: (b, 0, 0)),
            pl.BlockSpec((1, PAGE_SIZE, w), lambda b, p, pt: (b, 0, 0)),
            pl.BlockSpec(da_lambda.shape, lambda b, p, pt: (0, 0)),
            pl.BlockSpec((1, dv), lambda b, p, pt: (0, 0)),
        ] + [k_spec(i) for i in range(pp)] + [v_spec(i) for i in range(pp)],
        out_specs=pl.BlockSpec((1, t, w), lambda b, p, pt: (b, 0, 0)),
        scratch_shapes=[pltpu.VMEM((rows, w), BF16), pltpu.VMEM((rows, 1), F32),
                        pltpu.VMEM((rows, 1), F32), pltpu.VMEM((rows, w), F32)],
    )
    return pl.pallas_call(
        body,
        grid_spec=grid_spec,
        out_shape=jax.ShapeDtypeStruct((bd, t, w), BF16),
        compiler_params=_params(("parallel", "arbitrary")),
        name="attn_sample",
    )(page_table, qrep, kst_pad, vs_pad, da_lambda, subln_g.reshape(1, dv), *([ckt] * pp), *([cv] * pp))


def _softplus(z):
    return jnp.maximum(z, 0.0) + jnp.log1p(jnp.exp(-jnp.abs(z)))


def _sigmoid(z):
    return 1.0 / (1.0 + jnp.exp(-z))


def _split_heads(x, first):
    zero = jnp.zeros_like(x)
    return jnp.concatenate([jnp.where(first, x, zero), jnp.where(first, zero, x)], axis=0)


def _rwkv_kernel(rw_ref, shift_ref, s0_ref, mu_ref, w0_ref, a0_ref, wwa_ref, g2_ref, kk_ref, ka_ref,
                 rk_ref, lng_ref, lnb_ref, tril_ref, bd_ref, o_ref, sout_ref, state_ref, prev_ref,
                 *, wd, r_w, t_real, head):
    c_idx = pl.program_id(1)
    cl = RW_CHUNK
    pairs = range(wd // LANES)

    @pl.when(c_idx == 0)
    def _():
        state_ref[...] = s0_ref[0]
        prev_ref[...] = shift_ref[0]

    lane = lax.broadcasted_iota(jnp.int32, (cl, LANES), 1)
    first = lane < head
    tt = lax.broadcasted_iota(jnp.int32, (cl, 2 * cl), 0)
    ss = lax.broadcasted_iota(jnp.int32, (cl, 2 * cl), 1) % cl
    strict = ss < tt
    incl = ss <= tt
    firstw = lax.broadcasted_iota(jnp.int32, (cl, 2 * cl), 1) < cl
    r2 = lax.broadcasted_iota(jnp.int32, (2 * cl, 2 * cl), 0)
    c2 = lax.broadcasted_iota(jnp.int32, (2 * cl, 2 * cl), 1)
    eye = (r2 == c2).astype(F32)
    rs = lax.broadcasted_iota(jnp.int32, (LANES, LANES), 0) // head
    cs = lax.broadcasted_iota(jnp.int32, (LANES, LANES), 1) // head
    same_head = rs == cs
    bd = bd_ref[...]
    n_levels = int(math.log2(cl))
    sls = [slice(p * LANES, (p + 1) * LANES) for p in pairs]

    def chunk(rows, prev_row, st):
        x = rw_ref[0, rows, :]
        rowp = lax.broadcasted_iota(jnp.int32, x.shape, 0)
        prevx = jnp.where(rowp == 0, prev_row, pltpu.roll(x, 1, axis=0))
        xm = x + (prevx - x) * mu_ref[...]
        r = xm[:, 0:wd]
        kr = xm[:, wd:2 * wd]
        vr = xm[:, 2 * wd:3 * wd]
        slab = xm[:, 3 * wd:3 * wd + LANES]
        lane_s = lax.broadcasted_iota(jnp.int32, slab.shape, 1)
        wa = _mm(jnp.where(lane_s < r_w, jnp.tanh(slab), slab), wwa_ref[...])
        gd = xm[:, 3 * wd + LANES:]
        g = _mm(_sigmoid(gd), g2_ref[...])
        w = -_softplus(-(w0_ref[...] + wa[:, :wd])) - 0.5
        lw = -jnp.exp(w)
        a = _sigmoid(a0_ref[...] + wa[:, wd:])
        kk = kr * kk_ref[...]
        kh = kr * (1.0 + (a - 1.0) * ka_ref[...])
        beta_scale = a
        if t_real < cl:
            valid = lax.broadcasted_iota(jnp.int32, (cl, wd), 0) < t_real
            lw = jnp.where(valid, lw, 0.0)
            beta_scale = jnp.where(valid, a, 0.0)
            kh_state = jnp.where(valid, kh, 0.0)
        else:
            kh_state = kh
        cum = _mm_split(tril_ref[...], lw)
        cum_last = cum[cl - 1:cl, :]
        e_neg = jnp.exp(-cum)
        e_end = jnp.exp(cum_last - cum)
        w_end = jnp.exp(cum_last)
        rt_all = r * jnp.exp(cum)
        ex_all = jnp.exp(cum - lw)
        bonus_in = r * kh * rk_ref[...]

        kk_p = [kk[:, sl] for sl in sls]
        nrm = [jnp.sqrt(_mm(k * k, bd)) for k in kk_p]
        kkn = [k / jnp.maximum(n, 1e-12) for k, n in zip(kk_p, nrm)]
        beta = [kkn[p] * beta_scale[:, sls[p]] for p in pairs]
        v_p = [vr[:, sl] for sl in sls]
        lhs = [jnp.concatenate([-ex_all[:, sls[p]] * kkn[p], rt_all[:, sls[p]]], axis=0) for p in pairs]
        rhs = [jnp.concatenate([_split_heads(beta[p] * e_neg[:, sls[p]], first),
                                _split_heads(kh_state[:, sls[p]] * e_neg[:, sls[p]], first)], axis=0)
               for p in pairs]
        m = [_mm_nt(lhs[p], rhs[p]) for p in pairs]
        sa = [_mm_nt(lhs[p], st[p]) for p in pairs]
        aab = [jnp.where(strict, x_[:cl, :2 * cl], 0.0) for x_ in m]
        aak = [jnp.where(strict, x_[:cl, 2 * cl:], 0.0) for x_ in m]
        arbk = [jnp.concatenate([jnp.where(incl, x_[cl:, :2 * cl], 0.0), jnp.where(incl, x_[cl:, 2 * cl:], 0.0)],
                                axis=1) for x_ in m]
        abd = [_split_heads(x_, firstw) for x_ in aab]
        vbd = [_split_heads(x_, first) for x_ in v_p]
        rhs_u = [sa[p][:cl, :] + _mm(aak[p], vbd[p]) for p in pairs]
        tinv = [eye + x_ for x_ in abd]
        xp = [_mm(x_, x_) for x_ in abd]
        for lvl in range(1, n_levels):
            if lvl < n_levels - 1:
                both = [_mm(xp[p], jnp.concatenate([tinv[p], xp[p]], axis=1)) for p in pairs]
                tinv = [tinv[p] + both[p][:, :2 * cl] for p in pairs]
                xp = [both[p][:, 2 * cl:] for p in pairs]
            else:
                tinv = [tinv[p] + _mm(xp[p], tinv[p]) for p in pairs]
        tw = [x_[:cl, :] + x_[cl:, :] for x_ in tinv]
        u = [_mm(tw[p], _split_heads(rhs_u[p], first)) for p in pairs]
        y = [sa[p][cl:, :] + _mm(arbk[p], jnp.concatenate([_split_heads(u[p], first), vbd[p]], axis=0))
             for p in pairs]
        ds = [_mm_tn(jnp.concatenate([u[p], v_p[p]], axis=0),
                     jnp.concatenate([beta[p] * e_end[:, sls[p]], kh_state[:, sls[p]] * e_end[:, sls[p]]], axis=0))
              for p in pairs]
        mean = [_mm(x_, bd) * (1.0 / head) for x_ in y]
        dlt = [y[p] - mean[p] for p in pairs]
        var = [_mm(x_ * x_, bd) * (1.0 / head) for x_ in dlt]
        bonus = [_mm(bonus_in[:, sls[p]], bd) * v_p[p] for p in pairs]
        new_st = []
        for p in pairs:
            sl = sls[p]
            new_st.append(st[p] * w_end[:, sl] + jnp.where(same_head, ds[p], 0.0))
            yn = dlt[p] * lax.rsqrt(var[p] + LNX_EPS) * lng_ref[:, sl] + lnb_ref[:, sl]
            o_ref[0, rows, sl] = ((yn + bonus[p]) * g[:, sl]).astype(o_ref.dtype)
        return x[cl - 1:cl, :], new_st

    prev_row = prev_ref[...]
    st = [state_ref[p] for p in pairs]
    for sc in range(rw_ref.shape[1] // cl):
        prev_row, st = chunk(slice(sc * cl, (sc + 1) * cl), prev_row, st)
    prev_ref[...] = prev_row
    for p in pairs:
        state_ref[p] = st[p]

    @pl.when(c_idx == pl.num_programs(1) - 1)
    def _():
        sout_ref[0] = state_ref[...]


def _rwkv(rw, shift0, s0, prm, t_real):
    b, s, p = rw.shape
    wd = prm["w0"].shape[1]
    n_pairs = wd // LANES
    cl = RW_CHUNK * (2 if s % (2 * RW_CHUNK) == 0 else 1)
    body = functools.partial(_rwkv_kernel, wd=wd, r_w=prm["r_w"], t_real=t_real, head=prm["head"])

    def const(shape):
        return pl.BlockSpec(shape, lambda bi, ci: (0,) * len(shape))

    names = ["mu", "w0", "a0", "wwa", "g2", "kk", "ka", "rk", "lng", "lnb", "tril", "bd"]
    return pl.pallas_call(
        body,
        grid=(b, s // cl),
        in_specs=[
            pl.BlockSpec((1, cl, p), lambda bi, ci: (bi, ci, 0)),
            pl.BlockSpec((1, 1, p), lambda bi, ci: (bi, 0, 0)),
            pl.BlockSpec((1, n_pairs, LANES, LANES), lambda bi, ci: (bi, 0, 0, 0)),
        ] + [const(prm[nm].shape) for nm in names],
        out_specs=(pl.BlockSpec((1, cl, wd), lambda bi, ci: (bi, ci, 0)),
                   pl.BlockSpec((1, n_pairs, LANES, LANES), lambda bi, ci: (bi, 0, 0, 0))),
        out_shape=(jax.ShapeDtypeStruct((b, s, wd), BF16),
                   jax.ShapeDtypeStruct((b, n_pairs, LANES, LANES), F32)),
        scratch_shapes=[pltpu.VMEM((n_pairs, LANES, LANES), F32), pltpu.VMEM((1, p), F32)],
        compiler_params=_params(("parallel", "arbitrary")),
        name="rwkv",
    )(rw, shift0, s0, *[prm[nm] for nm in names])


def _pack_state(wkv, head):
    b, h = wkv.shape[:2]
    pairs = wkv.reshape(b, h // 2, 2, head, head)
    z = jnp.zeros((b, h // 2, head, head), wkv.dtype)
    top = jnp.concatenate([pairs[:, :, 0], z], axis=-1)
    bot = jnp.concatenate([z, pairs[:, :, 1]], axis=-1)
    return jnp.concatenate([top, bot], axis=-2)


def _unpack_state(sbd, head):
    b, n_pairs = sbd.shape[:2]
    h0 = sbd[:, :, :head, :head]
    h1 = sbd[:, :, head:, head:]
    return jnp.stack([h0, h1], axis=2).reshape(b, 2 * n_pairs, head, head)


def _outproj_kernel(x_ref, oda_ref, orw_ref, w1_ref, w2_ref, g_ref, x1_ref, h_ref, ht_ref):
    y = (jnp.dot(oda_ref[...], w1_ref[...], preferred_element_type=F32)
         + jnp.dot(orw_ref[...], w2_ref[...], preferred_element_type=F32))
    x1 = x_ref[...] + y
    x1_ref[...] = x1
    ms = jnp.mean(x1 * x1, axis=-1, keepdims=True)
    h2 = x1 * lax.rsqrt(ms + RMS_EPS) * g_ref[...]
    h_ref[...] = h2.astype(BF16)
    ht_ref[...] = h2.T.astype(BF16)


def _outproj(x, oda, orw, w1, w2, g, tm):
    n, d = x.shape
    wa, wr = oda.shape[1], orw.shape[1]
    return pl.pallas_call(
        _outproj_kernel,
        grid=(n // tm,),
        in_specs=[
            pl.BlockSpec((tm, d), lambda i: (i, 0)),
            pl.BlockSpec((tm, wa), lambda i: (i, 0)),
            pl.BlockSpec((tm, wr), lambda i: (i, 0)),
            pl.BlockSpec((wa, d), lambda i: (0, 0)),
            pl.BlockSpec((wr, d), lambda i: (0, 0)),
            pl.BlockSpec((1, d), lambda i: (0, 0)),
        ],
        out_specs=(pl.BlockSpec((tm, d), lambda i: (i, 0)), pl.BlockSpec((tm, d), lambda i: (i, 0)),
                   pl.BlockSpec((d, tm), lambda i: (0, i))),
        out_shape=(jax.ShapeDtypeStruct((n, d), F32), jax.ShapeDtypeStruct((n, d), BF16),
                   jax.ShapeDtypeStruct((d, n), BF16)),
        compiler_params=_params(("parallel",)),
        name="outproj",
    )(x, oda, orw, w1, w2, g.reshape(1, d))


def _peer_scores_kernel(ht_ref, wqt_ref, keys_ref, s_ref, q_ref):
    q_ref[...] = jnp.dot(wqt_ref[...], ht_ref[...], preferred_element_type=F32).astype(BF16)
    nhc, _, qd = keys_ref.shape
    for j in range(nhc):
        s_ref[j] = jnp.dot(keys_ref[j], q_ref[j * qd:(j + 1) * qd, :], preferred_element_type=F32)


def _peer_scores(ht, wqt, keys, tn):
    d, n = ht.shape
    nhc, n_keys, qd = keys.shape
    once = pl.Buffered(1)
    return pl.pallas_call(
        _peer_scores_kernel,
        grid=(n // tn,),
        in_specs=[
            pl.BlockSpec((d, tn), lambda t: (0, t)),
            pl.BlockSpec((nhc * qd, d), lambda t: (0, 0), pipeline_mode=once),
            pl.BlockSpec((nhc, n_keys, qd), lambda t: (0, 0, 0), pipeline_mode=once),
        ],
        out_specs=pl.BlockSpec((nhc, n_keys, tn), lambda t: (0, 0, t)),
        out_shape=jax.ShapeDtypeStruct((nhc, n_keys, n), F32),
        scratch_shapes=[pltpu.VMEM((nhc * qd, tn), BF16)],
        compiler_params=_params(("parallel",)),
        name="peer_scores",
    )(ht, wqt, keys)


def _gelu(x):
    return 0.5 * x * (1.0 + lax.erf(x * (2.0 ** -0.5)))


def _take_top(vals, k, payload=None):
    n_rows = vals.shape[0]
    rowid = lax.broadcasted_iota(jnp.int32, vals.shape, 0).astype(F32)
    cur = vals
    top, picked = [], []
    for _ in range(k):
        m = jnp.max(cur, axis=0, keepdims=True)
        sel = jnp.min(jnp.where(cur == m, rowid, float(n_rows)), axis=0, keepdims=True)
        hit = rowid == sel
        top.append(m)
        picked.append(sel if payload is None else jnp.sum(jnp.where(hit, payload, 0.0), axis=0, keepdims=True))
        cur = jnp.where(hit, -jnp.inf, cur)
    return top, picked


def _peer_pairs_kernel(s_ref, code_ref, g_ref, *, n_keys):
    s1 = s_ref[0]
    s2 = s_ref[1]
    a, ia = _take_top(s1, PEER_TOPK)
    b, ib = _take_top(s2, PEER_TOPK)
    b = jnp.concatenate(b, axis=0)
    ib = jnp.concatenate(ib, axis=0)
    counts = [PEER_TOPK // (i + 1) for i in range(PEER_TOPK)]
    cand = jnp.concatenate([a[i] + b[:c, :] for i, c in enumerate(counts)], axis=0)
    code = jnp.concatenate([ia[i] * float(n_keys) + ib[:c, :] for i, c in enumerate(counts)], axis=0)
    best, ids = _take_top(cand, PEER_TOPK, payload=code)
    ex = [jnp.exp(c - best[0]) for c in best]
    z = sum(ex)
    code_ref[...] = jnp.concatenate(ids, axis=0)
    g_ref[...] = jnp.concatenate(ex, axis=0) / z


def _peer_pairs(scores, tn):
    nhc, n_keys, n = scores.shape
    nh = nhc // 2
    blk = pl.BlockSpec((PEER_TOPK, tn), lambda t, h: (h, t))
    shape = jax.ShapeDtypeStruct((nh * PEER_TOPK, n), F32)
    return pl.pallas_call(
        functools.partial(_peer_pairs_kernel, n_keys=n_keys),
        grid=(n // tn, nh),
        in_specs=[pl.BlockSpec((2, n_keys, tn), lambda t, h: (h, 0, t))],
        out_specs=(blk, blk),
        out_shape=(shape, shape),
        compiler_params=_params(("parallel", "arbitrary")),
        name="peer_pairs",
    )(scores)


def _peer_gates_kernel(code_ref, g_ref, o_ref, i1t_ref, i2t_ref, gt_ref, *, n_keys, ib):
    code = code_ref[...].T
    i1 = jnp.floor(code * (1.0 / n_keys))
    i1t_ref[...] = i1
    i2t_ref[...] = code - i1 * float(n_keys)
    gt_ref[...] = g_ref[...].T
    tg, n_pairs = i1t_ref.shape
    key = lax.broadcasted_iota(jnp.int32, (n_keys, n_pairs), 0).astype(F32)

    def body(n, carry):
        i1r = i1t_ref[pl.ds(n, 1), :]
        i2r = i2t_ref[pl.ds(n, 1), :]
        gr = gt_ref[pl.ds(n, 1), :]
        a_t = jnp.where(key == i1r, gr, 0.0).astype(BF16)
        b_t = jnp.where(key == i2r, 1.0, 0.0).astype(BF16)
        grid_n = lax.dot_general(a_t, b_t, _NT, preferred_element_type=F32)
        for blk in range(o_ref.shape[0]):
            o_ref[blk, n] = grid_n[blk * ib:(blk + 1) * ib, :]
        return carry

    lax.fori_loop(0, tg, body, 0, unroll=32)


def _peer_gates(code, g, n_keys, tg, ib):
    n_pairs, n = code.shape
    blk = pl.BlockSpec((n_pairs, tg), lambda t: (0, t))
    body = functools.partial(_peer_gates_kernel, n_keys=n_keys, ib=ib)
    return pl.pallas_call(
        body,
        grid=(n // tg,),
        in_specs=[blk, blk],
        out_specs=pl.BlockSpec((n_keys // ib, tg, ib, n_keys), lambda t: (0, t, 0, 0)),
        out_shape=jax.ShapeDtypeStruct((n_keys // ib, n, ib, n_keys), F32),
        scratch_shapes=[pltpu.VMEM((tg, n_pairs), F32)] * 3,
        compiler_params=_params(("parallel",)),
        name="peer_gates",
    )(code, g)


def _peer_experts_kernel(h_ref, x1_ref, gate_ref, u_ref, v_ref, o_ref, *, ib):
    @pl.when(pl.program_id(1) == 0)
    def _():
        o_ref[...] = x1_ref[...]

    act = _gelu(lax.dot_general(h_ref[...], u_ref[...], _NT, preferred_element_type=F32))
    tn = h_ref.shape[0]
    gates = jnp.concatenate([gate_ref[0, pl.ds(i, tn, stride=ib), :] for i in range(ib)], axis=1)
    o_ref[...] += jnp.dot((gates * act).astype(BF16), v_ref[...], preferred_element_type=F32)


def _peer_experts(h2, x1, gates, u_bf, v_bf, tn, ib):
    n, d = h2.shape
    n_blocks, _, _, n_keys = gates.shape
    ec = ib * n_keys
    body = functools.partial(_peer_experts_kernel, ib=ib)
    return pl.pallas_call(
        body,
        grid=(n // tn, n_blocks),
        in_specs=[
            pl.BlockSpec((tn, d), lambda t, e: (t, 0)),
            pl.BlockSpec((tn, d), lambda t, e: (t, 0), pipeline_mode=pl.Buffered(1)),
            pl.BlockSpec((1, tn * ib, n_keys), lambda t, e: (e, t, 0)),
            pl.BlockSpec((ec, d), lambda t, e: (e, 0)),
            pl.BlockSpec((ec, d), lambda t, e: (e, 0)),
        ],
        out_specs=pl.BlockSpec((tn, d), lambda t, e: (t, 0)),
        out_shape=jax.ShapeDtypeStruct((n, d), F32),
        compiler_params=_params(("parallel", "arbitrary")),
        name="peer_experts",
    )(h2, x1, gates.reshape(n_blocks, n * ib, n_keys), u_bf, v_bf)


def _tile(n, pref):
    return pref if n % pref == 0 else n


def _layer(x, attend, shift0, s0, t_real, lam_init, lw):
    b, t, d = x.shape
    n = b * t
    xf = x.reshape(n, d)
    w_da = lw["w_q"].shape[1]
    tm = _tile(n, 512)
    h1 = _rms(xf, lw["norm1_g"], tm)
    tmp = _tile(n, 1024)
    qb = _proj("q", h1, lw["w_q"], tmp, _tile(w_da, 512), gain=lw["q_gain"], group=lw["dk"], scale=lw["dk"] ** -0.5)
    if t % LANES == 0:
        seqs, tmk = b, _tile(t, 1024)
    else:
        seqs, tmk = 1, n
    k, kbt = _proj_keys(h1, lw["w_k"], lw["k_gain"], lw["dk"], seqs, tmk, _tile(w_da, 512))
    v, vb = _proj("dual", h1, lw["w_v"], tmp, _tile(w_da, 512))
    p_pad = lw["w_rw"].shape[1]
    rw = _proj("plain", h1, lw["w_rw"], tmp, p_pad // 3)

    o_da = attend(qb.reshape(b, t, w_da), kbt, vb.reshape(b, t, w_da))

    rw3 = rw.reshape(b, t, p_pad)
    if t % RW_CHUNK:
        rw3 = jnp.pad(rw3, ((0, 0), (0, RW_CHUNK - t % RW_CHUNK), (0, 0)))
    o_rw, s_out = _rwkv(rw3, shift0, s0, lw["rwkv"], t_real)
    o_rw = o_rw[:, :t].reshape(n, -1)

    x1, h2, h2t = _outproj(xf, o_da.reshape(n, w_da), o_rw, lw["w_out_da"], lw["w_out_rw"], lw["norm2_g"],
                           _tile(n, 256))
    tn = _tile(n, 512)
    scores = _peer_scores(h2t, lw["wq_t"], lw["peer_keys"], tn)
    code, gate = _peer_pairs(scores, tn)
    ib = 8
    gates = _peer_gates(code, gate, scores.shape[1], _tile(n, 128), ib)
    out = _peer_experts(h2, x1, gates, lw["peer_u"], lw["peer_v"], tn, ib)
    return out.reshape(b, t, d), k, v, s_out, rw3[:, t - 1, :lw["rw_proj"]]


def kernel(x_prompt, x_sample, cache_k, cache_v, state_wkv, state_shift, page_table, norm1_g, w_in, da_qk_g,
           da_lambda, da_subln_g, rw_mu, rw_w0, rw_w2, rw_a0, rw_a2, rw_g2, rw_kk, rw_ka, rw_rk, rw_lnx,
           w_out, norm2_g, peer_wq, peer_keys, peer_u, peer_v):
    depth = w_in.shape[0]
    d = x_prompt.shape[-1]
    heads, dk = cache_k.shape[3], cache_k.shape[5]
    w_da = heads * 2 * dk
    rw_heads, head = state_wkv.shape[2], state_wkv.shape[3]
    wd = rw_heads * head
    r_w, r_a, r_g = rw_w2.shape[1], rw_a2.shape[1], rw_g2.shape[1]
    rw_proj = 3 * wd + r_w + r_a + r_g
    g_pad = 2 * LANES
    p_pad = 3 * wd + LANES + g_pad
    assert d == w_da + wd and head * 2 == LANES and wd % LANES == 0 and 2 * dk == LANES
    assert r_w + r_a == LANES and r_g <= g_pad and p_pad % 3 == 0 and (p_pad // 3) % LANES == 0
    assert cache_k.shape[2] == PAGE_SIZE
    nh, _, n_keys, qd = peer_keys.shape[1:]
    bsz = x_prompt.shape[0]
    dec_b, dec_t = x_sample.shape[:2]
    assert dec_t * heads <= PAGE_SIZE and x_prompt.shape[1] % LANES == 0
    assert n_keys & (n_keys - 1) == 0 and peer_u.shape[1] == n_keys * n_keys

    xp, xs = x_prompt, x_sample
    outs = [[] for _ in range(8)]
    for l in range(depth):
        lam_init = 0.8 - 0.6 * math.exp(-0.3 * l)
        wl = w_in[l].astype(BF16)
        w_rw = jnp.pad(wl[:, 3 * w_da:], ((0, 0), (0, p_pad - rw_proj)))
        zero_wa = jnp.zeros((r_w, wd), F32)
        wwa = jnp.concatenate([jnp.concatenate([rw_w2[l], zero_wa], axis=1),
                               jnp.concatenate([zero_wa, rw_a2[l]], axis=1)], axis=0).astype(BF16)
        row = lambda z: z.reshape(1, -1)
        rwkv_prm = dict(
            mu=jnp.pad(row(rw_mu[l]), ((0, 0), (0, p_pad - rw_proj))),
            w0=row(rw_w0[l]), a0=row(rw_a0[l]), wwa=wwa,
            g2=jnp.pad(rw_g2[l], ((0, g_pad - r_g), (0, 0))).astype(BF16),
            kk=row(rw_kk[l]), ka=row(rw_ka[l]), rk=row(rw_rk[l]),
            lng=row(rw_lnx[l, 0]), lnb=row(rw_lnx[l, 1]),
            tril=jnp.tril(jnp.ones((RW_CHUNK, RW_CHUNK), BF16)),
            bd=_block_diag_ones(LANES, head), r_w=r_w, head=head)
        lw = dict(
            norm1_g=norm1_g[l], w_q=wl[:, :w_da], w_k=wl[:, w_da:2 * w_da], w_v=wl[:, 2 * w_da:3 * w_da],
            w_rw=w_rw, q_gain=jnp.tile(da_qk_g[l, 0], 2 * heads), k_gain=jnp.tile(da_qk_g[l, 1], 2 * heads),
            dk=dk, rwkv=rwkv_prm, rw_proj=rw_proj,
            w_out_da=w_out[l, :w_da].astype(BF16), w_out_rw=w_out[l, w_da:].astype(BF16),
            norm2_g=norm2_g[l], wq_t=peer_wq[l].T.astype(BF16),
            peer_keys=peer_keys[l].reshape(nh * 2, n_keys, qd).astype(BF16),
            peer_u=peer_u[l].astype(BF16), peer_v=peer_v[l].astype(BF16))

        attend_p = functools.partial(_attn_prompt, da_lambda=da_lambda[l], subln_g=da_subln_g[l], heads=heads,
                                     lam_init=lam_init, tq=_tile(x_prompt.shape[1], 512))
        n_pairs = wd // LANES
        xp, kp, vp, sp, shp = _layer(
            xp, attend_p, jnp.zeros((bsz, 1, p_pad), F32), jnp.zeros((bsz, n_pairs, LANES, LANES), F32),
            RW_CHUNK, lam_init, lw)

        attend_s = functools.partial(_attn_sample, cache_k=cache_k, cache_v=cache_v, layer=l,
                                     page_table=page_table, da_lambda=da_lambda[l], subln_g=da_subln_g[l],
                                     heads=heads, lam_init=lam_init,
                                     pp=8 if page_table.shape[1] % 8 == 0 else 4)
        shift_s = jnp.pad(state_shift[l], ((0, 0), (0, p_pad - rw_proj))).reshape(dec_b, 1, p_pad)
        xs, ksm, vsm, ssm, shs = _layer(xs, attend_s, shift_s, _pack_state(state_wkv[l], head), dec_t, lam_init, lw)

        outs[0].append(jnp.transpose(kp.reshape(bsz, heads, 2, dk, -1), (0, 4, 1, 2, 3)))
        outs[1].append(vp.reshape(bsz, -1, heads, 2 * dk))
        outs[2].append(_unpack_state(sp, head))
        outs[3].append(shp)
        outs[4].append(jnp.transpose(ksm.reshape(heads, 2, dk, dec_b, dec_t), (3, 4, 0, 1, 2)))
        outs[5].append(vsm.reshape(dec_b, dec_t, heads, 2 * dk))
        outs[6].append(_unpack_state(ssm, head))
        outs[7].append(shs)
    return (xp, xs) + tuple(jnp.stack(o) for o in outs)
```

```python
import functools
import math

import jax
import jax.numpy as jnp
from jax import lax
from jax.experimental import pallas as pl
from jax.experimental.pallas import tpu as pltpu

F32 = jnp.float32
BF16 = jnp.bfloat16

LANES = 128
RMS_EPS = 1e-6
LNX_EPS = 64e-5
PEER_TOPK = 16
PAGE_SIZE = 128
RW_CHUNK = 64
VMEM_LIMIT = 48 * 1024 * 1024

_NT = (((1,), (1,)), ((), ()))
_TN = (((0,), (0,)), ((), ()))


def _params(sem):
    return pltpu.CompilerParams(dimension_semantics=sem, vmem_limit_bytes=VMEM_LIMIT)


def _mm(a, b):
    return jnp.dot(a.astype(BF16), b.astype(BF16), preferred_element_type=F32)


def _mm_nt(a, b):
    return lax.dot_general(a.astype(BF16), b.astype(BF16), _NT, preferred_element_type=F32)


def _mm_tn(a, b):
    return lax.dot_general(a.astype(BF16), b.astype(BF16), _TN, preferred_element_type=F32)


def _mm_split(a_exact_bf16, b):
    hi = b.astype(BF16)
    lo = (b - hi.astype(F32)).astype(BF16)
    return (jnp.dot(a_exact_bf16, hi, preferred_element_type=F32)
            + jnp.dot(a_exact_bf16, lo, preferred_element_type=F32))


def _mm_split_lhs(a, b_exact_bf16):
    hi = a.astype(BF16)
    lo = (a - hi.astype(F32)).astype(BF16)
    return (jnp.dot(hi, b_exact_bf16, preferred_element_type=F32)
            + jnp.dot(lo, b_exact_bf16, preferred_element_type=F32))


def _block_diag_ones(n, group):
    i = jnp.arange(n)
    return (i[:, None] // group == i[None, :] // group).astype(BF16)


def _rms_kernel(x_ref, g_ref, o_ref):
    x = x_ref[...]
    ms = jnp.mean(x * x, axis=-1, keepdims=True)
    o_ref[...] = (x * lax.rsqrt(ms + RMS_EPS) * g_ref[...]).astype(o_ref.dtype)


def _rms(x, g, tm):
    n, d = x.shape
    return pl.pallas_call(
        _rms_kernel,
        grid=(n // tm,),
        in_specs=[pl.BlockSpec((tm, d), lambda i: (i, 0)), pl.BlockSpec((1, d), lambda i: (0, 0))],
        out_specs=pl.BlockSpec((tm, d), lambda i: (i, 0)),
        out_shape=jax.ShapeDtypeStruct((n, d), BF16),
        compiler_params=_params(("parallel",)),
        name="rms",
    )(x, g.reshape(1, d))


def _proj_plain_kernel(h_ref, w_ref, o_ref):
    o_ref[...] = jnp.dot(h_ref[...], w_ref[...], preferred_element_type=F32)


def _proj_dual_kernel(h_ref, w_ref, o_ref, ob_ref):
    y = jnp.dot(h_ref[...], w_ref[...], preferred_element_type=F32)
    o_ref[...] = y
    ob_ref[...] = y.astype(BF16)


def _qk_normed(h_ref, w_ref, bd_ref, g_ref, group):
    y = jnp.dot(h_ref[...], w_ref[...], preferred_element_type=F32)
    yy = y * y
    bd = bd_ref[...]
    sums = [_mm_split_lhs(yy[:, c * LANES:(c + 1) * LANES], bd) for c in range(y.shape[1] // LANES)]
    ms = jnp.concatenate(sums, axis=1) * (1.0 / group)
    return y * lax.rsqrt(ms + RMS_EPS) * g_ref[...]


def _proj_q_kernel(h_ref, w_ref, bd_ref, g_ref, ob_ref, *, group, scale):
    ob_ref[...] = (_qk_normed(h_ref, w_ref, bd_ref, g_ref, group) * scale).astype(BF16)


def _proj_k_kernel(h_ref, w_ref, bd_ref, g_ref, o_ref, ob_ref, *, group):
    yt = _qk_normed(h_ref, w_ref, bd_ref, g_ref, group).T
    o_ref[0] = yt
    ob_ref[0] = yt.astype(BF16)


def _proj_keys(h, w, gain, group, seqs, tm, tn):
    n, d = h.shape
    ncol = w.shape[1]
    s = n // seqs
    tiles = s // tm
    o_spec = pl.BlockSpec((1, tn, tm), lambda i, j: (i // tiles, j, i % tiles))
    return pl.pallas_call(
        functools.partial(_proj_k_kernel, group=group),
        grid=(n // tm, ncol // tn),
        in_specs=[pl.BlockSpec((tm, d), lambda i, j: (i, 0)), pl.BlockSpec((d, tn), lambda i, j: (0, j)),
                  pl.BlockSpec((LANES, LANES), lambda i, j: (0, 0)), pl.BlockSpec((1, tn), lambda i, j: (0, j))],
        out_specs=(o_spec, o_spec),
        out_shape=(jax.ShapeDtypeStruct((seqs, ncol, s), F32), jax.ShapeDtypeStruct((seqs, ncol, s), BF16)),
        compiler_params=_params(("parallel", "arbitrary")),
        name="proj_k",
    )(h, w, _block_diag_ones(LANES, group), gain.reshape(1, ncol))


def _proj(kind, h, w, tm, tn, gain=None, group=None, scale=None):
    n, d = h.shape
    ncol = w.shape[1]
    grid = (n // tm, ncol // tn)
    h_spec = pl.BlockSpec((tm, d), lambda i, j: (i, 0))
    w_spec = pl.BlockSpec((d, tn), lambda i, j: (0, j))
    o_spec = pl.BlockSpec((tm, tn), lambda i, j: (i, j))
    f32_out = jax.ShapeDtypeStruct((n, ncol), F32)
    bf_out = jax.ShapeDtypeStruct((n, ncol), BF16)
    if kind == "q":
        bd = _block_diag_ones(LANES, group)
        extra = [bd, gain.reshape(1, ncol)]
        extra_specs = [pl.BlockSpec((LANES, LANES), lambda i, j: (0, 0)), pl.BlockSpec((1, tn), lambda i, j: (0, j))]
    else:
        extra, extra_specs = [], []
    if kind == "q":
        body = functools.partial(_proj_q_kernel, group=group, scale=scale)
        out_shape, out_specs = bf_out, o_spec
    elif kind == "dual":
        body = _proj_dual_kernel
        out_shape, out_specs = (f32_out, bf_out), (o_spec, o_spec)
    else:
        body = _proj_plain_kernel
        out_shape, out_specs = f32_out, o_spec
    return pl.pallas_call(
        body,
        grid=grid,
        in_specs=[h_spec, w_spec] + extra_specs,
        out_specs=out_specs,
        out_shape=out_shape,
        compiler_params=_params(("parallel", "arbitrary")),
        name="proj_" + kind,
    )(h, w, *extra)


def _lambda(lam_ref, lam_init):
    ll = lam_ref[...]
    s01 = jnp.sum(ll[0:1, :] * ll[1:2, :], axis=-1, keepdims=True)
    s23 = jnp.sum(ll[2:3, :] * ll[3:4, :], axis=-1, keepdims=True)
    return jnp.exp(s01) - jnp.exp(s23) + lam_init


def _subln(o, g, lam_init):
    ms = jnp.mean(o * o, axis=-1, keepdims=True)
    return (o * lax.rsqrt(ms + RMS_EPS) * g) * (1.0 - lam_init)


def _attn_prompt_kernel(q_ref, k_ref, v_ref, lam_ref, g_ref, o_ref, acc0_ref, acc1_ref, *, tq, tk, dk, lam_init):
    qi = pl.program_id(2)
    q = q_ref[0]
    lane = lax.broadcasted_iota(jnp.int32, q.shape, 1)
    qf = q.astype(F32)
    qc = (jnp.where(lane < dk, qf, 0.0).astype(BF16), jnp.where(lane >= dk, qf, 0.0).astype(BF16))
    accs = (acc0_ref, acc1_ref)
    acc0_ref[...] = jnp.zeros_like(acc0_ref)
    acc1_ref[...] = jnp.zeros_like(acc1_ref)
    row = lax.broadcasted_iota(jnp.int32, (tq, tk), 0)
    col = lax.broadcasted_iota(jnp.int32, (tq, tk), 1)
    n_sub = tq // tk

    def step(j, carry, diag_offset=None):
        kt = k_ref[0, :, pl.ds(pl.multiple_of(j * tk, tk), tk)]
        vb = v_ref[0, pl.ds(pl.multiple_of(j * tk, tk), tk), :]
        new = []
        for c in range(2):
            m, l = carry[2 * c], carry[2 * c + 1]
            s = jnp.dot(qc[c], kt, preferred_element_type=F32)
            if diag_offset is not None:
                s = jnp.where(col + diag_offset <= row, s, -jnp.inf)
            m_new = jnp.maximum(m, jnp.max(s, axis=-1, keepdims=True))
            corr = jnp.exp(m - m_new)
            p = jnp.exp(s - m_new)
            l = l * corr + jnp.sum(p, axis=-1, keepdims=True)
            accs[c][...] = accs[c][...] * corr + jnp.dot(p.astype(BF16), vb, preferred_element_type=F32)
            new += [m_new, l]
        return tuple(new)

    minit = jnp.full((tq, 1), -jnp.inf, F32)
    linit = jnp.zeros((tq, 1), F32)
    carry = lax.fori_loop(0, qi * n_sub, step, (minit, linit, minit, linit))
    for sub in range(n_sub):
        carry = step(qi * n_sub + sub, carry, diag_offset=sub * tk)
    _, l0, _, l1 = carry
    lam = _lambda(lam_ref, lam_init)
    o = acc0_ref[...] / l0 - lam * (acc1_ref[...] / l1)
    o_ref[0] = _subln(o, g_ref[...], lam_init).astype(o_ref.dtype)


def _attn_prompt(qb, kbt, vb, da_lambda, subln_g, heads, lam_init, tq):
    b, s, w = qb.shape
    dv = w // heads
    body = functools.partial(_attn_prompt_kernel, tq=tq, tk=tq, dk=dv // 2, lam_init=lam_init)
    return pl.pallas_call(
        body,
        grid=(b, heads, s // tq),
        in_specs=[
            pl.BlockSpec((1, tq, dv), lambda bi, h, qi: (bi, qi, h)),
            pl.BlockSpec((1, dv, s), lambda bi, h, qi: (bi, h, 0)),
            pl.BlockSpec((1, s, dv), lambda bi, h, qi: (bi, 0, h)),
            pl.BlockSpec(da_lambda.shape, lambda bi, h, qi: (0, 0)),
            pl.BlockSpec((1, dv), lambda bi, h, qi: (0, 0)),
        ],
        out_specs=pl.BlockSpec((1, tq, dv), lambda bi, h, qi: (bi, qi, h)),
        out_shape=jax.ShapeDtypeStruct((b, s, w), BF16),
        scratch_shapes=[pltpu.VMEM((tq, dv), F32), pltpu.VMEM((tq, dv), F32)],
        compiler_params=_params(("parallel", "parallel", "arbitrary")),
        name="attn_prompt",
    )(qb, kbt, vb, da_lambda, subln_g.reshape(1, dv))


def _attn_sample_kernel(pt_ref, qrep_ref, kst_ref, vs_ref, lam_ref, g_ref, *rest, pp, heads, t, dk, lam_init):
    kc_refs = rest[:pp]
    vc_refs = rest[pp:2 * pp]
    o_ref = rest[2 * pp]
    qbd_ref, m_ref, l_ref, acc_ref = rest[2 * pp + 1:]
    del pt_ref
    p_idx = pl.program_id(1)
    rows = 2 * t * heads
    dv = 2 * dk
    w = heads * dv

    @pl.when(p_idx == 0)
    def _():
        r = lax.broadcasted_iota(jnp.int32, (rows, w), 0)
        col = lax.broadcasted_iota(jnp.int32, (rows, w), 1)
        keep = (col // dk) == ((r % heads) * 2 + r // (t * heads))
        qbd_ref[...] = jnp.where(keep, qrep_ref[0].astype(F32), 0.0).astype(BF16)
        m_ref[...] = jnp.full(m_ref.shape, -jnp.inf, F32)
        l_ref[...] = jnp.zeros(l_ref.shape, F32)
        acc_ref[...] = jnp.zeros(acc_ref.shape, F32)

    qbd = qbd_ref[...]

    def online(s, vbs):
        m = m_ref[...]
        m_new = jnp.maximum(m, jnp.max(s, axis=-1, keepdims=True))
        corr = jnp.exp(m - m_new)
        p = jnp.exp(s - m_new)
        l_ref[...] = l_ref[...] * corr + jnp.sum(p, axis=-1, keepdims=True)
        pv = None
        for i, vb in enumerate(vbs):
            n = vb.shape[0]
            d = jnp.dot(p[:, i * n:(i + 1) * n].astype(BF16), vb, preferred_element_type=F32)
            pv = d if pv is None else pv + d
        acc_ref[...] = acc_ref[...] * corr + pv
        m_ref[...] = m_new

    kt = jnp.concatenate([kc[0] for kc in kc_refs], axis=1).astype(BF16)
    vwide = [jnp.concatenate([vc[0, pl.ds(h, PAGE_SIZE, stride=heads), :] for h in range(heads)], axis=1)
             .astype(BF16) for vc in vc_refs]
    online(jnp.dot(qbd, kt, preferred_element_type=F32), vwide)

    @pl.when(p_idx == pl.num_programs(1) - 1)
    def _():
        s = jnp.dot(qbd, kst_ref[0], preferred_element_type=F32)
        r = lax.broadcasted_iota(jnp.int32, s.shape, 0)
        col = lax.broadcasted_iota(jnp.int32, s.shape, 1)
        online(jnp.where(col <= (r // heads) % t, s, -jnp.inf), [vs_ref[0]])
        out = acc_ref[...] / l_ref[...]
        hrow = lax.broadcasted_iota(jnp.int32, (heads, w), 0)
        hcol = lax.broadcasted_iota(jnp.int32, (heads, w), 1) // dv
        pick = hrow == hcol
        lam = _lambda(lam_ref, lam_init)
        toks = []
        for tok_i in range(t):
            maps = []
            for c in range(2):
                base = (c * t + tok_i) * heads
                maps.append(jnp.sum(jnp.where(pick, out[base:base + heads, :], 0.0), axis=0, keepdims=True))
            toks.append(maps[0] - lam * maps[1])
        o = jnp.concatenate(toks, axis=0)
        g = g_ref[...]
        outs = [_subln(o[:, h * dv:(h + 1) * dv], g, lam_init) for h in range(heads)]
        o_ref[0] = jnp.concatenate(outs, axis=1).astype(o_ref.dtype)


def _attn_sample(qb, kbt, vb, cache_k, cache_v, layer, page_table, da_lambda, subln_g, heads, lam_init, pp):
    bd, t, w = qb.shape
    kb = jnp.transpose(kbt.reshape(w, bd, t), (1, 2, 0))
    dv = w // heads
    n_pool = cache_k.shape[1]
    n_pages = page_table.shape[1]
    rows = 2 * t * heads
    qrep = jnp.tile(jnp.repeat(qb, heads, axis=1), (1, 2, 1))
    kst_pad = jnp.pad(jnp.swapaxes(kb, 1, 2), ((0, 0), (0, 0), (0, PAGE_SIZE - t)))
    vs_pad = jnp.pad(vb, ((0, 0), (0, PAGE_SIZE - t), (0, 0)))
    ckt = jnp.transpose(cache_k, (0, 1, 3, 4, 5, 2)).reshape(-1, w, PAGE_SIZE)
    cv = cache_v.reshape(-1, PAGE_SIZE * heads, dv)
    base = layer * n_pool

    def k_spec(i):
        return pl.BlockSpec((1, w, PAGE_SIZE), lambda b, p, pt: (base + pt[b, p * pp + i], 0, 0))

    def v_spec(i):
        return pl.BlockSpec((1, PAGE_SIZE * heads, dv), lambda b, p, pt: (base + pt[b, p * pp + i], 0, 0))

    body = functools.partial(_attn_sample_kernel, pp=pp, heads=heads, t=t, dk=dv // 2, lam_init=lam_init)
    grid_spec = pltpu.PrefetchScalarGridSpec(
        num_scalar_prefetch=1,
        grid=(bd, n_pages // pp),
        in_specs=[
            pl.BlockSpec((1, rows, w), lambda b, p, pt: (b, 0, 0)),
            pl.BlockSpec((1, w, PAGE_SIZE), lambda b, p, pt: (b, 0, 0)),
            pl.BlockSpec((1, PAGE_SIZE, w), lambda b, p, pt: (b, 0, 0)),
            pl.BlockSpec(da_lambda.shape, lambda b, p, pt: (0, 0)),
            pl.BlockSpec((1, dv), lambda b, p, pt: (0, 0)),
        ] + [k_spec(i) for i in range(pp)] + [v_spec(i) for i in range(pp)],
        out_specs=pl.BlockSpec((1, t, w), lambda b, p, pt: (b, 0, 0)),
        scratch_shapes=[pltpu.VMEM((rows, w), BF16), pltpu.VMEM((rows, 1), F32),
                        pltpu.VMEM((rows, 1), F32), pltpu.VMEM((rows, w), F32)],
    )
    return pl.pallas_call(
        body,
        grid_spec=grid_spec,
        out_shape=jax.ShapeDtypeStruct((bd, t, w), BF16),
        compiler_params=_params(("parallel", "arbitrary")),
        name="attn_sample",
    )(page_table, qrep, kst_pad, vs_pad, da_lambda, subln_g.reshape(1, dv), *([ckt] * pp), *([cv] * pp))


def _softplus(z):
    return jnp.maximum(z, 0.0) + jnp.log1p(jnp.exp(-jnp.abs(z)))


def _sigmoid(z):
    return 1.0 / (1.0 + jnp.exp(-z))


def _split_heads(x, first):
    zero = jnp.zeros_like(x)
    return jnp.concatenate([jnp.where(first, x, zero), jnp.where(first, zero, x)], axis=0)


def _rwkv_kernel(rw_ref, shift_ref, s0_ref, mu_ref, w0_ref, a0_ref, wwa_ref, g2_ref, kk_ref, ka_ref,
                 rk_ref, lng_ref, lnb_ref, tril_ref, bd_ref, o_ref, sout_ref, state_ref, prev_ref,
                 *, wd, r_w, t_real, head):
    c_idx = pl.program_id(1)
    cl = RW_CHUNK
    pairs = range(wd // LANES)

    @pl.when(c_idx == 0)
    def _():
        state_ref[...] = s0_ref[0]
        prev_ref[...] = shift_ref[0]

    lane = lax.broadcasted_iota(jnp.int32, (cl, LANES), 1)
    first = lane < head
    tt = lax.broadcasted_iota(jnp.int32, (cl, 2 * cl), 0)
    ss = lax.broadcasted_iota(jnp.int32, (cl, 2 * cl), 1) % cl
    strict = ss < tt
    incl = ss <= tt
    firstw = lax.broadcasted_iota(jnp.int32, (cl, 2 * cl), 1) < cl
    r2 = lax.broadcasted_iota(jnp.int32, (2 * cl, 2 * cl), 0)
    c2 = lax.broadcasted_iota(jnp.int32, (2 * cl, 2 * cl), 1)
    eye = (r2 == c2).astype(F32)
    rs = lax.broadcasted_iota(jnp.int32, (LANES, LANES), 0) // head
    cs = lax.broadcasted_iota(jnp.int32, (LANES, LANES), 1) // head
    same_head = rs == cs
    bd = bd_ref[...]
    n_levels = int(math.log2(cl))
    sls = [slice(p * LANES, (p + 1) * LANES) for p in pairs]

    def chunk(rows, prev_row, st):
        x = rw_ref[0, rows, :]
        rowp = lax.broadcasted_iota(jnp.int32, x.shape, 0)
        prevx = jnp.where(rowp == 0, prev_row, pltpu.roll(x, 1, axis=0))
        xm = x + (prevx - x) * mu_ref[...]
        r = xm[:, 0:wd]
        kr = xm[:, wd:2 * wd]
        vr = xm[:, 2 * wd:3 * wd]
        slab = xm[:, 3 * wd:3 * wd + LANES]
        lane_s = lax.broadcasted_iota(jnp.int32, slab.shape, 1)
        wa = _mm(jnp.where(lane_s < r_w, jnp.tanh(slab), slab), wwa_ref[...])
        gd = xm[:, 3 * wd + LANES:]
        g = _mm(_sigmoid(gd), g2_ref[...])
        w = -_softplus(-(w0_ref[...] + wa[:, :wd])) - 0.5
        lw = -jnp.exp(w)
        a = _sigmoid(a0_ref[...] + wa[:, wd:])
        kk = kr * kk_ref[...]
        kh = kr * (1.0 + (a - 1.0) * ka_ref[...])
        beta_scale = a
        if t_real < cl:
            valid = lax.broadcasted_iota(jnp.int32, (cl, wd), 0) < t_real
            lw = jnp.where(valid, lw, 0.0)
            beta_scale = jnp.where(valid, a, 0.0)
            kh_state = jnp.where(valid, kh, 0.0)
        else:
            kh_state = kh
        cum = _mm_split(tril_ref[...], lw)
        cum_last = cum[cl - 1:cl, :]
        e_neg = jnp.exp(-cum)
        e_end = jnp.exp(cum_last - cum)
        w_end = jnp.exp(cum_last)
        rt_all = r * jnp.exp(cum)
        ex_all = jnp.exp(cum - lw)
        bonus_in = r * kh * rk_ref[...]

        kk_p = [kk[:, sl] for sl in sls]
        nrm = [jnp.sqrt(_mm(k * k, bd)) for k in kk_p]
        kkn = [k / jnp.maximum(n, 1e-12) for k, n in zip(kk_p, nrm)]
        beta = [kkn[p] * beta_scale[:, sls[p]] for p in pairs]
        v_p = [vr[:, sl] for sl in sls]
        lhs = [jnp.concatenate([-ex_all[:, sls[p]] * kkn[p], rt_all[:, sls[p]]], axis=0) for p in pairs]
        rhs = [jnp.concatenate([_split_heads(beta[p] * e_neg[:, sls[p]], first),
                                _split_heads(kh_state[:, sls[p]] * e_neg[:, sls[p]], first)], axis=0)
               for p in pairs]
        m = [_mm_nt(lhs[p], rhs[p]) for p in pairs]
        sa = [_mm_nt(lhs[p], st[p]) for p in pairs]
        aab = [jnp.where(strict, x_[:cl, :2 * cl], 0.0) for x_ in m]
        aak = [jnp.where(strict, x_[:cl, 2 * cl:], 0.0) for x_ in m]
        arbk = [jnp.concatenate([jnp.where(incl, x_[cl:, :2 * cl], 0.0), jnp.where(incl, x_[cl:, 2 * cl:], 0.0)],
                                axis=1) for x_ in m]
        abd = [_split_heads(x_, firstw) for x_ in aab]
        vbd = [_split_heads(x_, first) for x_ in v_p]
        rhs_u = [sa[p][:cl, :] + _mm(aak[p], vbd[p]) for p in pairs]
        tinv = [eye + x_ for x_ in abd]
        xp = [_mm(x_, x_) for x_ in abd]
        for lvl in range(1, n_levels):
            if lvl < n_levels - 1:
                both = [_mm(xp[p], jnp.concatenate([tinv[p], xp[p]], axis=1)) for p in pairs]
                tinv = [tinv[p] + both[p][:, :2 * cl] for p in pairs]
                xp = [both[p][:, 2 * cl:] for p in pairs]
            else:
                tinv = [tinv[p] + _mm(xp[p], tinv[p]) for p in pairs]
        tw = [x_[:cl, :] + x_[cl:, :] for x_ in tinv]
        u = [_mm(tw[p], _split_heads(rhs_u[p], first)) for p in pairs]
        y = [sa[p][cl:, :] + _mm(arbk[p], jnp.concatenate([_split_heads(u[p], first), vbd[p]], axis=0))
             for p in pairs]
        ds = [_mm_tn(jnp.concatenate([u[p], v_p[p]], axis=0),
                     jnp.concatenate([beta[p] * e_end[:, sls[p]], kh_state[:, sls[p]] * e_end[:, sls[p]]], axis=0))
              for p in pairs]
        mean = [_mm(x_, bd) * (1.0 / head) for x_ in y]
        dlt = [y[p] - mean[p] for p in pairs]
        var = [_mm(x_ * x_, bd) * (1.0 / head) for x_ in dlt]
        bonus = [_mm(bonus_in[:, sls[p]], bd) * v_p[p] for p in pairs]
        new_st = []
        for p in pairs:
            sl = sls[p]
            new_st.append(st[p] * w_end[:, sl] + jnp.where(same_head, ds[p], 0.0))
            yn = dlt[p] * lax.rsqrt(var[p] + LNX_EPS) * lng_ref[:, sl] + lnb_ref[:, sl]
            o_ref[0, rows, sl] = ((yn + bonus[p]) * g[:, sl]).astype(o_ref.dtype)
        return x[cl - 1:cl, :], new_st

    prev_row = prev_ref[...]
    st = [state_ref[p] for p in pairs]
    for sc in range(rw_ref.shape[1] // cl):
        prev_row, st = chunk(slice(sc * cl, (sc + 1) * cl), prev_row, st)
    prev_ref[...] = prev_row
    for p in pairs:
        state_ref[p] = st[p]

    @pl.when(c_idx == pl.num_programs(1) - 1)
    def _():
        sout_ref[0] = state_ref[...]


def _rwkv(rw, shift0, s0, prm, t_real):
    b, s, p = rw.shape
    wd = prm["w0"].shape[1]
    n_pairs = wd // LANES
    cl = RW_CHUNK * max(c for c in (4, 2, 1) if s % (c * RW_CHUNK) == 0)
    body = functools.partial(_rwkv_kernel, wd=wd, r_w=prm["r_w"], t_real=t_real, head=prm["head"])

    def const(shape):
        return pl.BlockSpec(shape, lambda bi, ci: (0,) * len(shape))

    names = ["mu", "w0", "a0", "wwa", "g2", "kk", "ka", "rk", "lng", "lnb", "tril", "bd"]
    return pl.pallas_call(
        body,
        grid=(b, s // cl),
        in_specs=[
            pl.BlockSpec((1, cl, p), lambda bi, ci: (bi, ci, 0)),
            pl.BlockSpec((1, 1, p), lambda bi, ci: (bi, 0, 0)),
            pl.BlockSpec((1, n_pairs, LANES, LANES), lambda bi, ci: (bi, 0, 0, 0)),
        ] + [const(prm[nm].shape) for nm in names],
        out_specs=(pl.BlockSpec((1, cl, wd), lambda bi, ci: (bi, ci, 0)),
                   pl.BlockSpec((1, n_pairs, LANES, LANES), lambda bi, ci: (bi, 0, 0, 0))),
        out_shape=(jax.ShapeDtypeStruct((b, s, wd), BF16),
                   jax.ShapeDtypeStruct((b, n_pairs, LANES, LANES), F32)),
        scratch_shapes=[pltpu.VMEM((n_pairs, LANES, LANES), F32), pltpu.VMEM((1, p), F32)],
        compiler_params=_params(("parallel", "arbitrary")),
        name="rwkv",
    )(rw, shift0, s0, *[prm[nm] for nm in names])


def _pack_state(wkv, head):
    b, h = wkv.shape[:2]
    pairs = wkv.reshape(b, h // 2, 2, head, head)
    z = jnp.zeros((b, h // 2, head, head), wkv.dtype)
    top = jnp.concatenate([pairs[:, :, 0], z], axis=-1)
    bot = jnp.concatenate([z, pairs[:, :, 1]], axis=-1)
    return jnp.concatenate([top, bot], axis=-2)


def _unpack_state(sbd, head):
    b, n_pairs = sbd.shape[:2]
    h0 = sbd[:, :, :head, :head]
    h1 = sbd[:, :, head:, head:]
    return jnp.stack([h0, h1], axis=2).reshape(b, 2 * n_pairs, head, head)


def _outproj_kernel(x_ref, oda_ref, orw_ref, w1_ref, w2_ref, g_ref, x1_ref, h_ref, ht_ref):
    y = (jnp.dot(oda_ref[...], w1_ref[...], preferred_element_type=F32)
         + jnp.dot(orw_ref[...], w2_ref[...], preferred_element_type=F32))
    x1 = x_ref[...] + y
    x1_ref[...] = x1
    ms = jnp.mean(x1 * x1, axis=-1, keepdims=True)
    h2 = x1 * lax.rsqrt(ms + RMS_EPS) * g_ref[...]
    h_ref[...] = h2.astype(BF16)
    ht_ref[...] = h2.T.astype(BF16)


def _outproj(x, oda, orw, w1, w2, g, tm):
    n, d = x.shape
    wa, wr = oda.shape[1], orw.shape[1]
    return pl.pallas_call(
        _outproj_kernel,
        grid=(n // tm,),
        in_specs=[
            pl.BlockSpec((tm, d), lambda i: (i, 0)),
            pl.BlockSpec((tm, wa), lambda i: (i, 0)),
            pl.BlockSpec((tm, wr), lambda i: (i, 0)),
            pl.BlockSpec((wa, d), lambda i: (0, 0)),
            pl.BlockSpec((wr, d), lambda i: (0, 0)),
            pl.BlockSpec((1, d), lambda i: (0, 0)),
        ],
        out_specs=(pl.BlockSpec((tm, d), lambda i: (i, 0)), pl.BlockSpec((tm, d), lambda i: (i, 0)),
                   pl.BlockSpec((d, tm), lambda i: (0, i))),
        out_shape=(jax.ShapeDtypeStruct((n, d), F32), jax.ShapeDtypeStruct((n, d), BF16),
                   jax.ShapeDtypeStruct((d, n), BF16)),
        compiler_params=_params(("parallel",)),
        name="outproj",
    )(x, oda, orw, w1, w2, g.reshape(1, d))


def _peer_scores_kernel(ht_ref, wqt_ref, keys_ref, s_ref, q_ref):
    q_ref[...] = jnp.dot(wqt_ref[...], ht_ref[...], preferred_element_type=F32).astype(BF16)
    nhc, _, qd = keys_ref.shape
    for j in range(nhc):
        s_ref[j] = jnp.dot(keys_ref[j], q_ref[j * qd:(j + 1) * qd, :], preferred_element_type=F32)


def _peer_scores(ht, wqt, keys, tn):
    d, n = ht.shape
    nhc, n_keys, qd = keys.shape
    once = pl.Buffered(1)
    return pl.pallas_call(
        _peer_scores_kernel,
        grid=(n // tn,),
        in_specs=[
            pl.BlockSpec((d, tn), lambda t: (0, t)),
            pl.BlockSpec((nhc * qd, d), lambda t: (0, 0), pipeline_mode=once),
            pl.BlockSpec((nhc, n_keys, qd), lambda t: (0, 0, 0), pipeline_mode=once),
        ],
        out_specs=pl.BlockSpec((nhc, n_keys, tn), lambda t: (0, 0, t)),
        out_shape=jax.ShapeDtypeStruct((nhc, n_keys, n), F32),
        scratch_shapes=[pltpu.VMEM((nhc * qd, tn), BF16)],
        compiler_params=_params(("parallel",)),
        name="peer_scores",
    )(ht, wqt, keys)


def _gelu(x):
    return 0.5 * x * (1.0 + lax.erf(x * (2.0 ** -0.5)))


def _take_top(vals, k, payload=None):
    n_rows = vals.shape[0]
    rowid = lax.broadcasted_iota(jnp.int32, vals.shape, 0).astype(F32)
    cur = vals
    top, picked = [], []
    for _ in range(k):
        m = jnp.max(cur, axis=0, keepdims=True)
        sel = jnp.min(jnp.where(cur == m, rowid, float(n_rows)), axis=0, keepdims=True)
        hit = rowid == sel
        top.append(m)
        picked.append(sel if payload is None else jnp.sum(jnp.where(hit, payload, 0.0), axis=0, keepdims=True))
        cur = jnp.where(hit, -jnp.inf, cur)
    return top, picked


def _peer_pairs_kernel(s_ref, code_ref, g_ref, *, n_keys):
    s1 = s_ref[0]
    s2 = s_ref[1]
    a, ia = _take_top(s1, PEER_TOPK)
    b, ib = _take_top(s2, PEER_TOPK)
    b = jnp.concatenate(b, axis=0)
    ib = jnp.concatenate(ib, axis=0)
    counts = [PEER_TOPK // (i + 1) for i in range(PEER_TOPK)]
    cand = jnp.concatenate([a[i] + b[:c, :] for i, c in enumerate(counts)], axis=0)
    code = jnp.concatenate([ia[i] * float(n_keys) + ib[:c, :] for i, c in enumerate(counts)], axis=0)
    best, ids = _take_top(cand, PEER_TOPK, payload=code)
    ex = [jnp.exp(c - best[0]) for c in best]
    z = sum(ex)
    code_ref[...] = jnp.concatenate(ids, axis=0)
    g_ref[...] = jnp.concatenate(ex, axis=0) / z


def _peer_pairs(scores, tn):
    nhc, n_keys, n = scores.shape
    nh = nhc // 2
    blk = pl.BlockSpec((PEER_TOPK, tn), lambda t, h: (h, t))
    shape = jax.ShapeDtypeStruct((nh * PEER_TOPK, n), F32)
    return pl.pallas_call(
        functools.partial(_peer_pairs_kernel, n_keys=n_keys),
        grid=(n // tn, nh),
        in_specs=[pl.BlockSpec((2, n_keys, tn), lambda t, h: (h, 0, t))],
        out_specs=(blk, blk),
        out_shape=(shape, shape),
        compiler_params=_params(("parallel", "arbitrary")),
        name="peer_pairs",
    )(scores)


def _peer_gates_kernel(code_ref, g_ref, o_ref, i1t_ref, i2t_ref, gt_ref, *, n_keys, ib):
    code = code_ref[...].T
    i1 = jnp.floor(code * (1.0 / n_keys))
    i1t_ref[...] = i1
    i2t_ref[...] = code - i1 * float(n_keys)
    gt_ref[...] = g_ref[...].T
    tg, n_pairs = i1t_ref.shape
    key = lax.broadcasted_iota(jnp.int32, (n_keys, n_pairs), 0).astype(F32)

    def body(n, carry):
        i1r = i1t_ref[pl.ds(n, 1), :]
        i2r = i2t_ref[pl.ds(n, 1), :]
        gr = gt_ref[pl.ds(n, 1), :]
        a_t = jnp.where(key == i1r, gr, 0.0).astype(BF16)
        b_t = jnp.where(key == i2r, 1.0, 0.0).astype(BF16)
        grid_n = lax.dot_general(a_t, b_t, _NT, preferred_element_type=F32)
        for blk in range(o_ref.shape[0]):
            o_ref[blk, n] = grid_n[blk * ib:(blk + 1) * ib, :]
        return carry

    lax.fori_loop(0, tg, body, 0, unroll=32)


def _peer_gates(code, g, n_keys, tg, ib):
    n_pairs, n = code.shape
    blk = pl.BlockSpec((n_pairs, tg), lambda t: (0, t))
    body = functools.partial(_peer_gates_kernel, n_keys=n_keys, ib=ib)
    return pl.pallas_call(
        body,
        grid=(n // tg,),
        in_specs=[blk, blk],
        out_specs=pl.BlockSpec((n_keys // ib, tg, ib, n_keys), lambda t: (0, t, 0, 0)),
        out_shape=jax.ShapeDtypeStruct((n_keys // ib, n, ib, n_keys), F32),
        scratch_shapes=[pltpu.VMEM((tg, n_pairs), F32)] * 3,
        compiler_params=_params(("parallel",)),
        name="peer_gates",
    )(code, g)


def _peer_experts_kernel(h_ref, x1_ref, gate_ref, u_ref, v_ref, o_ref, *, ib):
    @pl.when(pl.program_id(1) == 0)
    def _():
        o_ref[...] = x1_ref[...]

    act = _gelu(lax.dot_general(h_ref[...], u_ref[...], _NT, preferred_element_type=F32))
    tn = h_ref.shape[0]
    gates = jnp.concatenate([gate_ref[0, pl.ds(i, tn, stride=ib), :] for i in range(ib)], axis=1)
    o_ref[...] += jnp.dot((gates * act).astype(BF16), v_ref[...], preferred_element_type=F32)


def _peer_experts(h2, x1, gates, u_bf, v_bf, tn, ib):
    n, d = h2.shape
    n_blocks, _, _, n_keys = gates.shape
    ec = ib * n_keys
    body = functools.partial(_peer_experts_kernel, ib=ib)
    return pl.pallas_call(
        body,
        grid=(n // tn, n_blocks),
        in_specs=[
            pl.BlockSpec((tn, d), lambda t, e: (t, 0)),
            pl.BlockSpec((tn, d), lambda t, e: (t, 0), pipeline_mode=pl.Buffered(1)),
            pl.BlockSpec((1, tn * ib, n_keys), lambda t, e: (e, t, 0)),
            pl.BlockSpec((ec, d), lambda t, e: (e, 0)),
            pl.BlockSpec((ec, d), lambda t, e: (e, 0)),
        ],
        out_specs=pl.BlockSpec((tn, d), lambda t, e: (t, 0)),
        out_shape=jax.ShapeDtypeStruct((n, d), F32),
        compiler_params=_params(("parallel", "arbitrary")),
        name="peer_experts",
    )(h2, x1, gates.reshape(n_blocks, n * ib, n_keys), u_bf, v_bf)


def _tile(n, pref):
    return pref if n % pref == 0 else n


def _layer(x, attend, shift0, s0, t_real, lam_init, lw):
    b, t, d = x.shape
    n = b * t
    xf = x.reshape(n, d)
    w_da = lw["w_q"].shape[1]
    tm = _tile(n, 512)
    h1 = _rms(xf, lw["norm1_g"], tm)
    tmp = _tile(n, 1024)
    qb = _proj("q", h1, lw["w_q"], tmp, _tile(w_da, 512), gain=lw["q_gain"], group=lw["dk"], scale=lw["dk"] ** -0.5)
    if t % LANES == 0:
        seqs, tmk = b, _tile(t, 1024)
    else:
        seqs, tmk = 1, n
    k, kbt = _proj_keys(h1, lw["w_k"], lw["k_gain"], lw["dk"], seqs, tmk, _tile(w_da, 512))
    v, vb = _proj("dual", h1, lw["w_v"], tmp, _tile(w_da, 512))
    p_pad = lw["w_rw"].shape[1]
    rw = _proj("plain", h1, lw["w_rw"], tmp, p_pad // 3)

    o_da = attend(qb.reshape(b, t, w_da), kbt, vb.reshape(b, t, w_da))

    rw3 = rw.reshape(b, t, p_pad)
    if t % RW_CHUNK:
        rw3 = jnp.pad(rw3, ((0, 0), (0, RW_CHUNK - t % RW_CHUNK), (0, 0)))
    o_rw, s_out = _rwkv(rw3, shift0, s0, lw["rwkv"], t_real)
    o_rw = o_rw[:, :t].reshape(n, -1)

    x1, h2, h2t = _outproj(xf, o_da.reshape(n, w_da), o_rw, lw["w_out_da"], lw["w_out_rw"], lw["norm2_g"],
                           _tile(n, 256))
    tn = _tile(n, 512)
    scores = _peer_scores(h2t, lw["wq_t"], lw["peer_keys"], tn)
    code, gate = _peer_pairs(scores, tn)
    ib = 8
    gates = _peer_gates(code, gate, scores.shape[1], _tile(n, 128), ib)
    out = _peer_experts(h2, x1, gates, lw["peer_u"], lw["peer_v"], tn, ib)
    return out.reshape(b, t, d), k, v, s_out, rw3[:, t - 1, :lw["rw_proj"]]


def kernel(x_prompt, x_sample, cache_k, cache_v, state_wkv, state_shift, page_table, norm1_g, w_in, da_qk_g,
           da_lambda, da_subln_g, rw_mu, rw_w0, rw_w2, rw_a0, rw_a2, rw_g2, rw_kk, rw_ka, rw_rk, rw_lnx,
           w_out, norm2_g, peer_wq, peer_keys, peer_u, peer_v):
    depth = w_in.shape[0]
    d = x_prompt.shape[-1]
    heads, dk = cache_k.shape[3], cache_k.shape[5]
    w_da = heads * 2 * dk
    rw_heads, head = state_wkv.shape[2], state_wkv.shape[3]
    wd = rw_heads * head
    r_w, r_a, r_g = rw_w2.shape[1], rw_a2.shape[1], rw_g2.shape[1]
    rw_proj = 3 * wd + r_w + r_a + r_g
    g_pad = 2 * LANES
    p_pad = 3 * wd + LANES + g_pad
    assert d == w_da + wd and head * 2 == LANES and wd % LANES == 0 and 2 * dk == LANES
    assert r_w + r_a == LANES and r_g <= g_pad and p_pad % 3 == 0 and (p_pad // 3) % LANES == 0
    assert cache_k.shape[2] == PAGE_SIZE
    nh, _, n_keys, qd = peer_keys.shape[1:]
    bsz = x_prompt.shape[0]
    dec_b, dec_t = x_sample.shape[:2]
    assert dec_t * heads <= PAGE_SIZE and x_prompt.shape[1] % LANES == 0
    assert n_keys & (n_keys - 1) == 0 and peer_u.shape[1] == n_keys * n_keys

    xp, xs = x_prompt, x_sample
    outs = [[] for _ in range(8)]
    for l in range(depth):
        lam_init = 0.8 - 0.6 * math.exp(-0.3 * l)
        wl = w_in[l].astype(BF16)
        w_rw = jnp.pad(wl[:, 3 * w_da:], ((0, 0), (0, p_pad - rw_proj)))
        zero_wa = jnp.zeros((r_w, wd), F32)
        wwa = jnp.concatenate([jnp.concatenate([rw_w2[l], zero_wa], axis=1),
                               jnp.concatenate([zero_wa, rw_a2[l]], axis=1)], axis=0).astype(BF16)
        row = lambda z: z.reshape(1, -1)
        rwkv_prm = dict(
            mu=jnp.pad(row(rw_mu[l]), ((0, 0), (0, p_pad - rw_proj))),
            w0=row(rw_w0[l]), a0=row(rw_a0[l]), wwa=wwa,
            g2=jnp.pad(rw_g2[l], ((0, g_pad - r_g), (0, 0))).astype(BF16),
            kk=row(rw_kk[l]), ka=row(rw_ka[l]), rk=row(rw_rk[l]),
            lng=row(rw_lnx[l, 0]), lnb=row(rw_lnx[l, 1]),
            tril=jnp.tril(jnp.ones((RW_CHUNK, RW_CHUNK), BF16)),
            bd=_block_diag_ones(LANES, head), r_w=r_w, head=head)
        lw = dict(
            norm1_g=norm1_g[l], w_q=wl[:, :w_da], w_k=wl[:, w_da:2 * w_da], w_v=wl[:, 2 * w_da:3 * w_da],
            w_rw=w_rw, q_gain=jnp.tile(da_qk_g[l, 0], 2 * heads), k_gain=jnp.tile(da_qk_g[l, 1], 2 * heads),
            dk=dk, rwkv=rwkv_prm, rw_proj=rw_proj,
            w_out_da=w_out[l, :w_da].astype(BF16), w_out_rw=w_out[l, w_da:].astype(BF16),
            norm2_g=norm2_g[l], wq_t=peer_wq[l].T.astype(BF16),
            peer_keys=peer_keys[l].reshape(nh * 2, n_keys, qd).astype(BF16),
            peer_u=peer_u[l].astype(BF16), peer_v=peer_v[l].astype(BF16))

        attend_p = functools.partial(_attn_prompt, da_lambda=da_lambda[l], subln_g=da_subln_g[l], heads=heads,
                                     lam_init=lam_init, tq=_tile(x_prompt.shape[1], 512))
        n_pairs = wd // LANES
        xp, kp, vp, sp, shp = _layer(
            xp, attend_p, jnp.zeros((bsz, 1, p_pad), F32), jnp.zeros((bsz, n_pairs, LANES, LANES), F32),
            RW_CHUNK, lam_init, lw)

        attend_s = functools.partial(_attn_sample, cache_k=cache_k, cache_v=cache_v, layer=l,
                                     page_table=page_table, da_lambda=da_lambda[l], subln_g=da_subln_g[l],
                                     heads=heads, lam_init=lam_init,
                                     pp=max(c for c in (16, 8, 4, 2, 1) if page_table.shape[1] % c == 0))
        shift_s = jnp.pad(state_shift[l], ((0, 0), (0, p_pad - rw_proj))).reshape(dec_b, 1, p_pad)
        xs, ksm, vsm, ssm, shs = _layer(xs, attend_s, shift_s, _pack_state(state_wkv[l], head), dec_t, lam_init, lw)

        outs[0].append(jnp.transpose(kp.reshape(bsz, heads, 2, dk, -1), (0, 4, 1, 2, 3)))
        outs[1].append(vp.reshape(bsz, -1, heads, 2 * dk))
        outs[2].append(_unpack_state(sp, head))
        outs[3].append(shp)
        outs[4].append(jnp.transpose(ksm.reshape(heads, 2, dk, dec_b, dec_t), (3, 4, 0, 1, 2)))
        outs[5].append(vsm.reshape(dec_b, dec_t, heads, 2 * dk))
        outs[6].append(_unpack_state(ssm, head))
        outs[7].append(shs)
    return (xp, xs) + tuple(jnp.stack(o) for o in outs)
```

```python
import functools
import math

import jax
import jax.numpy as jnp
from jax import lax
from jax.experimental import pallas as pl
from jax.experimental.pallas import tpu as pltpu

F32 = jnp.float32
BF16 = jnp.bfloat16

LANES = 128
RMS_EPS = 1e-6
LNX_EPS = 64e-5
PEER_TOPK = 16
PAGE_SIZE = 128
RW_CHUNK = 64
VMEM_LIMIT = 48 * 1024 * 1024

_NT = (((1,), (1,)), ((), ()))
_TN = (((0,), (0,)), ((), ()))


def _params(sem):
    return pltpu.CompilerParams(dimension_semantics=sem, vmem_limit_bytes=VMEM_LIMIT)


def _mm(a, b):
    return jnp.dot(a.astype(BF16), b.astype(BF16), preferred_element_type=F32)


def _mm_nt(a, b):
    return lax.dot_general(a.astype(BF16), b.astype(BF16), _NT, preferred_element_type=F32)


def _mm_tn(a, b):
    return lax.dot_general(a.astype(BF16), b.astype(BF16), _TN, preferred_element_type=F32)


def _mm_split(a_exact_bf16, b):
    hi = b.astype(BF16)
    lo = (b - hi.astype(F32)).astype(BF16)
    return (jnp.dot(a_exact_bf16, hi, preferred_element_type=F32)
            + jnp.dot(a_exact_bf16, lo, preferred_element_type=F32))


def _mm_split_lhs(a, b_exact_bf16):
    hi = a.astype(BF16)
    lo = (a - hi.astype(F32)).astype(BF16)
    return (jnp.dot(hi, b_exact_bf16, preferred_element_type=F32)
            + jnp.dot(lo, b_exact_bf16, preferred_element_type=F32))


def _block_diag_ones(n, group):
    i = jnp.arange(n)
    return (i[:, None] // group == i[None, :] // group).astype(BF16)


def _rms_kernel(x_ref, g_ref, o_ref):
    x = x_ref[...]
    ms = jnp.mean(x * x, axis=-1, keepdims=True)
    o_ref[...] = (x * lax.rsqrt(ms + RMS_EPS) * g_ref[...]).astype(o_ref.dtype)


def _rms(x, g, tm):
    n, d = x.shape
    return pl.pallas_call(
        _rms_kernel,
        grid=(n // tm,),
        in_specs=[pl.BlockSpec((tm, d), lambda i: (i, 0)), pl.BlockSpec((1, d), lambda i: (0, 0))],
        out_specs=pl.BlockSpec((tm, d), lambda i: (i, 0)),
        out_shape=jax.ShapeDtypeStruct((n, d), BF16),
        compiler_params=_params(("parallel",)),
        name="rms",
    )(x, g.reshape(1, d))


def _proj_plain_kernel(h_ref, w_ref, o_ref):
    o_ref[...] = jnp.dot(h_ref[...], w_ref[...], preferred_element_type=F32)


def _proj_dual_kernel(h_ref, w_ref, o_ref, ob_ref):
    y = jnp.dot(h_ref[...], w_ref[...], preferred_element_type=F32)
    o_ref[...] = y
    ob_ref[...] = y.astype(BF16)


def _qk_normed(h_ref, w_ref, bd_ref, g_ref, group):
    y = jnp.dot(h_ref[...], w_ref[...], preferred_element_type=F32)
    yy = y * y
    bd = bd_ref[...]
    sums = [_mm_split_lhs(yy[:, c * LANES:(c + 1) * LANES], bd) for c in range(y.shape[1] // LANES)]
    ms = jnp.concatenate(sums, axis=1) * (1.0 / group)
    return y * lax.rsqrt(ms + RMS_EPS) * g_ref[...]


def _proj_q_kernel(h_ref, w_ref, bd_ref, g_ref, ob_ref, *, group, scale):
    ob_ref[...] = (_qk_normed(h_ref, w_ref, bd_ref, g_ref, group) * scale).astype(BF16)


def _proj_k_kernel(h_ref, w_ref, bd_ref, g_ref, o_ref, ob_ref, *, group):
    yt = _qk_normed(h_ref, w_ref, bd_ref, g_ref, group).T
    o_ref[0] = yt
    ob_ref[0] = yt.astype(BF16)


def _proj_keys(h, w, gain, group, seqs, tm, tn):
    n, d = h.shape
    ncol = w.shape[1]
    s = n // seqs
    tiles = s // tm
    o_spec = pl.BlockSpec((1, tn, tm), lambda i, j: (i // tiles, j, i % tiles))
    return pl.pallas_call(
        functools.partial(_proj_k_kernel, group=group),
        grid=(n // tm, ncol // tn),
        in_specs=[pl.BlockSpec((tm, d), lambda i, j: (i, 0)), pl.BlockSpec((d, tn), lambda i, j: (0, j)),
                  pl.BlockSpec((LANES, LANES), lambda i, j: (0, 0)), pl.BlockSpec((1, tn), lambda i, j: (0, j))],
        out_specs=(o_spec, o_spec),
        out_shape=(jax.ShapeDtypeStruct((seqs, ncol, s), F32), jax.ShapeDtypeStruct((seqs, ncol, s), BF16)),
        compiler_params=_params(("parallel", "arbitrary")),
        name="proj_k",
    )(h, w, _block_diag_ones(LANES, group), gain.reshape(1, ncol))


def _proj(kind, h, w, tm, tn, gain=None, group=None, scale=None):
    n, d = h.shape
    ncol = w.shape[1]
    grid = (n // tm, ncol // tn)
    h_spec = pl.BlockSpec((tm, d), lambda i, j: (i, 0))
    w_spec = pl.BlockSpec((d, tn), lambda i, j: (0, j))
    o_spec = pl.BlockSpec((tm, tn), lambda i, j: (i, j))
    f32_out = jax.ShapeDtypeStruct((n, ncol), F32)
    bf_out = jax.ShapeDtypeStruct((n, ncol), BF16)
    if kind == "q":
        bd = _block_diag_ones(LANES, group)
        extra = [bd, gain.reshape(1, ncol)]
        extra_specs = [pl.BlockSpec((LANES, LANES), lambda i, j: (0, 0)), pl.BlockSpec((1, tn), lambda i, j: (0, j))]
    else:
        extra, extra_specs = [], []
    if kind == "q":
        body = functools.partial(_proj_q_kernel, group=group, scale=scale)
        out_shape, out_specs = bf_out, o_spec
    elif kind == "dual":
        body = _proj_dual_kernel
        out_shape, out_specs = (f32_out, bf_out), (o_spec, o_spec)
    else:
        body = _proj_plain_kernel
        out_shape, out_specs = f32_out, o_spec
    return pl.pallas_call(
        body,
        grid=grid,
        in_specs=[h_spec, w_spec] + extra_specs,
        out_specs=out_specs,
        out_shape=out_shape,
        compiler_params=_params(("parallel", "arbitrary")),
        name="proj_" + kind,
    )(h, w, *extra)


def _lambda(lam_ref, lam_init):
    ll = lam_ref[...]
    s01 = jnp.sum(ll[0:1, :] * ll[1:2, :], axis=-1, keepdims=True)
    s23 = jnp.sum(ll[2:3, :] * ll[3:4, :], axis=-1, keepdims=True)
    return jnp.exp(s01) - jnp.exp(s23) + lam_init


def _subln(o, g, lam_init):
    ms = jnp.mean(o * o, axis=-1, keepdims=True)
    return (o * lax.rsqrt(ms + RMS_EPS) * g) * (1.0 - lam_init)


def _attn_prompt_kernel(q_ref, k_ref, v_ref, lam_ref, g_ref, o_ref, acc0_ref, acc1_ref, *, tq, tk, dk, lam_init):
    qi = pl.program_id(2)
    q = q_ref[0]
    lane = lax.broadcasted_iota(jnp.int32, q.shape, 1)
    qf = q.astype(F32)
    qc = (jnp.where(lane < dk, qf, 0.0).astype(BF16), jnp.where(lane >= dk, qf, 0.0).astype(BF16))
    accs = (acc0_ref, acc1_ref)
    acc0_ref[...] = jnp.zeros_like(acc0_ref)
    acc1_ref[...] = jnp.zeros_like(acc1_ref)
    row = lax.broadcasted_iota(jnp.int32, (tq, tk), 0)
    col = lax.broadcasted_iota(jnp.int32, (tq, tk), 1)
    n_sub = tq // tk

    def step(j, carry, diag_offset=None):
        kt = k_ref[0, :, pl.ds(pl.multiple_of(j * tk, tk), tk)]
        vb = v_ref[0, pl.ds(pl.multiple_of(j * tk, tk), tk), :]
        new = []
        for c in range(2):
            m, l = carry[2 * c], carry[2 * c + 1]
            s = jnp.dot(qc[c], kt, preferred_element_type=F32)
            if diag_offset is not None:
                s = jnp.where(col + diag_offset <= row, s, -jnp.inf)
            m_new = jnp.maximum(m, jnp.max(s, axis=-1, keepdims=True))
            corr = jnp.exp(m - m_new)
            p = jnp.exp(s - m_new)
            l = l * corr + jnp.sum(p, axis=-1, keepdims=True)
            accs[c][...] = accs[c][...] * corr + jnp.dot(p.astype(BF16), vb, preferred_element_type=F32)
            new += [m_new, l]
        return tuple(new)

    minit = jnp.full((tq, 1), -jnp.inf, F32)
    linit = jnp.zeros((tq, 1), F32)
    carry = lax.fori_loop(0, qi * n_sub, step, (minit, linit, minit, linit))
    for sub in range(n_sub):
        carry = step(qi * n_sub + sub, carry, diag_offset=sub * tk)
    _, l0, _, l1 = carry
    lam = _lambda(lam_ref, lam_init)
    o = acc0_ref[...] / l0 - lam * (acc1_ref[...] / l1)
    o_ref[0] = _subln(o, g_ref[...], lam_init).astype(o_ref.dtype)


def _attn_prompt(qb, kbt, vb, da_lambda, subln_g, heads, lam_init, tq):
    b, s, w = qb.shape
    dv = w // heads
    body = functools.partial(_attn_prompt_kernel, tq=tq, tk=tq, dk=dv // 2, lam_init=lam_init)
    return pl.pallas_call(
        body,
        grid=(b, heads, s // tq),
        in_specs=[
            pl.BlockSpec((1, tq, dv), lambda bi, h, qi: (bi, qi, h)),
            pl.BlockSpec((1, dv, s), lambda bi, h, qi: (bi, h, 0)),
            pl.BlockSpec((1, s, dv), lambda bi, h, qi: (bi, 0, h)),
            pl.BlockSpec(da_lambda.shape, lambda bi, h, qi: (0, 0)),
            pl.BlockSpec((1, dv), lambda bi, h, qi: (0, 0)),
        ],
        out_specs=pl.BlockSpec((1, tq, dv), lambda bi, h, qi: (bi, qi, h)),
        out_shape=jax.ShapeDtypeStruct((b, s, w), BF16),
        scratch_shapes=[pltpu.VMEM((tq, dv), F32), pltpu.VMEM((tq, dv), F32)],
        compiler_params=_params(("parallel", "parallel", "arbitrary")),
        name="attn_prompt",
    )(qb, kbt, vb, da_lambda, subln_g.reshape(1, dv))


def _attn_sample_kernel(pt_ref, qrep_ref, kst_ref, vs_ref, lam_ref, g_ref, *rest, pp, heads, t, dk, lam_init):
    kc_refs = rest[:pp]
    vc_refs = rest[pp:2 * pp]
    o_ref = rest[2 * pp]
    qbd_ref, m_ref, l_ref, acc_ref = rest[2 * pp + 1:]
    del pt_ref
    p_idx = pl.program_id(1)
    rows = 2 * t * heads
    dv = 2 * dk
    w = heads * dv

    @pl.when(p_idx == 0)
    def _():
        r = lax.broadcasted_iota(jnp.int32, (rows, w), 0)
        col = lax.broadcasted_iota(jnp.int32, (rows, w), 1)
        keep = (col // dk) == ((r % heads) * 2 + r // (t * heads))
        qbd_ref[...] = jnp.where(keep, qrep_ref[0].astype(F32), 0.0).astype(BF16)
        m_ref[...] = jnp.full(m_ref.shape, -jnp.inf, F32)
        l_ref[...] = jnp.zeros(l_ref.shape, F32)
        acc_ref[...] = jnp.zeros(acc_ref.shape, F32)

    qbd = qbd_ref[...]

    def online(s, vbs):
        m = m_ref[...]
        m_new = jnp.maximum(m, jnp.max(s, axis=-1, keepdims=True))
        corr = jnp.exp(m - m_new)
        p = jnp.exp(s - m_new)
        l_ref[...] = l_ref[...] * corr + jnp.sum(p, axis=-1, keepdims=True)
        pv = None
        for i, vb in enumerate(vbs):
            n = vb.shape[0]
            d = jnp.dot(p[:, i * n:(i + 1) * n].astype(BF16), vb, preferred_element_type=F32)
            pv = d if pv is None else pv + d
        acc_ref[...] = acc_ref[...] * corr + pv
        m_ref[...] = m_new

    kt = jnp.concatenate([kc[0] for kc in kc_refs], axis=1).astype(BF16)
    vwide = [jnp.concatenate([vc[0, pl.ds(h, PAGE_SIZE, stride=heads), :] for h in range(heads)], axis=1)
             .astype(BF16) for vc in vc_refs]
    online(jnp.dot(qbd, kt, preferred_element_type=F32), vwide)

    @pl.when(p_idx == pl.num_programs(1) - 1)
    def _():
        s = jnp.dot(qbd, kst_ref[0], preferred_element_type=F32)
        r = lax.broadcasted_iota(jnp.int32, s.shape, 0)
        col = lax.broadcasted_iota(jnp.int32, s.shape, 1)
        online(jnp.where(col <= (r // heads) % t, s, -jnp.inf), [vs_ref[0]])
        out = acc_ref[...] / l_ref[...]
        hrow = lax.broadcasted_iota(jnp.int32, (heads, w), 0)
        hcol = lax.broadcasted_iota(jnp.int32, (heads, w), 1) // dv
        pick = hrow == hcol
        lam = _lambda(lam_ref, lam_init)
        toks = []
        for tok_i in range(t):
            maps = []
            for c in range(2):
                base = (c * t + tok_i) * heads
                maps.append(jnp.sum(jnp.where(pick, out[base:base + heads, :], 0.0), axis=0, keepdims=True))
            toks.append(maps[0] - lam * maps[1])
        o = jnp.concatenate(toks, axis=0)
        g = g_ref[...]
        outs = [_subln(o[:, h * dv:(h + 1) * dv], g, lam_init) for h in range(heads)]
        o_ref[0] = jnp.concatenate(outs, axis=1).astype(o_ref.dtype)


def _attn_sample(qb, kbt, vb, cache_k, cache_v, layer, page_table, da_lambda, subln_g, heads, lam_init, pp):
    bd, t, w = qb.shape
    kb = jnp.transpose(kbt.reshape(w, bd, t), (1, 2, 0))
    dv = w // heads
    n_pool = cache_k.shape[1]
    n_pages = page_table.shape[1]
    rows = 2 * t * heads
    qrep = jnp.tile(jnp.repeat(qb, heads, axis=1), (1, 2, 1))
    kst_pad = jnp.pad(jnp.swapaxes(kb, 1, 2), ((0, 0), (0, 0), (0, PAGE_SIZE - t)))
    vs_pad = jnp.pad(vb, ((0, 0), (0, PAGE_SIZE - t), (0, 0)))
    ckt = jnp.transpose(cache_k, (0, 1, 3, 4, 5, 2)).reshape(-1, w, PAGE_SIZE)
    cv = cache_v.reshape(-1, PAGE_SIZE * heads, dv)
    base = layer * n_pool

    def k_spec(i):
        return pl.BlockSpec((1, w, PAGE_SIZE), lambda b, p, pt: (base + pt[b, p * pp + i], 0, 0))

    def v_spec(i):
        return pl.BlockSpec((1, PAGE_SIZE * heads, dv), lambda b, p, pt: (base + pt[b, p * pp + i], 0, 0))

    body = functools.partial(_attn_sample_kernel, pp=pp, heads=heads, t=t, dk=dv // 2, lam_init=lam_init)
    grid_spec = pltpu.PrefetchScalarGridSpec(
        num_scalar_prefetch=1,
        grid=(bd, n_pages // pp),
        in_specs=[
            pl.BlockSpec((1, rows, w), lambda b, p, pt: (b, 0, 0)),
            pl.BlockSpec((1, w, PAGE_SIZE), lambda b, p, pt: (b, 0, 0)),
            pl.BlockSpec((1, PAGE_SIZE, w), lambda b, p, pt: (b, 0, 0)),
            pl.BlockSpec(da_lambda.shape, lambda b, p, pt: (0, 0)),
            pl.BlockSpec((1, dv), lambda b, p, pt: (0, 0)),
        ] + [k_spec(i) for i in range(pp)] + [v_spec(i) for i in range(pp)],
        out_specs=pl.BlockSpec((1, t, w), lambda b, p, pt: (b, 0, 0)),
        scratch_shapes=[pltpu.VMEM((rows, w), BF16), pltpu.VMEM((rows, 1), F32),
                        pltpu.VMEM((rows, 1), F32), pltpu.VMEM((rows, w), F32)],
    )
    return pl.pallas_call(
        body,
        grid_spec=grid_spec,
        out_shape=jax.ShapeDtypeStruct((bd, t, w), BF16),
        compiler_params=_params(("parallel", "arbitrary")),
        name="attn_sample",
    )(page_table, qrep, kst_pad, vs_pad, da_lambda, subln_g.reshape(1, dv), *([ckt] * pp), *([cv] * pp))


def _softplus(z):
    return jnp.maximum(z, 0.0) + jnp.log1p(jnp.exp(-jnp.abs(z)))


def _sigmoid(z):
    return 1.0 / (1.0 + jnp.exp(-z))


def _split_heads(x, first):
    zero = jnp.zeros_like(x)
    return jnp.concatenate([jnp.where(first, x, zero), jnp.where(first, zero, x)], axis=0)


def _rwkv_kernel(rw_ref, shift_ref, s0_ref, mu_ref, w0_ref, a0_ref, wwa_ref, g2_ref, kk_ref, ka_ref,
                 rk_ref, lng_ref, lnb_ref, tril_ref, bd_ref, o_ref, sout_ref, state_ref, prev_ref,
                 *, wd, r_w, t_real, head):
    c_idx = pl.program_id(1)
    cl = RW_CHUNK
    pairs = range(wd // LANES)

    @pl.when(c_idx == 0)
    def _():
        state_ref[...] = s0_ref[0]
        prev_ref[...] = shift_ref[0]

    lane = lax.broadcasted_iota(jnp.int32, (cl, LANES), 1)
    first = lane < head
    tt = lax.broadcasted_iota(jnp.int32, (cl, 2 * cl), 0)
    ss = lax.broadcasted_iota(jnp.int32, (cl, 2 * cl), 1) % cl
    strict = ss < tt
    incl = ss <= tt
    firstw = lax.broadcasted_iota(jnp.int32, (cl, 2 * cl), 1) < cl
    r2 = lax.broadcasted_iota(jnp.int32, (2 * cl, 2 * cl), 0)
    c2 = lax.broadcasted_iota(jnp.int32, (2 * cl, 2 * cl), 1)
    eye = (r2 == c2).astype(F32)
    rs = lax.broadcasted_iota(jnp.int32, (LANES, LANES), 0) // head
    cs = lax.broadcasted_iota(jnp.int32, (LANES, LANES), 1) // head
    same_head = rs == cs
    bd = bd_ref[...]
    n_levels = int(math.log2(cl))
    sls = [slice(p * LANES, (p + 1) * LANES) for p in pairs]

    def chunk(rows, prev_row, st):
        x = rw_ref[0, rows, :]
        rowp = lax.broadcasted_iota(jnp.int32, x.shape, 0)
        prevx = jnp.where(rowp == 0, prev_row, pltpu.roll(x, 1, axis=0))
        xm = x + (prevx - x) * mu_ref[...]
        r = xm[:, 0:wd]
        kr = xm[:, wd:2 * wd]
        vr = xm[:, 2 * wd:3 * wd]
        slab = xm[:, 3 * wd:3 * wd + LANES]
        lane_s = lax.broadcasted_iota(jnp.int32, slab.shape, 1)
        wa = _mm(jnp.where(lane_s < r_w, jnp.tanh(slab), slab), wwa_ref[...])
        gd = xm[:, 3 * wd + LANES:]
        g = _mm(_sigmoid(gd), g2_ref[...])
        w = -_softplus(-(w0_ref[...] + wa[:, :wd])) - 0.5
        lw = -jnp.exp(w)
        a = _sigmoid(a0_ref[...] + wa[:, wd:])
        kk = kr * kk_ref[...]
        kh = kr * (1.0 + (a - 1.0) * ka_ref[...])
        beta_scale = a
        if t_real < cl:
            valid = lax.broadcasted_iota(jnp.int32, (cl, wd), 0) < t_real
            lw = jnp.where(valid, lw, 0.0)
            beta_scale = jnp.where(valid, a, 0.0)
            kh_state = jnp.where(valid, kh, 0.0)
        else:
            kh_state = kh
        cum = _mm_split(tril_ref[...], lw)
        cum_last = cum[cl - 1:cl, :]
        e_neg = jnp.exp(-cum)
        e_end = jnp.exp(cum_last - cum)
        w_end = jnp.exp(cum_last)
        rt_all = r * jnp.exp(cum)
        ex_all = jnp.exp(cum - lw)
        bonus_in = r * kh * rk_ref[...]

        kk_p = [kk[:, sl] for sl in sls]
        nrm = [jnp.sqrt(_mm(k * k, bd)) for k in kk_p]
        kkn = [k / jnp.maximum(n, 1e-12) for k, n in zip(kk_p, nrm)]
        beta = [kkn[p] * beta_scale[:, sls[p]] for p in pairs]
        v_p = [vr[:, sl] for sl in sls]
        lhs = [jnp.concatenate([-ex_all[:, sls[p]] * kkn[p], rt_all[:, sls[p]]], axis=0) for p in pairs]
        rhs = [jnp.concatenate([_split_heads(beta[p] * e_neg[:, sls[p]], first),
                                _split_heads(kh_state[:, sls[p]] * e_neg[:, sls[p]], first)], axis=0)
               for p in pairs]
        m = [_mm_nt(lhs[p], rhs[p]) for p in pairs]
        sa = [_mm_nt(lhs[p], st[p]) for p in pairs]
        aab = [jnp.where(strict, x_[:cl, :2 * cl], 0.0) for x_ in m]
        aak = [jnp.where(strict, x_[:cl, 2 * cl:], 0.0) for x_ in m]
        arbk = [jnp.concatenate([jnp.where(incl, x_[cl:, :2 * cl], 0.0), jnp.where(incl, x_[cl:, 2 * cl:], 0.0)],
                                axis=1) for x_ in m]
        abd = [_split_heads(x_, firstw) for x_ in aab]
        vbd = [_split_heads(x_, first) for x_ in v_p]
        rhs_u = [sa[p][:cl, :] + _mm(aak[p], vbd[p]) for p in pairs]
        tinv = [eye + x_ for x_ in abd]
        xp = [_mm(x_, x_) for x_ in abd]
        for lvl in range(1, n_levels):
            if lvl < n_levels - 1:
                both = [_mm(xp[p], jnp.concatenate([tinv[p], xp[p]], axis=1)) for p in pairs]
                tinv = [tinv[p] + both[p][:, :2 * cl] for p in pairs]
                xp = [both[p][:, 2 * cl:] for p in pairs]
            else:
                tinv = [tinv[p] + _mm(xp[p], tinv[p]) for p in pairs]
        tw = [x_[:cl, :] + x_[cl:, :] for x_ in tinv]
        u = [_mm(tw[p], _split_heads(rhs_u[p], first)) for p in pairs]
        y = [sa[p][cl:, :] + _mm(arbk[p], jnp.concatenate([_split_heads(u[p], first), vbd[p]], axis=0))
             for p in pairs]
        ds = [_mm_tn(jnp.concatenate([u[p], v_p[p]], axis=0),
                     jnp.concatenate([beta[p] * e_end[:, sls[p]], kh_state[:, sls[p]] * e_end[:, sls[p]]], axis=0))
              for p in pairs]
        mean = [_mm(x_, bd) * (1.0 / head) for x_ in y]
        dlt = [y[p] - mean[p] for p in pairs]
        var = [_mm(x_ * x_, bd) * (1.0 / head) for x_ in dlt]
        bonus = [_mm(bonus_in[:, sls[p]], bd) * v_p[p] for p in pairs]
        new_st = []
        for p in pairs:
            sl = sls[p]
            new_st.append(st[p] * w_end[:, sl] + jnp.where(same_head, ds[p], 0.0))
            yn = dlt[p] * lax.rsqrt(var[p] + LNX_EPS) * lng_ref[:, sl] + lnb_ref[:, sl]
            o_ref[0, rows, sl] = ((yn + bonus[p]) * g[:, sl]).astype(o_ref.dtype)
        return x[cl - 1:cl, :], new_st

    prev_row = prev_ref[...]
    st = [state_ref[p] for p in pairs]
    for sc in range(rw_ref.shape[1] // cl):
        prev_row, st = chunk(slice(sc * cl, (sc + 1) * cl), prev_row, st)
    prev_ref[...] = prev_row
    for p in pairs:
        state_ref[p] = st[p]

    @pl.when(c_idx == pl.num_programs(1) - 1)
    def _():
        sout_ref[0] = state_ref[...]


def _rwkv(rw, shift0, s0, prm, t_real):
    b, s, p = rw.shape
    wd = prm["w0"].shape[1]
    n_pairs = wd // LANES
    cl = RW_CHUNK * max(c for c in (4, 2, 1) if s % (c * RW_CHUNK) == 0)
    body = functools.partial(_rwkv_kernel, wd=wd, r_w=prm["r_w"], t_real=t_real, head=prm["head"])

    def const(shape):
        return pl.BlockSpec(shape, lambda bi, ci: (0,) * len(shape))

    names = ["mu", "w0", "a0", "wwa", "g2", "kk", "ka", "rk", "lng", "lnb", "tril", "bd"]
    return pl.pallas_call(
        body,
        grid=(b, s // cl),
        in_specs=[
            pl.BlockSpec((1, cl, p), lambda bi, ci: (bi, ci, 0)),
            pl.BlockSpec((1, 1, p), lambda bi, ci: (bi, 0, 0)),
            pl.BlockSpec((1, n_pairs, LANES, LANES), lambda bi, ci: (bi, 0, 0, 0)),
        ] + [const(prm[nm].shape) for nm in names],
        out_specs=(pl.BlockSpec((1, cl, wd), lambda bi, ci: (bi, ci, 0)),
                   pl.BlockSpec((1, n_pairs, LANES, LANES), lambda bi, ci: (bi, 0, 0, 0))),
        out_shape=(jax.ShapeDtypeStruct((b, s, wd), BF16),
                   jax.ShapeDtypeStruct((b, n_pairs, LANES, LANES), F32)),
        scratch_shapes=[pltpu.VMEM((n_pairs, LANES, LANES), F32), pltpu.VMEM((1, p), F32)],
        compiler_params=_params(("parallel", "arbitrary")),
        name="rwkv",
    )(rw, shift0, s0, *[prm[nm] for nm in names])


def _pack_state(wkv, head):
    b, h = wkv.shape[:2]
    pairs = wkv.reshape(b, h // 2, 2, head, head)
    z = jnp.zeros((b, h // 2, head, head), wkv.dtype)
    top = jnp.concatenate([pairs[:, :, 0], z], axis=-1)
    bot = jnp.concatenate([z, pairs[:, :, 1]], axis=-1)
    return jnp.concatenate([top, bot], axis=-2)


def _unpack_state(sbd, head):
    b, n_pairs = sbd.shape[:2]
    h0 = sbd[:, :, :head, :head]
    h1 = sbd[:, :, head:, head:]
    return jnp.stack([h0, h1], axis=2).reshape(b, 2 * n_pairs, head, head)


def _outproj_kernel(x_ref, oda_ref, orw_ref, w1_ref, w2_ref, g_ref, x1_ref, h_ref, ht_ref):
    y = (jnp.dot(oda_ref[...], w1_ref[...], preferred_element_type=F32)
         + jnp.dot(orw_ref[...], w2_ref[...], preferred_element_type=F32))
    x1 = x_ref[...] + y
    x1_ref[...] = x1
    ms = jnp.mean(x1 * x1, axis=-1, keepdims=True)
    h2 = x1 * lax.rsqrt(ms + RMS_EPS) * g_ref[...]
    h_ref[...] = h2.astype(BF16)
    ht_ref[...] = h2.T.astype(BF16)


def _outproj(x, oda, orw, w1, w2, g, tm):
    n, d = x.shape
    wa, wr = oda.shape[1], orw.shape[1]
    return pl.pallas_call(
        _outproj_kernel,
        grid=(n // tm,),
        in_specs=[
            pl.BlockSpec((tm, d), lambda i: (i, 0)),
            pl.BlockSpec((tm, wa), lambda i: (i, 0)),
            pl.BlockSpec((tm, wr), lambda i: (i, 0)),
            pl.BlockSpec((wa, d), lambda i: (0, 0)),
            pl.BlockSpec((wr, d), lambda i: (0, 0)),
            pl.BlockSpec((1, d), lambda i: (0, 0)),
        ],
        out_specs=(pl.BlockSpec((tm, d), lambda i: (i, 0)), pl.BlockSpec((tm, d), lambda i: (i, 0)),
                   pl.BlockSpec((d, tm), lambda i: (0, i))),
        out_shape=(jax.ShapeDtypeStruct((n, d), F32), jax.ShapeDtypeStruct((n, d), BF16),
                   jax.ShapeDtypeStruct((d, n), BF16)),
        compiler_params=_params(("parallel",)),
        name="outproj",
    )(x, oda, orw, w1, w2, g.reshape(1, d))


def _peer_scores_kernel(ht_ref, wqt_ref, keys_ref, s_ref, q_ref):
    q_ref[...] = jnp.dot(wqt_ref[...], ht_ref[...], preferred_element_type=F32).astype(BF16)
    nhc, _, qd = keys_ref.shape
    for j in range(nhc):
        s_ref[j] = jnp.dot(keys_ref[j], q_ref[j * qd:(j + 1) * qd, :], preferred_element_type=F32)


def _peer_scores(ht, wqt, keys, tn):
    d, n = ht.shape
    nhc, n_keys, qd = keys.shape
    once = pl.Buffered(1)
    return pl.pallas_call(
        _peer_scores_kernel,
        grid=(n // tn,),
        in_specs=[
            pl.BlockSpec((d, tn), lambda t: (0, t)),
            pl.BlockSpec((nhc * qd, d), lambda t: (0, 0), pipeline_mode=once),
            pl.BlockSpec((nhc, n_keys, qd), lambda t: (0, 0, 0), pipeline_mode=once),
        ],
        out_specs=pl.BlockSpec((nhc, n_keys, tn), lambda t: (0, 0, t)),
        out_shape=jax.ShapeDtypeStruct((nhc, n_keys, n), F32),
        scratch_shapes=[pltpu.VMEM((nhc * qd, tn), BF16)],
        compiler_params=_params(("parallel",)),
        name="peer_scores",
    )(ht, wqt, keys)


def _gelu(x):
    return 0.5 * x * (1.0 + lax.erf(x * (2.0 ** -0.5)))


def _take_top(vals, k, payload=None):
    n_rows = vals.shape[0]
    rowid = lax.broadcasted_iota(jnp.int32, vals.shape, 0).astype(F32)
    cur = vals
    top, picked = [], []
    for _ in range(k):
        m = jnp.max(cur, axis=0, keepdims=True)
        sel = jnp.min(jnp.where(cur == m, rowid, float(n_rows)), axis=0, keepdims=True)
        hit = rowid == sel
        top.append(m)
        picked.append(sel if payload is None else jnp.sum(jnp.where(hit, payload, 0.0), axis=0, keepdims=True))
        cur = jnp.where(hit, -jnp.inf, cur)
    return top, picked


def _peer_pairs_kernel(s_ref, code_ref, g_ref, *, n_keys):
    s1 = s_ref[0]
    s2 = s_ref[1]
    a, ia = _take_top(s1, PEER_TOPK)
    b, ib = _take_top(s2, PEER_TOPK)
    b = jnp.concatenate(b, axis=0)
    ib = jnp.concatenate(ib, axis=0)
    counts = [PEER_TOPK // (i + 1) for i in range(PEER_TOPK)]
    cand = jnp.concatenate([a[i] + b[:c, :] for i, c in enumerate(counts)], axis=0)
    code = jnp.concatenate([ia[i] * float(n_keys) + ib[:c, :] for i, c in enumerate(counts)], axis=0)
    best, ids = _take_top(cand, PEER_TOPK, payload=code)
    ex = [jnp.exp(c - best[0]) for c in best]
    z = sum(ex)
    code_ref[...] = jnp.concatenate(ids, axis=0)
    g_ref[...] = jnp.concatenate(ex, axis=0) / z


def _peer_pairs(scores, tn):
    nhc, n_keys, n = scores.shape
    nh = nhc // 2
    blk = pl.BlockSpec((PEER_TOPK, tn), lambda t, h: (h, t))
    shape = jax.ShapeDtypeStruct((nh * PEER_TOPK, n), F32)
    return pl.pallas_call(
        functools.partial(_peer_pairs_kernel, n_keys=n_keys),
        grid=(n // tn, nh),
        in_specs=[pl.BlockSpec((2, n_keys, tn), lambda t, h: (h, 0, t))],
        out_specs=(blk, blk),
        out_shape=(shape, shape),
        compiler_params=_params(("parallel", "arbitrary")),
        name="peer_pairs",
    )(scores)


def _peer_gates_kernel(code_ref, g_ref, o_ref, i1t_ref, i2t_ref, gt_ref, *, n_keys, ib):
    code = code_ref[...].T
    i1 = jnp.floor(code * (1.0 / n_keys))
    i1t_ref[...] = i1
    i2t_ref[...] = code - i1 * float(n_keys)
    gt_ref[...] = g_ref[...].T
    tg, n_pairs = i1t_ref.shape
    key = lax.broadcasted_iota(jnp.int32, (n_keys, n_pairs), 0).astype(F32)

    def body(n, carry):
        i1r = i1t_ref[pl.ds(n, 1), :]
        i2r = i2t_ref[pl.ds(n, 1), :]
        gr = gt_ref[pl.ds(n, 1), :]
        a_t = jnp.where(key == i1r, gr, 0.0).astype(BF16)
        b_t = jnp.where(key == i2r, 1.0, 0.0).astype(BF16)
        grid_n = lax.dot_general(a_t, b_t, _NT, preferred_element_type=F32)
        for blk in range(o_ref.shape[0]):
            o_ref[blk, n] = grid_n[blk * ib:(blk + 1) * ib, :]
        return carry

    lax.fori_loop(0, tg, body, 0, unroll=32)


def _peer_gates(code, g, n_keys, tg, ib):
    n_pairs, n = code.shape
    blk = pl.BlockSpec((n_pairs, tg), lambda t: (0, t))
    body = functools.partial(_peer_gates_kernel, n_keys=n_keys, ib=ib)
    return pl.pallas_call(
        body,
        grid=(n // tg,),
        in_specs=[blk, blk],
        out_specs=pl.BlockSpec((n_keys // ib, tg, ib, n_keys), lambda t: (0, t, 0, 0)),
        out_shape=jax.ShapeDtypeStruct((n_keys // ib, n, ib, n_keys), F32),
        scratch_shapes=[pltpu.VMEM((tg, n_pairs), F32)] * 3,
        compiler_params=_params(("parallel",)),
        name="peer_gates",
    )(code, g)


def _peer_experts_kernel(h_ref, x1_ref, gate_ref, u_ref, v_ref, o_ref, *, ib):
    @pl.when(pl.program_id(1) == 0)
    def _():
        o_ref[...] = x1_ref[...]

    act = _gelu(lax.dot_general(h_ref[...], u_ref[...], _NT, preferred_element_type=F32))
    tn = h_ref.shape[0]
    gates = jnp.concatenate([gate_ref[0, pl.ds(i, tn, stride=ib), :] for i in range(ib)], axis=1)
    o_ref[...] += jnp.dot((gates * act).astype(BF16), v_ref[...], preferred_element_type=F32)


def _peer_experts(h2, x1, gates, u_bf, v_bf, tn, ib):
    n, d = h2.shape
    n_blocks, _, _, n_keys = gates.shape
    ec = ib * n_keys
    body = functools.partial(_peer_experts_kernel, ib=ib)
    return pl.pallas_call(
        body,
        grid=(n // tn, n_blocks),
        in_specs=[
            pl.BlockSpec((tn, d), lambda t, e: (t, 0)),
            pl.BlockSpec((tn, d), lambda t, e: (t, 0), pipeline_mode=pl.Buffered(1)),
            pl.BlockSpec((1, tn * ib, n_keys), lambda t, e: (e, t, 0)),
            pl.BlockSpec((ec, d), lambda t, e: (e, 0)),
            pl.BlockSpec((ec, d), lambda t, e: (e, 0)),
        ],
        out_specs=pl.BlockSpec((tn, d), lambda t, e: (t, 0)),
        out_shape=jax.ShapeDtypeStruct((n, d), F32),
        compiler_params=_params(("parallel", "arbitrary")),
        name="peer_experts",
    )(h2, x1, gates.reshape(n_blocks, n * ib, n_keys), u_bf, v_bf)


def _tile(n, pref):
    return pref if n % pref == 0 else n


def _layer(x, attend, shift0, s0, t_real, lam_init, lw):
    b, t, d = x.shape
    n = b * t
    xf = x.reshape(n, d)
    w_da = lw["w_q"].shape[1]
    tm = _tile(n, 512)
    h1 = _rms(xf, lw["norm1_g"], tm)
    tmp = _tile(n, 1024)
    qb = _proj("q", h1, lw["w_q"], tmp, _tile(w_da, 512), gain=lw["q_gain"], group=lw["dk"], scale=lw["dk"] ** -0.5)
    if t % LANES == 0:
        seqs, tmk = b, _tile(t, 1024)
    else:
        seqs, tmk = 1, n
    k, kbt = _proj_keys(h1, lw["w_k"], lw["k_gain"], lw["dk"], seqs, tmk, _tile(w_da, 512))
    v, vb = _proj("dual", h1, lw["w_v"], tmp, _tile(w_da, 512))
    p_pad = lw["w_rw"].shape[1]
    rw = _proj("plain", h1, lw["w_rw"], tmp, p_pad // 3)

    o_da = attend(qb.reshape(b, t, w_da), kbt, vb.reshape(b, t, w_da))

    rw3 = rw.reshape(b, t, p_pad)
    if t % RW_CHUNK:
        rw3 = jnp.pad(rw3, ((0, 0), (0, RW_CHUNK - t % RW_CHUNK), (0, 0)))
    o_rw, s_out = _rwkv(rw3, shift0, s0, lw["rwkv"], t_real)
    o_rw = o_rw[:, :t].reshape(n, -1)

    x1, h2, h2t = _outproj(xf, o_da.reshape(n, w_da), o_rw, lw["w_out_da"], lw["w_out_rw"], lw["norm2_g"],
                           _tile(n, 256))
    tn = _tile(n, 512)
    scores = _peer_scores(h2t, lw["wq_t"], lw["peer_keys"], tn)
    code, gate = _peer_pairs(scores, _tile(n, 1024))
    ib = 8
    gates = _peer_gates(code, gate, scores.shape[1], _tile(n, 128), ib)
    out = _peer_experts(h2, x1, gates, lw["peer_u"], lw["peer_v"], tn, ib)
    return out.reshape(b, t, d), k, v, s_out, rw3[:, t - 1, :lw["rw_proj"]]


def kernel(x_prompt, x_sample, cache_k, cache_v, state_wkv, state_shift, page_table, norm1_g, w_in, da_qk_g,
           da_lambda, da_subln_g, rw_mu, rw_w0, rw_w2, rw_a0, rw_a2, rw_g2, rw_kk, rw_ka, rw_rk, rw_lnx,
           w_out, norm2_g, peer_wq, peer_keys, peer_u, peer_v):
    depth = w_in.shape[0]
    d = x_prompt.shape[-1]
    heads, dk = cache_k.shape[3], cache_k.shape[5]
    w_da = heads * 2 * dk
    rw_heads, head = state_wkv.shape[2], state_wkv.shape[3]
    wd = rw_heads * head
    r_w, r_a, r_g = rw_w2.shape[1], rw_a2.shape[1], rw_g2.shape[1]
    rw_proj = 3 * wd + r_w + r_a + r_g
    g_pad = 2 * LANES
    p_pad = 3 * wd + LANES + g_pad
    assert d == w_da + wd and head * 2 == LANES and wd % LANES == 0 and 2 * dk == LANES
    assert r_w + r_a == LANES and r_g <= g_pad and p_pad % 3 == 0 and (p_pad // 3) % LANES == 0
    assert cache_k.shape[2] == PAGE_SIZE
    nh, _, n_keys, qd = peer_keys.shape[1:]
    bsz = x_prompt.shape[0]
    dec_b, dec_t = x_sample.shape[:2]
    assert dec_t * heads <= PAGE_SIZE and x_prompt.shape[1] % LANES == 0
    assert n_keys & (n_keys - 1) == 0 and peer_u.shape[1] == n_keys * n_keys

    xp, xs = x_prompt, x_sample
    outs = [[] for _ in range(8)]
    for l in range(depth):
        lam_init = 0.8 - 0.6 * math.exp(-0.3 * l)
        wl = w_in[l].astype(BF16)
        w_rw = jnp.pad(wl[:, 3 * w_da:], ((0, 0), (0, p_pad - rw_proj)))
        zero_wa = jnp.zeros((r_w, wd), F32)
        wwa = jnp.concatenate([jnp.concatenate([rw_w2[l], zero_wa], axis=1),
                               jnp.concatenate([zero_wa, rw_a2[l]], axis=1)], axis=0).astype(BF16)
        row = lambda z: z.reshape(1, -1)
        rwkv_prm = dict(
            mu=jnp.pad(row(rw_mu[l]), ((0, 0), (0, p_pad - rw_proj))),
            w0=row(rw_w0[l]), a0=row(rw_a0[l]), wwa=wwa,
            g2=jnp.pad(rw_g2[l], ((0, g_pad - r_g), (0, 0))).astype(BF16),
            kk=row(rw_kk[l]), ka=row(rw_ka[l]), rk=row(rw_rk[l]),
            lng=row(rw_lnx[l, 0]), lnb=row(rw_lnx[l, 1]),
            tril=jnp.tril(jnp.ones((RW_CHUNK, RW_CHUNK), BF16)),
            bd=_block_diag_ones(LANES, head), r_w=r_w, head=head)
        lw = dict(
            norm1_g=norm1_g[l], w_q=wl[:, :w_da], w_k=wl[:, w_da:2 * w_da], w_v=wl[:, 2 * w_da:3 * w_da],
            w_rw=w_rw, q_gain=jnp.tile(da_qk_g[l, 0], 2 * heads), k_gain=jnp.tile(da_qk_g[l, 1], 2 * heads),
            dk=dk, rwkv=rwkv_prm, rw_proj=rw_proj,
            w_out_da=w_out[l, :w_da].astype(BF16), w_out_rw=w_out[l, w_da:].astype(BF16),
            norm2_g=norm2_g[l], wq_t=peer_wq[l].T.astype(BF16),
            peer_keys=peer_keys[l].reshape(nh * 2, n_keys, qd).astype(BF16),
            peer_u=peer_u[l].astype(BF16), peer_v=peer_v[l].astype(BF16))

        attend_p = functools.partial(_attn_prompt, da_lambda=da_lambda[l], subln_g=da_subln_g[l], heads=heads,
                                     lam_init=lam_init, tq=_tile(x_prompt.shape[1], 512))
        n_pairs = wd // LANES
        xp, kp, vp, sp, shp = _layer(
            xp, attend_p, jnp.zeros((bsz, 1, p_pad), F32), jnp.zeros((bsz, n_pairs, LANES, LANES), F32),
            RW_CHUNK, lam_init, lw)

        attend_s = functools.partial(_attn_sample, cache_k=cache_k, cache_v=cache_v, layer=l,
                                     page_table=page_table, da_lambda=da_lambda[l], subln_g=da_subln_g[l],
                                     heads=heads, lam_init=lam_init,
                                     pp=max(c for c in (16, 8, 4, 2, 1) if page_table.shape[1] % c == 0))
        shift_s = jnp.pad(state_shift[l], ((0, 0), (0, p_pad - rw_proj))).reshape(dec_b, 1, p_pad)
        xs, ksm, vsm, ssm, shs = _layer(xs, attend_s, shift_s, _pack_state(state_wkv[l], head), dec_t, lam_init, lw)

        outs[0].append(jnp.transpose(kp.reshape(bsz, heads, 2, dk, -1), (0, 4, 1, 2, 3)))
        outs[1].append(vp.reshape(bsz, -1, heads, 2 * dk))
        outs[2].append(_unpack_state(sp, head))
        outs[3].append(shp)
        outs[4].append(jnp.transpose(ksm.reshape(heads, 2, dk, dec_b, dec_t), (3, 4, 0, 1, 2)))
        outs[5].append(vsm.reshape(dec_b, dec_t, heads, 2 * dk))
        outs[6].append(_unpack_state(ssm, head))
        outs[7].append(shs)
    return (xp, xs) + tuple(jnp.stack(o) for o in outs)
```
